```python
import functools
import jax, jax.numpy as jnp
from jax import lax
import numpy as np

D_MODEL = 1024
BATCH = 4
SEQ = 4096
DEPTH = 1
DEC_BATCH = 128
DEC_SEQ = 1
PAST_LEN = 2048
PAGE_SIZE = 128

HEAD_DIM = 64
ATT_GROUPS = ((128, 1), (512, 4), (2048, 16))
N_GROUPS = len(ATT_GROUPS)
HEADS_PER_GROUP = 4
ATT_WIDTH = N_GROUPS * HEADS_PER_GROUP * HEAD_DIM
ATT_OUT = HEADS_PER_GROUP * HEAD_DIM
BLOCK = 128
RWKV_HEADS = 8
RWKV_HEAD = 64
RWKV_WIDTH = RWKV_HEADS * RWKV_HEAD
DECAY_LORA = 64
ICLR_LORA = 64
GATE_LORA = 128
RWKV_PROJ = 3 * RWKV_WIDTH + DECAY_LORA + ICLR_LORA + GATE_LORA
P_TOTAL = 3 * ATT_WIDTH + RWKV_PROJ + 2 * D_MODEL
D_FF = 2816
CONV_W = 3
RMS_EPS = 1e-6
GN_EPS = 64e-5
L2_EPS = 1e-12

kernel_name = 'hybrid_dilated_attn_rwkv7_convffn_step'


def rms_norm(x, g):
    xf = x.astype(jnp.float32)
    y = xf * lax.rsqrt(jnp.mean(xf * xf, axis=-1, keepdims=True) + RMS_EPS)
    return (y * g.astype(jnp.float32)).astype(x.dtype)


def dilated_band_attention(q, k, v, dil, n_step):
    b, s, h, dh = q.shape
    seg = dil * BLOCK
    sp = -(-s // seg) * seg
    nb = sp // seg

    def to_res(t):
        t = jnp.pad(t, ((0, 0), (0, sp - s), (0, 0), (0, 0))).reshape(b, nb * BLOCK, dil, h, dh)
        return t.transpose(0, 2, 1, 3, 4).reshape(b, dil, nb, BLOCK, h, dh)

    def with_prev(t):
        prev = jnp.pad(t, ((0, 0), (0, 0), (1, 0), (0, 0), (0, 0), (0, 0)))[:, :, :-1]
        return jnp.concatenate([prev, t], axis=3)

    qb = to_res(q).astype(jnp.float32)
    kb = with_prev(to_res(k)).astype(jnp.float32)
    vb = with_prev(to_res(v)).astype(jnp.float32)
    scores = jnp.einsum('brnqhd,brnkhd->brnhqk', qb, kb) * (dh ** -0.5)
    qi = jnp.arange(BLOCK)[:, None]
    kj = jnp.arange(2 * BLOCK)[None, :]
    rel = qi + BLOCK - kj
    band = (rel >= 0) & (rel <= n_step)
    blk = jnp.arange(nb)[:, None, None]
    mask = band[None] & ((blk > 0) | (kj[None] >= BLOCK))
    scores = jnp.where(mask[None, None, :, None], scores, -jnp.inf)
    lse = jax.nn.logsumexp(scores, axis=-1)
    o = jnp.einsum('brnhqk,brnkhd->brnqhd', jnp.exp(scores - lse[..., None]), vb)
    o = o.reshape(b, dil, nb * BLOCK, h, dh).transpose(0, 2, 1, 3, 4).reshape(b, sp, h, dh)[:, :s]
    lse = lse.transpose(0, 1, 2, 4, 3).reshape(b, dil, nb * BLOCK, h).transpose(0, 2, 1, 3).reshape(b, sp, h)[:, :s]
    return o, lse


def dilated_cached_attention(q, k_all, v_all, n_past, dil, n_step):
    t, dh = q.shape[1], q.shape[-1]
    idx = n_past + jnp.arange(t)[:, None] - dil * jnp.arange(n_step + 1)[None, :]
    valid = idx >= 0
    idx = jnp.maximum(idx, 0)
    kg = k_all[:, idx].astype(jnp.float32)
    vg = v_all[:, idx].astype(jnp.float32)
    scores = jnp.einsum('bqhd,bqjhd->bqhj', q.astype(jnp.float32), kg) * (dh ** -0.5)
    scores = jnp.where(valid[None, :, None, :], scores, -jnp.inf)
    lse = jax.nn.logsumexp(scores, axis=-1)
    o = jnp.einsum('bqhj,bqjhd->bqhd', jnp.exp(scores - lse[..., None]), vg)
    return o, lse


def merge_groups(outs, lses):
    w = jax.nn.softmax(jnp.stack(lses), axis=0)
    o = jnp.sum(w[..., None] * jnp.stack(outs), axis=0)
    return o.reshape(o.shape[0], o.shape[1], ATT_OUT)


def attend_prompt(q, k, v):
    t = q.shape[1]
    outs, lses, k_rows, v_rows = [], [], [], []
    for gi, (win, dil) in enumerate(ATT_GROUPS):
        o, l = dilated_band_attention(q[:, :, gi], k[:, :, gi], v[:, :, gi], dil, win // dil)
        outs.append(o)
        lses.append(l)
        keep = min(win, t)
        k_rows.append(k[:, t - keep:, gi])
        v_rows.append(v[:, t - keep:, gi])
    return merge_groups(outs, lses), k_rows, v_rows


def attend_cached(q, k, v, k_bufs, v_bufs):
    t = q.shape[1]
    outs, lses, k_rows, v_rows = [], [], [], []
    for gi, (win, dil) in enumerate(ATT_GROUPS):
        n_past = k_bufs[gi].shape[1]
        k_all = jnp.concatenate([k_bufs[gi].astype(k.dtype), k[:, :, gi]], axis=1)
        v_all = jnp.concatenate([v_bufs[gi].astype(v.dtype), v[:, :, gi]], axis=1)
        o, l = dilated_cached_attention(q[:, :, gi], k_all, v_all, n_past, dil, win // dil)
        outs.append(o)
        lses.append(l)
        keep = min(win, n_past + t)
        k_rows.append(k_all[:, n_past + t - keep:])
        v_rows.append(v_all[:, n_past + t - keep:])
    return merge_groups(outs, lses), k_rows, v_rows


def rwkv7_time_mix(pr, shift_prev, wkv0, mu, w0, w2, a0, a2, g2, k_k, k_a, r_k, gn_g, gn_b):
    b, t, _ = pr.shape
    f32 = jnp.float32
    p = pr.astype(f32)
    prev = jnp.concatenate([shift_prev[:, None].astype(f32), p[:, :-1]], axis=1)
    xs = p + mu.astype(f32) * (prev - p)
    cuts = [RWKV_WIDTH, 2 * RWKV_WIDTH, 3 * RWKV_WIDTH, 3 * RWKV_WIDTH + DECAY_LORA, 3 * RWKV_WIDTH + DECAY_LORA + ICLR_LORA]
    r, k, v, wl, al, gl = jnp.split(xs, cuts, axis=-1)
    w = -jax.nn.softplus(-(w0 + jnp.tanh(wl) @ w2)) - 0.5
    decay = jnp.exp(-jnp.exp(w))
    a = jax.nn.sigmoid(a0 + al @ a2)
    g = jax.nn.sigmoid(gl) @ g2

    def heads(z):
        return z.reshape(b, t, RWKV_HEADS, RWKV_HEAD)

    kk = heads(k * k_k)
    kk = kk / jnp.maximum(jnp.sqrt(jnp.sum(kk * kk, axis=-1, keepdims=True)), L2_EPS)
    k = k * (1.0 + (a - 1.0) * k_a)
    rh, kh, vh, ah, wh = heads(r), heads(k), heads(v), heads(a), heads(decay)
    seq = tuple(z.transpose(1, 0, 2, 3) for z in (rh, wh, kh, vh, -kk, kk * ah))

    def step(S, inp):
        r_t, w_t, k_t, v_t, na_t, nb_t = inp
        sa = jnp.einsum('bhij,bhj->bhi', S, na_t)
        S = S * w_t[:, :, None, :] + sa[..., None] * nb_t[:, :, None, :] + v_t[..., None] * k_t[:, :, None, :]
        return S, jnp.einsum('bhij,bhj->bhi', S, r_t)

    S, ys = lax.scan(step, wkv0.astype(f32), seq)
    y = ys.transpose(1, 0, 2, 3)
    mean = jnp.mean(y, axis=-1, keepdims=True)
    var = jnp.mean(jnp.square(y - mean), axis=-1, keepdims=True)
    yn = ((y - mean) * lax.rsqrt(var + GN_EPS)).reshape(b, t, RWKV_WIDTH) * gn_g + gn_b
    bonus = jnp.sum(rh * kh * r_k, axis=-1, keepdims=True) * vh
    out = (yn + bonus.reshape(b, t, RWKV_WIDTH)) * g
    return out.astype(pr.dtype), pr[:, -1], S


def decoder_layer(x, attend, shift_prev, wkv0, conv_prev, lw):
    b, t, _ = x.shape
    xn = rms_norm(x, lw['ln1_g'])
    proj = xn @ lw['w_in']
    c0 = 3 * ATT_WIDTH + RWKV_PROJ
    q, k, v, pr, g_att, g_rwkv = jnp.split(proj, [ATT_WIDTH, 2 * ATT_WIDTH, 3 * ATT_WIDTH, c0, c0 + D_MODEL], axis=-1)
    hs = (b, t, N_GROUPS, HEADS_PER_GROUP, HEAD_DIM)
    q = rms_norm(q.reshape(hs), lw['q_norm_g'])
    k = rms_norm(k.reshape(hs), lw['k_norm_g'])
    v = v.reshape(hs)
    o_att, k_rows, v_rows = attend(q, k, v)
    y_att = o_att.astype(x.dtype) @ lw['w_attn_out']
    o_rwkv, shift_last, wkv_last = rwkv7_time_mix(
        pr, shift_prev, wkv0, lw['rwkv_mu'], lw['rwkv_w0'], lw['rwkv_w2'], lw['rwkv_a0'], lw['rwkv_a2'],
        lw['rwkv_g2'], lw['rwkv_k_k'], lw['rwkv_k_a'], lw['rwkv_r_k'], lw['rwkv_gn_g'], lw['rwkv_gn_b'])
    y_rwkv = o_rwkv @ lw['w_rwkv_out']
    merged = jax.nn.sigmoid(g_att) * y_att + jax.nn.sigmoid(g_rwkv) * y_rwkv
    x = x + merged @ lw['w_o']
    u = rms_norm(x, lw['ln2_g']) @ lw['w_up']
    u_ext = jnp.concatenate([conv_prev.astype(u.dtype), u], axis=1)
    cw = lw['conv_w']
    c = lw['conv_b']
    for i in range(CONV_W):
        c = c + u_ext[:, i:i + t] * cw[i]
    gate, val = jnp.split(c, [D_FF], axis=-1)
    x = x + (jax.nn.silu(gate) * val) @ lw['w_down']
    return x, k_rows, v_rows, shift_last, wkv_last, u_ext[:, t:]


def setup_inputs(seed: int = 0) -> dict:
    key = jax.random.key(seed)
    ks = iter(jax.random.split(key, 48))
    f32 = jnp.float32

    def nrm(shape, scale=1.0):
        return scale * jax.random.normal(next(ks), shape, f32)

    def gain(shape):
        return 1.0 + 0.05 * jax.random.normal(next(ks), shape, f32)

    def kv_shape(win):
        return (DEPTH, DEC_BATCH, min(win, PAST_LEN), HEADS_PER_GROUP, HEAD_DIM)

    return {
        'x_prompt': nrm((BATCH, SEQ, D_MODEL)),
        'x_sample': nrm((DEC_BATCH, DEC_SEQ, D_MODEL)),
        'cache_k_w128': nrm(kv_shape(128)),
        'cache_v_w128': nrm(kv_shape(128)),
        'cache_k_w512': nrm(kv_shape(512)),
        'cache_v_w512': nrm(kv_shape(512)),
        'cache_k_w2048': nrm(kv_shape(2048)),
        'cache_v_w2048': nrm(kv_shape(2048)),
        'state_rwkv_shift': nrm((DEPTH, DEC_BATCH, RWKV_PROJ)),
        'state_rwkv_wkv': nrm((DEPTH, DEC_BATCH, RWKV_HEADS, RWKV_HEAD, RWKV_HEAD), 0.3),
        'state_ffn_conv': nrm((DEPTH, DEC_BATCH, CONV_W - 1, 2 * D_FF)),
        'ln1_g': gain((DEPTH, D_MODEL)),
        'w_in': nrm((DEPTH, D_MODEL, P_TOTAL), D_MODEL ** -0.5),
        'q_norm_g': gain((DEPTH, HEAD_DIM)),
        'k_norm_g': gain((DEPTH, HEAD_DIM)),
        'w_attn_out': nrm((DEPTH, ATT_OUT, D_MODEL), ATT_OUT ** -0.5),
        'rwkv_mu': jax.random.uniform(next(ks), (DEPTH, RWKV_PROJ), f32),
        'rwkv_w0': 0.5 + nrm((DEPTH, RWKV_WIDTH), 0.5),
        'rwkv_w2': nrm((DEPTH, DECAY_LORA, RWKV_WIDTH), 0.5 * DECAY_LORA ** -0.5),
        'rwkv_a0': nrm((DEPTH, RWKV_WIDTH), 0.1),
        'rwkv_a2': nrm((DEPTH, ICLR_LORA, RWKV_WIDTH), 0.5 * ICLR_LORA ** -0.5),
        'rwkv_g2': nrm((DEPTH, GATE_LORA, RWKV_WIDTH), GATE_LORA ** -0.5),
        'rwkv_k_k': 0.85 + nrm((DEPTH, RWKV_WIDTH), 0.05),
        'rwkv_k_a': gain((DEPTH, RWKV_WIDTH)),
        'rwkv_r_k': nrm((DEPTH, RWKV_HEADS, RWKV_HEAD), 0.1),
        'rwkv_gn_g': gain((DEPTH, RWKV_WIDTH)),
        'rwkv_gn_b': nrm((DEPTH, RWKV_WIDTH), 0.02),
        'w_rwkv_out': nrm((DEPTH, RWKV_WIDTH, D_MODEL), RWKV_WIDTH ** -0.5),
        'w_o': nrm((DEPTH, D_MODEL, D_MODEL), D_MODEL ** -0.5),
        'ln2_g': gain((DEPTH, D_MODEL)),
        'w_up': nrm((DEPTH, D_MODEL, 2 * D_FF), D_MODEL ** -0.5),
        'conv_w': nrm((DEPTH, CONV_W, 2 * D_FF), CONV_W ** -0.5),
        'conv_b': nrm((DEPTH, 2 * D_FF), 0.02),
        'w_down': nrm((DEPTH, D_FF, D_MODEL), D_FF ** -0.5),
    }


def reference(x_prompt, x_sample, cache_k_w128, cache_v_w128, cache_k_w512, cache_v_w512, cache_k_w2048, cache_v_w2048,
              state_rwkv_shift, state_rwkv_wkv, state_ffn_conv, ln1_g, w_in, q_norm_g, k_norm_g, w_attn_out,
              rwkv_mu, rwkv_w0, rwkv_w2, rwkv_a0, rwkv_a2, rwkv_g2, rwkv_k_k, rwkv_k_a, rwkv_r_k, rwkv_gn_g, rwkv_gn_b,
              w_rwkv_out, w_o, ln2_g, w_up, conv_w, conv_b, w_down):
    k_caches = (cache_k_w128, cache_k_w512, cache_k_w2048)
    v_caches = (cache_v_w128, cache_v_w512, cache_v_w2048)
    pk = [[] for _ in ATT_GROUPS]
    pv = [[] for _ in ATT_GROUPS]
    sk = [[] for _ in ATT_GROUPS]
    sv = [[] for _ in ATT_GROUPS]
    p_shift, s_shift, p_wkv, s_wkv, p_conv, s_conv = [], [], [], [], [], []
    xp, xs = x_prompt, x_sample
    bp = xp.shape[0]
    for l in range(DEPTH):
        lw = dict(ln1_g=ln1_g[l], w_in=w_in[l], q_norm_g=q_norm_g[l], k_norm_g=k_norm_g[l], w_attn_out=w_attn_out[l],
                  rwkv_mu=rwkv_mu[l], rwkv_w0=rwkv_w0[l], rwkv_w2=rwkv_w2[l], rwkv_a0=rwkv_a0[l], rwkv_a2=rwkv_a2[l],
                  rwkv_g2=rwkv_g2[l], rwkv_k_k=rwkv_k_k[l], rwkv_k_a=rwkv_k_a[l], rwkv_r_k=rwkv_r_k[l],
                  rwkv_gn_g=rwkv_gn_g[l], rwkv_gn_b=rwkv_gn_b[l], w_rwkv_out=w_rwkv_out[l], w_o=w_o[l],
                  ln2_g=ln2_g[l], w_up=w_up[l], conv_w=conv_w[l], conv_b=conv_b[l], w_down=w_down[l])
        xp, kr, vr, sh, wkv, cv = decoder_layer(
            xp, attend_prompt,
            jnp.zeros((bp, RWKV_PROJ), xp.dtype),
            jnp.zeros((bp, RWKV_HEADS, RWKV_HEAD, RWKV_HEAD), jnp.float32),
            jnp.zeros((bp, CONV_W - 1, 2 * D_FF), xp.dtype), lw)
        for gi in range(N_GROUPS):
            pk[gi].append(kr[gi])
            pv[gi].append(vr[gi])
        p_shift.append(sh)
        p_wkv.append(wkv)
        p_conv.append(cv)
        attend_s = functools.partial(attend_cached, k_bufs=tuple(c[l] for c in k_caches),
                                     v_bufs=tuple(c[l] for c in v_caches))
        xs, kr, vr, sh, wkv, cv = decoder_layer(xs, attend_s, state_rwkv_shift[l], state_rwkv_wkv[l],
                                                state_ffn_conv[l], lw)
        for gi in range(N_GROUPS):
            sk[gi].append(kr[gi])
            sv[gi].append(vr[gi])
        s_shift.append(sh)
        s_wkv.append(wkv)
        s_conv.append(cv)
    st = functools.partial(jnp.stack, axis=0)
    return (xp, xs,
            st(pk[0]), st(sk[0]), st(pv[0]), st(sv[0]),
            st(pk[1]), st(sk[1]), st(pv[1]), st(sv[1]),
            st(pk[2]), st(sk[2]), st(pv[2]), st(sv[2]),
            st(p_shift), st(s_shift), st(p_wkv), st(s_wkv), st(p_conv), st(s_conv))
```

```python
import functools

import jax
import jax.numpy as jnp
from jax import lax
from jax.experimental import pallas as pl
from jax.experimental.pallas import tpu as pltpu

F32 = jnp.float32
BF16 = jnp.bfloat16

D_MODEL = 1024
HEAD_DIM = 64
ATT_GROUPS = ((128, 1), (512, 4), (2048, 16))
N_GROUPS = len(ATT_GROUPS)
HEADS_PER_GROUP = 4
GROUP_WIDTH = HEADS_PER_GROUP * HEAD_DIM
ATT_WIDTH = N_GROUPS * GROUP_WIDTH
BLOCK = 128
SEG = 2048
RWKV_HEADS = 8
RWKV_HEAD = 64
RWKV_WIDTH = RWKV_HEADS * RWKV_HEAD
DECAY_LORA = 64
ICLR_LORA = 64
GATE_LORA = 128
RWKV_PROJ = 3 * RWKV_WIDTH + DECAY_LORA + ICLR_LORA + GATE_LORA
LORA_OFF = 3 * RWKV_WIDTH
GATE_OFF = LORA_OFF + DECAY_LORA + ICLR_LORA
PR_OFF = 3 * ATT_WIDTH
GATES_OFF = PR_OFF + RWKV_PROJ
P_TOTAL = GATES_OFF + 2 * D_MODEL
D_FF = 2816
FF_CHUNK = 256
RMS_EPS = 1e-6
GN_EPS = 64e-5
L2_EPS = 1e-12
NEG_BIG = -1e30

LANES = 128
WKV_Q = 4
WKV_SUB = RWKV_HEAD // WKV_Q
WKV_TB = 64
VMEM_LIMIT = 56 * 1024 * 1024


def _cparams(*sem):
    return pltpu.CompilerParams(dimension_semantics=sem, vmem_limit_bytes=VMEM_LIMIT)


def _resident(shape):
    n = len(shape)
    return pl.BlockSpec(shape, lambda *_: (0,) * n, pipeline_mode=pl.Buffered(1))


def _split_bf16(x):
    hi = x.astype(BF16)
    lo = (x - hi.astype(F32)).astype(BF16)
    return hi, lo


def _dot(a, b):
    return jnp.dot(a, b, preferred_element_type=F32)


def _dot3(a, b_hi, b_lo):
    a_hi, a_lo = _split_bf16(a)
    return _dot(a_hi, b_hi) + _dot(a_lo, b_hi) + _dot(a_hi, b_lo)


def _seg_sum(x, ones_bd):
    hi, lo = _split_bf16(x)
    return _dot(hi, ones_bd) + _dot(lo, ones_bd)


def _sigmoid(x):
    return 1.0 / (1.0 + jnp.exp(-x))


def _block_diag_ones(width):
    idx = jnp.arange(width) // HEAD_DIM
    return (idx[:, None] == idx[None, :]).astype(BF16)


def _proj_kernel(x_ref, g1_ref, w_ref, bd_ref, gq_ref, gk_ref, q_ref, k_ref, v_ref, pr_ref, gate_ref):
    x = x_ref[...]
    ms = jnp.mean(x * x, axis=-1, keepdims=True)
    xn = (x * lax.rsqrt(ms + RMS_EPS) * g1_ref[...]).astype(BF16)

    def proj(c0, c1):
        return _dot(xn, w_ref[:, c0:c1])

    def head_norm(z, g):
        ss = _dot((z * z).astype(BF16), bd_ref[...])
        return z * lax.rsqrt(ss * (1.0 / HEAD_DIM) + RMS_EPS) * g

    q_ref[...] = head_norm(proj(0, ATT_WIDTH), gq_ref[...])
    k_ref[...] = head_norm(proj(ATT_WIDTH, 2 * ATT_WIDTH), gk_ref[...])
    v_ref[...] = proj(2 * ATT_WIDTH, PR_OFF)
    pr_ref[...] = proj(PR_OFF, GATES_OFF)
    gate_ref[...] = proj(GATES_OFF, P_TOTAL)


def _proj(x2d, ln1_g, w_in_bf, bd768, gq, gk, tm):
    m = x2d.shape[0]
    row = lambda w: pl.BlockSpec((tm, w), lambda i: (i, 0))
    widths = (ATT_WIDTH, ATT_WIDTH, ATT_WIDTH, RWKV_PROJ, 2 * D_MODEL)
    return pl.pallas_call(
        _proj_kernel,
        out_shape=tuple(jax.ShapeDtypeStruct((m, w), F32) for w in widths),
        grid=(m // tm,),
        in_specs=[row(D_MODEL), _resident((1, D_MODEL)), _resident((D_MODEL, P_TOTAL)),
                  _resident((ATT_WIDTH, ATT_WIDTH)), _resident((1, ATT_WIDTH)), _resident((1, ATT_WIDTH))],
        out_specs=tuple(row(w) for w in widths),
        compiler_params=_cparams("parallel"),
        name="proj",
    )(x2d, ln1_g, w_in_bf, bd768, gq, gk)


def _attn_prompt_kernel(*refs, dil):
    q_ref, kp_ref, kc_ref, vp_ref, vc_ref = (refs[2 * i:2 * i + 2] for i in range(5))
    o_ref, lse_ref = refs[10], refs[11]
    not_first_seg = pl.program_id(1) > 0
    nblk = SEG // (BLOCK * dil)
    stacked = (HEADS_PER_GROUP * BLOCK, 2 * BLOCK)
    row = lax.broadcasted_iota(jnp.int32, stacked, 0) % BLOCK
    col = lax.broadcasted_iota(jnp.int32, stacked, 1)
    in_prev = col < BLOCK
    band_prev = in_prev & (col >= row)
    band_cur = jnp.logical_not(in_prev) & (col - BLOCK <= row)
    lane = lax.broadcasted_iota(jnp.int32, (1, GROUP_WIDTH), 1)
    head_masks = [(lane // HEAD_DIM) == h for h in range(HEADS_PER_GROUP)]

    def row_slice(start):
        return pl.ds(start, BLOCK) if dil == 1 else pl.ds(start, BLOCK, stride=dil)

    def rows(halves, start):
        return jnp.concatenate([h[row_slice(start), :] for h in halves], axis=1)

    def block(q_start, k1_ref, k1_start, v1_ref, prev_valid):
        q = rows(q_ref, q_start) * (HEAD_DIM ** -0.5)
        q4 = jnp.concatenate([jnp.where(hm, q, 0.0) for hm in head_masks], axis=0).astype(BF16)
        kt = jnp.concatenate([rows(k1_ref, k1_start), rows(kc_ref, q_start)], axis=0).astype(BF16)
        vt = jnp.concatenate([rows(v1_ref, k1_start), rows(vc_ref, q_start)], axis=0).astype(BF16)
        s = lax.dot_general(q4, kt, (((1,), (1,)), ((), ())), preferred_element_type=F32)
        s = jnp.where(band_cur | (band_prev & prev_valid), s, NEG_BIG)
        m = jnp.max(s, axis=-1, keepdims=True)
        p = jnp.exp(s - m)
        l = jnp.sum(p, axis=-1, keepdims=True)
        ov = _dot(p.astype(BF16), vt) / l
        lse = m + jnp.log(l)
        o = jnp.zeros((BLOCK, GROUP_WIDTH), F32)
        lse_b = jnp.zeros((BLOCK, GROUP_WIDTH), F32)
        for h, hm in enumerate(head_masks):
            sl = slice(h * BLOCK, (h + 1) * BLOCK)
            o = jnp.where(hm, ov[sl], o)
            lse_b = jnp.where(hm, lse[sl], lse_b)
        for half in range(2):
            lanes = slice(half * LANES, (half + 1) * LANES)
            o_ref[half, row_slice(q_start), :] = o[:, lanes]
            lse_ref[half, row_slice(q_start), :] = lse_b[:, lanes]

    def first_blocks(r, carry):
        block(r, kp_ref, SEG - BLOCK * dil + r, vp_ref, not_first_seg)
        return carry

    def later_blocks(i, carry):
        r = i % dil
        nb = 1 + i // dil
        q_start = nb * BLOCK * dil + r
        block(q_start, kc_ref, q_start - BLOCK * dil, vc_ref, True)
        return carry

    lax.fori_loop(0, dil, first_blocks, 0)
    if nblk > 1:
        lax.fori_loop(0, dil * (nblk - 1), later_blocks, 0)


def _attn_prompt(qn, kn, v, group, batch, seq):
    dil = ATT_GROUPS[group][1]
    nseg = seq // SEG
    halves = GROUP_WIDTH // LANES
    cur = [pl.BlockSpec((SEG, LANES), lambda b, s, c=halves * group + h: (b * nseg + s, c)) for h in range(halves)]
    prev = [pl.BlockSpec((SEG, LANES), lambda b, s, c=halves * group + h: (b * nseg + jnp.maximum(s - 1, 0), c))
            for h in range(halves)]
    out = pl.BlockSpec((halves, SEG, LANES), lambda b, s: (0, b * nseg + s, 0))
    shape = jax.ShapeDtypeStruct((halves, batch * seq, LANES), F32)
    return pl.pallas_call(
        functools.partial(_attn_prompt_kernel, dil=dil),
        out_shape=(shape, shape),
        grid=(batch, nseg),
        in_specs=cur + prev + cur + prev + cur,
        out_specs=(out, out),
        compiler_params=_cparams("parallel", "arbitrary"),
        name=f"attn_prompt_g{group}",
    )(qn, qn, kn, kn, kn, kn, v, v, v, v)


def _cache_attn_kernel(q_ref, kn_ref, vn_ref, k_lo_ref, k_hi_ref, v_lo_ref, v_hi_ref, hsel_ref, hexp_ref,
                       ko_ref, vo_ref, o_ref, lse_ref, *, group, dil, bb):
    hsel = hsel_ref[...]
    hexp = hexp_ref[...]
    win = k_lo_ref.shape[1]
    cols = slice(group * GROUP_WIDTH, (group + 1) * GROUP_WIDTH)
    last_row = lax.broadcasted_iota(jnp.int32, (win, LANES), 0) == win - 1

    def head_sum(x):
        hi, lo = _split_bf16(x)
        return _dot(hi, hsel) + _dot(lo, hsel)

    def head_spread(x):
        hi, lo = _split_bf16(x)
        return _dot(hi, hexp) + _dot(lo, hexp)

    def window(halves, b):
        if dil == 1:
            return jnp.concatenate([h[b] for h in halves], axis=1)
        return jnp.concatenate([h[b, pl.ds(0, BLOCK, stride=dil), :] for h in halves], axis=1)

    def shifted(halves, new, dst, b):
        for i, h in enumerate(halves):
            lanes = slice(i * LANES, (i + 1) * LANES)
            dst[b, :, lanes] = jnp.where(last_row, new[:, lanes], pltpu.roll(h[b], win - 1, 0))

    for b in range(bb):
        n = pl.program_id(0) * bb + b
        q = q_ref[pl.ds(n, 1), cols] * (HEAD_DIM ** -0.5)
        k_new = kn_ref[pl.ds(n, 1), cols]
        v_new = vn_ref[pl.ds(n, 1), cols]
        s = head_sum(window((k_lo_ref, k_hi_ref), b) * q)
        s_new = head_sum(k_new * q)
        m = jnp.maximum(jnp.max(s, axis=0, keepdims=True), s_new)
        p = jnp.exp(s - m)
        p_new = jnp.exp(s_new - m)
        l = jnp.sum(p, axis=0, keepdims=True) + p_new
        acc = (jnp.sum(head_spread(p) * window((v_lo_ref, v_hi_ref), b), axis=0, keepdims=True)
               + head_spread(p_new) * v_new)
        o = acc / head_spread(l)
        lse = head_spread(m + jnp.log(l))
        for half in range(GROUP_WIDTH // LANES):
            lanes = slice(half * LANES, (half + 1) * LANES)
            o_ref[half, pl.ds(n, 1), :] = o[:, lanes]
            lse_ref[half, pl.ds(n, 1), :] = lse[:, lanes]
        shifted((k_lo_ref, k_hi_ref), k_new, ko_ref, b)
        shifted((v_lo_ref, v_hi_ref), v_new, vo_ref, b)


def _cache_attn(qn, kn, v, k_cache, v_cache, hsel, hexp, group, bb):
    n, win, _ = k_cache.shape
    dil = ATT_GROUPS[group][1]
    halves = GROUP_WIDTH // LANES
    half = [pl.BlockSpec((bb, win, LANES), lambda i, c=c: (i, 0, c)) for c in range(halves)]
    full = pl.BlockSpec((bb, win, GROUP_WIDTH), lambda i: (i, 0, 0))
    rows = _resident((n, ATT_WIDTH))
    out = pl.BlockSpec((halves, n, LANES), lambda i: (0, 0, 0))
    cache_shape = jax.ShapeDtypeStruct(k_cache.shape, F32)
    o_shape = jax.ShapeDtypeStruct((halves, n, LANES), F32)
    return pl.pallas_call(
        functools.partial(_cache_attn_kernel, group=group, dil=dil, bb=bb),
        out_shape=(cache_shape, cache_shape, o_shape, o_shape),
        grid=(n // bb,),
        in_specs=[rows, rows, rows] + half + half
                 + [_resident((GROUP_WIDTH, LANES)), _resident((LANES, GROUP_WIDTH))],
        out_specs=(full, full, out, out),
        compiler_params=_cparams("arbitrary"),
        name=f"cache_attn_g{group}",
    )(qn, kn, v, k_cache, k_cache, v_cache, v_cache, hsel, hexp)


def _prep_math(p, prev, mu_ref, w0_ref, a0_ref, w2h_ref, w2l_ref, a2h_ref, a2l_ref, g2h_ref, g2l_ref,
               kk_ref, ka_ref, rk_ref, bd_ref, outs):
    na_ref, dec_ref, nb_ref, kp_ref, rp_ref, v_ref, vkr_ref, bonus_ref, g_ref = outs
    bd = bd_ref[...]
    xs = p + mu_ref[...] * (prev - p)
    r = xs[:, 0:RWKV_WIDTH]
    k = xs[:, RWKV_WIDTH:2 * RWKV_WIDTH]
    v = xs[:, 2 * RWKV_WIDTH:3 * RWKV_WIDTH]
    lora = xs[:, LORA_OFF:GATE_OFF]
    gl = xs[:, GATE_OFF:RWKV_PROJ]
    z = w0_ref[...] + _dot3(jnp.tanh(lora), w2h_ref[...], w2l_ref[...])
    softplus_neg = jnp.maximum(-z, 0.0) + jnp.log(1.0 + jnp.exp(-jnp.abs(z)))
    w = -softplus_neg - 0.5
    dec = jnp.exp(-jnp.exp(w))
    a = _sigmoid(a0_ref[...] + _dot3(lora, a2h_ref[...], a2l_ref[...]))
    g = _dot3(_sigmoid(gl), g2h_ref[...], g2l_ref[...])
    kk = k * kk_ref[...]
    kk = kk / jnp.maximum(jnp.sqrt(_seg_sum(kk * kk, bd)), L2_EPS)
    kp = k * (1.0 + (a - 1.0) * ka_ref[...])
    na = -kk
    nb = kk * a
    rp = dec * r + na * _seg_sum(nb * r, bd)
    na_ref[...] = na
    dec_ref[...] = dec
    nb_ref[...] = nb
    kp_ref[...] = kp
    rp_ref[...] = rp
    v_ref[...] = v
    vkr_ref[...] = v * _seg_sum(kp * r, bd)
    bonus_ref[...] = _seg_sum(r * kp * rk_ref[...], bd) * v
    g_ref[...] = g


def _prep_seq_kernel(p_ref, *rest):
    weights, outs, carry_ref = rest[:13], rest[13:22], rest[22]

    @pl.when(pl.program_id(1) == 0)
    def _():
        carry_ref[...] = jnp.zeros_like(carry_ref)

    p = p_ref[...]
    tm = p.shape[0]
    row = lax.broadcasted_iota(jnp.int32, p.shape, 0)
    prev = jnp.where(row == 0, carry_ref[7:8, :], pltpu.roll(p, 1, 0))
    carry_ref[7:8, :] = p[tm - 1:tm, :]
    _prep_math(p, prev, *weights, outs)


def _prep_step_kernel(p_ref, prev_ref, *rest):
    weights, outs = rest[:13], rest[13:22]
    _prep_math(p_ref[...], prev_ref[...], *weights, outs)


def _prep(pr, shift_prev, weights, batch, seq, tm):
    m = pr.shape[0]
    w_specs = [_resident(w.shape) for w in weights]
    out_shape = tuple(jax.ShapeDtypeStruct((m, RWKV_WIDTH), F32) for _ in range(9))
    if shift_prev is None:
        nt = seq // tm
        row = lambda w: pl.BlockSpec((tm, w), lambda b, t: (b * nt + t, 0))
        return pl.pallas_call(
            _prep_seq_kernel, out_shape=out_shape, grid=(batch, nt),
            in_specs=[row(RWKV_PROJ)] + w_specs, out_specs=tuple(row(RWKV_WIDTH) for _ in range(9)),
            scratch_shapes=[pltpu.VMEM((8, RWKV_PROJ), F32)],
            compiler_params=_cparams("parallel", "arbitrary"), name="rwkv_prep_seq",
        )(pr, *weights)
    row = lambda w: pl.BlockSpec((tm, w), lambda i: (i, 0))
    return pl.pallas_call(
        _prep_step_kernel, out_shape=out_shape, grid=(m // tm,),
        in_specs=[row(RWKV_PROJ), row(RWKV_PROJ)] + w_specs, out_specs=tuple(row(RWKV_WIDTH) for _ in range(9)),
        compiler_params=_cparams("parallel"), name="rwkv_prep_step",
    )(pr, shift_prev, *weights)


def _wkv_seq_kernel(w_ref, nb_ref, k_ref, na_ref, rp_ref, v_ref, y_ref, sout_ref, s_ref, e_ref, acc_ref):
    @pl.when(pl.program_id(0) == 0)
    def _():
        s_ref[...] = jnp.zeros_like(s_ref)
        acc_ref[...] = jnp.zeros_like(acc_ref)

    quarter = lax.broadcasted_iota(jnp.int32, (WKV_SUB, LANES), 1) // (LANES // WKV_Q)
    qmask = [quarter == q for q in range(WKV_Q)]

    def expand(i, xc):
        rolled = [xc] + [pltpu.roll(xc, (LANES // WKV_Q) * s, 1) for s in range(1, WKV_Q)]
        for q in range(WKV_Q):
            out = rolled[(0 - q) % WKV_Q]
            for pq in range(1, WKV_Q):
                out = jnp.where(qmask[pq], rolled[(pq - q) % WKV_Q], out)
            e_ref[i, q * WKV_SUB:(q + 1) * WKV_SUB, :] = out

    def step(t, carry):
        sa, y = carry
        y_ref[t] = y
        for i, ref in enumerate((w_ref, nb_ref, k_ref, na_ref, rp_ref)):
            expand(i, ref[t])
        v = v_ref[t]
        n_acc = 4
        sa_acc = [jnp.zeros((WKV_SUB, LANES), F32) for _ in range(n_acc)]
        y_acc = [jnp.zeros((WKV_SUB, LANES), F32) for _ in range(n_acc)]
        for j in range(RWKV_HEAD):
            sn = s_ref[j] * e_ref[0, j:j + 1, :] + sa * e_ref[1, j:j + 1, :] + v * e_ref[2, j:j + 1, :]
            s_ref[j] = sn
            sa_acc[j % n_acc] = sa_acc[j % n_acc] + sn * e_ref[3, j:j + 1, :]
            y_acc[j % n_acc] = y_acc[j % n_acc] + sn * e_ref[4, j:j + 1, :]
        return (sa_acc[0] + sa_acc[1]) + (sa_acc[2] + sa_acc[3]), (y_acc[0] + y_acc[1]) + (y_acc[2] + y_acc[3])

    sa, y = lax.fori_loop(0, WKV_TB, step, (acc_ref[0], acc_ref[1]))
    acc_ref[0] = sa
    acc_ref[1] = y

    @pl.when(pl.program_id(0) == pl.num_programs(0) - 1)
    def _():
        sout_ref[...] = s_ref[...]


def _wkv_seq(w, nb, k, na_next, rp_next, v):
    t = w.shape[0]
    blk = pl.BlockSpec((WKV_TB, WKV_SUB, LANES), lambda i: (i, 0, 0))
    state = pl.BlockSpec((RWKV_HEAD, WKV_SUB, LANES), lambda i: (0, 0, 0))
    return pl.pallas_call(
        _wkv_seq_kernel,
        out_shape=(jax.ShapeDtypeStruct((t, WKV_SUB, LANES), F32),
                   jax.ShapeDtypeStruct((RWKV_HEAD, WKV_SUB, LANES), F32)),
        grid=(t // WKV_TB,),
        in_specs=[blk] * 6,
        out_specs=(blk, state),
        scratch_shapes=[pltpu.VMEM((RWKV_HEAD, WKV_SUB, LANES), F32),
                        pltpu.VMEM((5, RWKV_HEAD, LANES), F32),
                        pltpu.VMEM((2, WKV_SUB, LANES), F32)],
        compiler_params=_cparams("arbitrary"),
        name="wkv_seq",
    )(w, nb, k, na_next, rp_next, v)


def _wkv_step_kernel(s_ref, w_ref, nb_ref, k_ref, na_ref, rp_ref, v_ref, y_ref, sout_ref):
    n_acc = 4
    shape = v_ref.shape
    sa_acc = [jnp.zeros(shape, F32) for _ in range(n_acc)]
    y_acc = [jnp.zeros(shape, F32) for _ in range(n_acc)]
    for j in range(RWKV_HEAD):
        sj = s_ref[j]
        sa_acc[j % n_acc] = sa_acc[j % n_acc] + sj * na_ref[j:j + 1, :]
        y_acc[j % n_acc] = y_acc[j % n_acc] + sj * rp_ref[j:j + 1, :]
    sa = (sa_acc[0] + sa_acc[1]) + (sa_acc[2] + sa_acc[3])
    y_ref[...] = (y_acc[0] + y_acc[1]) + (y_acc[2] + y_acc[3])
    v = v_ref[...]
    for j in range(RWKV_HEAD):
        sout_ref[j] = s_ref[j] * w_ref[j:j + 1, :] + sa * nb_ref[j:j + 1, :] + v * k_ref[j:j + 1, :]


def _wkv_step(state, w, nb, k, na, rp, v):
    n = state.shape[-1]
    st = pl.BlockSpec((RWKV_HEAD, RWKV_HEAD, LANES), lambda i: (0, 0, i))
    vec = pl.BlockSpec((RWKV_HEAD, LANES), lambda i: (0, i))
    return pl.pallas_call(
        _wkv_step_kernel,
        out_shape=(jax.ShapeDtypeStruct((RWKV_HEAD, n), F32), jax.ShapeDtypeStruct(state.shape, F32)),
        grid=(n // LANES,),
        in_specs=[st] + [vec] * 6,
        out_specs=(vec, st),
        compiler_params=_cparams("parallel"),
        name="wkv_step",
    )(state, w, nb, k, na, rp, v)


def _merge_kernel(x_ref, o0_ref, o1_ref, o2_ref, l0_ref, l1_ref, l2_ref, yw_ref, vkr_ref, bonus_ref, g_ref,
                  gate_ref, gng_ref, gnb_ref, bd_ref, wa_ref, wr_ref, wo_ref, out_ref):
    wide = lambda ref: jnp.concatenate([ref[0], ref[1]], axis=1)
    l0, l1, l2 = wide(l0_ref), wide(l1_ref), wide(l2_ref)
    lm = jnp.maximum(jnp.maximum(l0, l1), l2)
    e0, e1, e2 = jnp.exp(l0 - lm), jnp.exp(l1 - lm), jnp.exp(l2 - lm)
    o_att = (e0 * wide(o0_ref) + e1 * wide(o1_ref) + e2 * wide(o2_ref)) / (e0 + e1 + e2)
    y_att = _dot(o_att.astype(BF16), wa_ref[...])

    bd = bd_ref[...]
    y = yw_ref[...] + vkr_ref[...]
    mean = _seg_sum(y, bd) * (1.0 / RWKV_HEAD)
    yc = y - mean
    var = _seg_sum(yc * yc, bd) * (1.0 / RWKV_HEAD)
    yn = yc * lax.rsqrt(var + GN_EPS) * gng_ref[...] + gnb_ref[...]
    o_rwkv = (yn + bonus_ref[...]) * g_ref[...]
    y_rwkv = _dot(o_rwkv.astype(BF16), wr_ref[...])

    merged = _sigmoid(gate_ref[:, 0:D_MODEL]) * y_att + _sigmoid(gate_ref[:, D_MODEL:2 * D_MODEL]) * y_rwkv
    out_ref[...] = x_ref[...] + _dot(merged.astype(BF16), wo_ref[...])


def _merge(x2d, o_groups, lse_groups, yw, vkr, bonus, g, gates, gn_g, gn_b, bd512, wa, wr, wo, tm):
    m = x2d.shape[0]
    row = lambda w: pl.BlockSpec((tm, w), lambda i: (i, 0))
    return pl.pallas_call(
        _merge_kernel,
        out_shape=jax.ShapeDtypeStruct((m, D_MODEL), F32),
        grid=(m // tm,),
        in_specs=[row(D_MODEL)] + [pl.BlockSpec((GROUP_WIDTH // LANES, tm, LANES), lambda i: (0, i, 0))] * 6
                 + [row(RWKV_WIDTH)] * 4 + [row(2 * D_MODEL)]
                 + [_resident((1, RWKV_WIDTH)), _resident((1, RWKV_WIDTH)), _resident((RWKV_WIDTH, RWKV_WIDTH)),
                    _resident((GROUP_WIDTH, D_MODEL)), _resident((RWKV_WIDTH, D_MODEL)),
                    _resident((D_MODEL, D_MODEL))],
        out_specs=row(D_MODEL),
        compiler_params=_cparams("parallel"),
        name="merge",
    )(x2d, *o_groups, *lse_groups, yw, vkr, bonus, g, gates, gn_g, gn_b, bd512, wa, wr, wo)


def _ffn_chunk(xn, u_prev, wup_ref, cw_ref, cb_ref, c):
    c0, c1 = c, c + FF_CHUNK
    u = _dot(xn, wup_ref[:, c0:c1])
    u_m2, u_m1 = u_prev(u, c0, c1)
    conv = cb_ref[:, c0:c1] + u_m2 * cw_ref[0:1, c0:c1] + u_m1 * cw_ref[1:2, c0:c1] + u * cw_ref[2:3, c0:c1]
    return u, conv


def _ffn_body(x_ref, g2_ref, wup_ref, cw_ref, cb_ref, wdn_ref, out_ref, u_prev, u_sink):
    x = x_ref[...]
    ms = jnp.mean(x * x, axis=-1, keepdims=True)
    xn = (x * lax.rsqrt(ms + RMS_EPS) * g2_ref[...]).astype(BF16)
    acc = x
    for c in range(0, D_FF, FF_CHUNK):
        ug, gate = _ffn_chunk(xn, u_prev, wup_ref, cw_ref, cb_ref, c)
        uv, val = _ffn_chunk(xn, u_prev, wup_ref, cw_ref, cb_ref, D_FF + c)
        u_sink(ug, c)
        u_sink(uv, D_FF + c)
        h = gate * _sigmoid(gate) * val
        acc = acc + _dot(h.astype(BF16), wdn_ref[c:c + FF_CHUNK, :])
    out_ref[...] = acc


def _ffn_seq_kernel(x_ref, g2_ref, wup_ref, cw_ref, cb_ref, wdn_ref, out_ref, ust_ref, carry_ref):
    @pl.when(pl.program_id(1) == 0)
    def _():
        carry_ref[...] = jnp.zeros_like(carry_ref)

    tm = x_ref.shape[0]
    row = lax.broadcasted_iota(jnp.int32, (tm, FF_CHUNK), 0)

    def u_prev(u, c0, c1):
        c6 = carry_ref[6:7, c0:c1]
        c7 = carry_ref[7:8, c0:c1]
        u_m1 = jnp.where(row == 0, c7, pltpu.roll(u, 1, 0))
        u_m2 = jnp.where(row == 0, c6, jnp.where(row == 1, c7, pltpu.roll(u, 2, 0)))
        return u_m2, u_m1

    def u_sink(u, c0):
        carry_ref[:, c0:c0 + FF_CHUNK] = u[tm - 8:tm, :]

    _ffn_body(x_ref, g2_ref, wup_ref, cw_ref, cb_ref, wdn_ref, out_ref, u_prev, u_sink)
    ust_ref[...] = carry_ref[...]


def _ffn_step_kernel(x_ref, p0_ref, p1_ref, g2_ref, wup_ref, cw_ref, cb_ref, wdn_ref, out_ref, u_ref):
    def u_prev(u, c0, c1):
        return p0_ref[:, c0:c1], p1_ref[:, c0:c1]

    def u_sink(u, c0):
        u_ref[:, c0:c0 + FF_CHUNK] = u

    _ffn_body(x_ref, g2_ref, wup_ref, cw_ref, cb_ref, wdn_ref, out_ref, u_prev, u_sink)


def _ffn(x2d, conv_prev, ln2_g, wup, cw, cb, wdn, batch, seq, tm):
    m = x2d.shape[0]
    w_specs = [_resident((1, D_MODEL)), _resident((D_MODEL, 2 * D_FF)), _resident((3, 2 * D_FF)),
               _resident((1, 2 * D_FF)), _resident((D_FF, D_MODEL))]
    if conv_prev is None:
        nt = seq // tm
        row = pl.BlockSpec((tm, D_MODEL), lambda b, t: (b * nt + t, 0))
        return pl.pallas_call(
            _ffn_seq_kernel,
            out_shape=(jax.ShapeDtypeStruct((m, D_MODEL), F32), jax.ShapeDtypeStruct((batch, 8, 2 * D_FF), F32)),
            grid=(batch, nt),
            in_specs=[row] + w_specs,
            out_specs=(row, pl.BlockSpec((None, 8, 2 * D_FF), lambda b, t: (b, 0, 0))),
            scratch_shapes=[pltpu.VMEM((8, 2 * D_FF), F32)],
            compiler_params=_cparams("parallel", "arbitrary"), name="ffn_seq",
        )(x2d, ln2_g, wup, cw, cb, wdn)
    row = lambda w: pl.BlockSpec((tm, w), lambda i: (i, 0))
    return pl.pallas_call(
        _ffn_step_kernel,
        out_shape=(jax.ShapeDtypeStruct((m, D_MODEL), F32), jax.ShapeDtypeStruct((m, 2 * D_FF), F32)),
        grid=(m // tm,),
        in_specs=[row(D_MODEL), row(2 * D_FF), row(2 * D_FF)] + w_specs,
        out_specs=(row(D_MODEL), row(2 * D_FF)),
        compiler_params=_cparams("parallel"), name="ffn_step",
    )(x2d, conv_prev[0], conv_prev[1], ln2_g, wup, cw, cb, wdn)


def _to_state_lanes(x, batch, seq):
    x = x.reshape(batch, seq, RWKV_HEADS, WKV_Q, WKV_SUB)
    return x.transpose(1, 4, 3, 0, 2).reshape(seq, WKV_SUB, LANES)


def _from_state_lanes(y, batch, seq):
    y = y.reshape(seq, WKV_SUB, WKV_Q, batch, RWKV_HEADS)
    return y.transpose(3, 0, 4, 2, 1).reshape(batch * seq, RWKV_WIDTH)


def _shift_next(x):
    return jnp.concatenate([x[1:], jnp.zeros_like(x[:1])], axis=0)


def _pad_rows(w, before, total):
    return jnp.zeros((total, w.shape[1]), w.dtype).at[before:before + w.shape[0]].set(w)


def kernel(x_prompt, x_sample, cache_k_w128, cache_v_w128, cache_k_w512, cache_v_w512, cache_k_w2048, cache_v_w2048,
           state_rwkv_shift, state_rwkv_wkv, state_ffn_conv, ln1_g, w_in, q_norm_g, k_norm_g, w_attn_out,
           rwkv_mu, rwkv_w0, rwkv_w2, rwkv_a0, rwkv_a2, rwkv_g2, rwkv_k_k, rwkv_k_a, rwkv_r_k, rwkv_gn_g, rwkv_gn_b,
           w_rwkv_out, w_o, ln2_g, w_up, conv_w, conv_b, w_down):
    batch, seq, _ = x_prompt.shape
    nsamp = x_sample.shape[0]
    assert seq % SEG == 0 and x_sample.shape[1] == 1 and batch * RWKV_HEADS * WKV_Q == LANES
    assert ln1_g.shape[0] == 1, "single layer"
    lyr = 0

    w_in_bf = w_in[lyr].astype(BF16)
    bd768 = _block_diag_ones(ATT_WIDTH)
    bd512 = _block_diag_ones(RWKV_WIDTH)
    gq = jnp.tile(q_norm_g[lyr], ATT_WIDTH // HEAD_DIM)[None]
    gk = jnp.tile(k_norm_g[lyr], ATT_WIDTH // HEAD_DIM)[None]
    row1 = lambda a: a.reshape(1, -1)
    prep_w = (row1(rwkv_mu[lyr]), row1(rwkv_w0[lyr]), row1(rwkv_a0[lyr]),
              *_split_bf16(_pad_rows(rwkv_w2[lyr], 0, DECAY_LORA + ICLR_LORA)),
              *_split_bf16(_pad_rows(rwkv_a2[lyr], DECAY_LORA, DECAY_LORA + ICLR_LORA)),
              *_split_bf16(rwkv_g2[lyr]),
              row1(rwkv_k_k[lyr]), row1(rwkv_k_a[lyr]), row1(rwkv_r_k[lyr]), bd512)
    merge_w = (row1(rwkv_gn_g[lyr]), row1(rwkv_gn_b[lyr]), bd512, w_attn_out[lyr].astype(BF16),
               w_rwkv_out[lyr].astype(BF16), w_o[lyr].astype(BF16))
    ffn_w = (row1(ln2_g[lyr]), w_up[lyr].astype(BF16), conv_w[lyr], row1(conv_b[lyr]), w_down[lyr].astype(BF16))

    xp = x_prompt.reshape(batch * seq, D_MODEL)
    qn, kn, v, pr, gates = _proj(xp, row1(ln1_g[lyr]), w_in_bf, bd768, gq, gk, tm=256)
    att = [_attn_prompt(qn, kn, v, g, batch, seq) for g in range(N_GROUPS)]
    na, dec, nb, kp, rp, rv, vkr, bonus, gg = _prep(pr, None, prep_w, batch, seq, tm=256)
    lanes = lambda a: _to_state_lanes(a, batch, seq)
    yw, s_fin = _wkv_seq(lanes(dec), lanes(nb), lanes(kp), _shift_next(lanes(na)), _shift_next(lanes(rp)), lanes(rv))
    yw = _from_state_lanes(yw, batch, seq)
    x2 = _merge(xp, [a[0] for a in att], [a[1] for a in att], yw, vkr, bonus, gg, gates, *merge_w, tm=256)
    yp, ust = _ffn(x2, None, *ffn_w, batch, seq, tm=256)

    y_prompt = yp.reshape(batch, seq, D_MODEL)
    k5 = kn.reshape(batch, seq, N_GROUPS, HEADS_PER_GROUP, HEAD_DIM)
    v5 = v.reshape(batch, seq, N_GROUPS, HEADS_PER_GROUP, HEAD_DIM)
    kv_prompt = []
    for g, (win, _) in enumerate(ATT_GROUPS):
        keep = min(win, seq)
        kv_prompt += [k5[None, :, seq - keep:, g], v5[None, :, seq - keep:, g]]
    shift_prompt = pr.reshape(batch, seq, RWKV_PROJ)[None, :, -1]
    wkv_prompt = (s_fin.reshape(RWKV_HEAD, WKV_SUB, WKV_Q, batch, RWKV_HEADS)
                  .transpose(3, 4, 2, 1, 0).reshape(1, batch, RWKV_HEADS, RWKV_HEAD, RWKV_HEAD))
    conv_prompt = ust[None, :, 6:8]

    xs = x_sample.reshape(nsamp, D_MODEL)
    qs, ks, vs, prs, gates_s = _proj(xs, row1(ln1_g[lyr]), w_in_bf, bd768, gq, gk, tm=nsamp)
    caches = [c[lyr].reshape(nsamp, c.shape[2], GROUP_WIDTH)
              for c in (cache_k_w128, cache_v_w128, cache_k_w512, cache_v_w512, cache_k_w2048, cache_v_w2048)]
    lane_head = jnp.arange(GROUP_WIDTH) // HEAD_DIM
    hsel = (lane_head[:, None] == jnp.arange(LANES)[None, :]).astype(BF16)
    kv_sample, o_s, lse_s = [], [], []
    for g, (win, dil) in enumerate(ATT_GROUPS):
        assert caches[2 * g].shape[1] == BLOCK * dil, "window caches must be full"
        k_out, v_out, o_g, lse_g = _cache_attn(qs, ks, vs, caches[2 * g], caches[2 * g + 1], hsel, hsel.T, g,
                                               bb=8 if win <= 512 else 2)
        kv_sample += [c.reshape(1, nsamp, win, HEADS_PER_GROUP, HEAD_DIM) for c in (k_out, v_out)]
        o_s.append(o_g)
        lse_s.append(lse_g)
    na, dec, nb, kp, rp, rv, vkr, bonus, gg = _prep(prs, state_rwkv_shift[lyr], prep_w, nsamp, 1, tm=nsamp)
    cols = lambda a: a.reshape(nsamp * RWKV_HEADS, RWKV_HEAD).T
    state_t = state_rwkv_wkv[lyr].transpose(3, 2, 0, 1).reshape(RWKV_HEAD, RWKV_HEAD, nsamp * RWKV_HEADS)
    yw_s, state_new = _wkv_step(state_t, cols(dec), cols(nb), cols(kp), cols(na), cols(rp), cols(rv))
    yw_s = yw_s.T.reshape(nsamp, RWKV_WIDTH)
    x2s = _merge(xs, o_s, lse_s, yw_s, vkr, bonus, gg, gates_s, *merge_w, tm=nsamp)
    conv_prev = state_ffn_conv[lyr]
    ys, u_s = _ffn(x2s, (conv_prev[:, 0], conv_prev[:, 1]), *ffn_w, nsamp, 1, tm=nsamp)

    y_sample = ys.reshape(nsamp, 1, D_MODEL)
    shift_sample = prs[None]
    wkv_sample = (state_new.reshape(RWKV_HEAD, RWKV_HEAD, nsamp, RWKV_HEADS)
                  .transpose(2, 3, 1, 0)[None])
    conv_sample = jnp.stack([conv_prev[:, 1], u_s], axis=1)[None]

    return (y_prompt, y_sample,
            kv_prompt[0], kv_sample[0], kv_prompt[1], kv_sample[1],
            kv_prompt[2], kv_sample[2], kv_prompt[3], kv_sample[3],
            kv_prompt[4], kv_sample[4], kv_prompt[5], kv_sample[5],
            shift_prompt, shift_sample, wkv_prompt, wkv_sample, conv_prompt, conv_sample)
```

```python
import functools

import jax
import jax.numpy as jnp
from jax import lax
from jax.experimental import pallas as pl
from jax.experimental.pallas import tpu as pltpu

F32 = jnp.float32
BF16 = jnp.bfloat16

D_MODEL = 1024
HEAD_DIM = 64
ATT_GROUPS = ((128, 1), (512, 4), (2048, 16))
N_GROUPS = len(ATT_GROUPS)
HEADS_PER_GROUP = 4
GROUP_WIDTH = HEADS_PER_GROUP * HEAD_DIM
ATT_WIDTH = N_GROUPS * GROUP_WIDTH
BLOCK = 128
SEG = 2048
RWKV_HEADS = 8
RWKV_HEAD = 64
RWKV_WIDTH = RWKV_HEADS * RWKV_HEAD
DECAY_LORA = 64
ICLR_LORA = 64
GATE_LORA = 128
RWKV_PROJ = 3 * RWKV_WIDTH + DECAY_LORA + ICLR_LORA + GATE_LORA
LORA_OFF = 3 * RWKV_WIDTH
GATE_OFF = LORA_OFF + DECAY_LORA + ICLR_LORA
PR_OFF = 3 * ATT_WIDTH
GATES_OFF = PR_OFF + RWKV_PROJ
P_TOTAL = GATES_OFF + 2 * D_MODEL
D_FF = 2816
FF_CHUNK = 256
RMS_EPS = 1e-6
GN_EPS = 64e-5
L2_EPS = 1e-12
NEG_BIG = -1e30

LANES = 128
WKV_Q = 4
WKV_SUB = RWKV_HEAD // WKV_Q
WKV_TB = 64
VMEM_LIMIT = 56 * 1024 * 1024


def _cparams(*sem):
    return pltpu.CompilerParams(dimension_semantics=sem, vmem_limit_bytes=VMEM_LIMIT)


def _resident(shape):
    n = len(shape)
    return pl.BlockSpec(shape, lambda *_: (0,) * n, pipeline_mode=pl.Buffered(1))


def _split_bf16(x):
    hi = x.astype(BF16)
    lo = (x - hi.astype(F32)).astype(BF16)
    return hi, lo


def _dot(a, b):
    return jnp.dot(a, b, preferred_element_type=F32)


def _dot3(a, b_hi, b_lo):
    a_hi, a_lo = _split_bf16(a)
    return _dot(a_hi, b_hi) + _dot(a_lo, b_hi) + _dot(a_hi, b_lo)


def _seg_sum(x, ones_bd):
    hi, lo = _split_bf16(x)
    return _dot(hi, ones_bd) + _dot(lo, ones_bd)


def _sigmoid(x):
    return 1.0 / (1.0 + jnp.exp(-x))


def _block_diag_ones(width):
    idx = jnp.arange(width) // HEAD_DIM
    return (idx[:, None] == idx[None, :]).astype(BF16)


def _proj_kernel(x_ref, g1_ref, w_ref, bd_ref, gq_ref, gk_ref, q_ref, k_ref, v_ref, pr_ref, gate_ref):
    x = x_ref[...]
    ms = jnp.mean(x * x, axis=-1, keepdims=True)
    xn = (x * lax.rsqrt(ms + RMS_EPS) * g1_ref[...]).astype(BF16)

    def proj(c0, c1):
        return _dot(xn, w_ref[:, c0:c1])

    def head_norm(z, g):
        ss = _dot((z * z).astype(BF16), bd_ref[...])
        return z * lax.rsqrt(ss * (1.0 / HEAD_DIM) + RMS_EPS) * g

    q_ref[...] = head_norm(proj(0, ATT_WIDTH), gq_ref[...])
    k_ref[...] = head_norm(proj(ATT_WIDTH, 2 * ATT_WIDTH), gk_ref[...])
    v_ref[...] = proj(2 * ATT_WIDTH, PR_OFF)
    pr_ref[...] = proj(PR_OFF, GATES_OFF)
    gate_ref[...] = proj(GATES_OFF, P_TOTAL)


def _proj(x2d, ln1_g, w_in_bf, bd768, gq, gk, tm):
    m = x2d.shape[0]
    row = lambda w: pl.BlockSpec((tm, w), lambda i: (i, 0))
    widths = (ATT_WIDTH, ATT_WIDTH, ATT_WIDTH, RWKV_PROJ, 2 * D_MODEL)
    return pl.pallas_call(
        _proj_kernel,
        out_shape=tuple(jax.ShapeDtypeStruct((m, w), F32) for w in widths),
        grid=(m // tm,),
        in_specs=[row(D_MODEL), _resident((1, D_MODEL)), _resident((D_MODEL, P_TOTAL)),
                  _resident((ATT_WIDTH, ATT_WIDTH)), _resident((1, ATT_WIDTH)), _resident((1, ATT_WIDTH))],
        out_specs=tuple(row(w) for w in widths),
        compiler_params=_cparams("parallel"),
        name="proj",
    )(x2d, ln1_g, w_in_bf, bd768, gq, gk)


def _attn_prompt_kernel(*refs, dil):
    q_ref, kp_ref, kc_ref, vp_ref, vc_ref = (refs[2 * i:2 * i + 2] for i in range(5))
    o_ref, lse_ref = refs[10], refs[11]
    not_first_seg = pl.program_id(1) > 0
    nblk = SEG // (BLOCK * dil)
    stacked = (HEADS_PER_GROUP * BLOCK, 2 * BLOCK)
    row = lax.broadcasted_iota(jnp.int32, stacked, 0) % BLOCK
    col = lax.broadcasted_iota(jnp.int32, stacked, 1)
    in_prev = col < BLOCK
    band_prev = in_prev & (col >= row)
    band_cur = jnp.logical_not(in_prev) & (col - BLOCK <= row)
    lane = lax.broadcasted_iota(jnp.int32, (1, GROUP_WIDTH), 1)
    head_masks = [(lane // HEAD_DIM) == h for h in range(HEADS_PER_GROUP)]

    def row_slice(start):
        return pl.ds(start, BLOCK) if dil == 1 else pl.ds(start, BLOCK, stride=dil)

    def rows(halves, start):
        return jnp.concatenate([h[row_slice(start), :] for h in halves], axis=1)

    def block(q_start, k1_ref, k1_start, v1_ref, prev_valid):
        q = rows(q_ref, q_start) * (HEAD_DIM ** -0.5)
        q4 = jnp.concatenate([jnp.where(hm, q, 0.0) for hm in head_masks], axis=0).astype(BF16)
        kt = jnp.concatenate([rows(k1_ref, k1_start), rows(kc_ref, q_start)], axis=0).astype(BF16)
        vt = jnp.concatenate([rows(v1_ref, k1_start), rows(vc_ref, q_start)], axis=0).astype(BF16)
        s = lax.dot_general(q4, kt, (((1,), (1,)), ((), ())), preferred_element_type=F32)
        s = jnp.where(band_cur | (band_prev & prev_valid), s, NEG_BIG)
        m = jnp.max(s, axis=-1, keepdims=True)
        p = jnp.exp(s - m)
        l = jnp.sum(p, axis=-1, keepdims=True)
        ov = _dot(p.astype(BF16), vt) / l
        lse = m + jnp.log(l)
        o = jnp.zeros((BLOCK, GROUP_WIDTH), F32)
        lse_b = jnp.zeros((BLOCK, GROUP_WIDTH), F32)
        for h, hm in enumerate(head_masks):
            sl = slice(h * BLOCK, (h + 1) * BLOCK)
            o = jnp.where(hm, ov[sl], o)
            lse_b = jnp.where(hm, lse[sl], lse_b)
        for half in range(2):
            lanes = slice(half * LANES, (half + 1) * LANES)
            o_ref[half, row_slice(q_start), :] = o[:, lanes]
            lse_ref[half, row_slice(q_start), :] = lse_b[:, lanes]

    def first_blocks(r, carry):
        block(r, kp_ref, SEG - BLOCK * dil + r, vp_ref, not_first_seg)
        return carry

    def later_blocks(i, carry):
        r = i % dil
        nb = 1 + i // dil
        q_start = nb * BLOCK * dil + r
        block(q_start, kc_ref, q_start - BLOCK * dil, vc_ref, True)
        return carry

    lax.fori_loop(0, dil, first_blocks, 0)
    if nblk > 1:
        lax.fori_loop(0, dil * (nblk - 1), later_blocks, 0)


def _attn_prompt(qn, kn, v, group, batch, seq):
    dil = ATT_GROUPS[group][1]
    nseg = seq // SEG
    halves = GROUP_WIDTH // LANES
    cur = [pl.BlockSpec((SEG, LANES), lambda b, s, c=halves * group + h: (b * nseg + s, c)) for h in range(halves)]
    prev = [pl.BlockSpec((SEG, LANES), lambda b, s, c=halves * group + h: (b * nseg + jnp.maximum(s - 1, 0), c))
            for h in range(halves)]
    out = pl.BlockSpec((halves, SEG, LANES), lambda b, s: (0, b * nseg + s, 0))
    shape = jax.ShapeDtypeStruct((halves, batch * seq, LANES), F32)
    return pl.pallas_call(
        functools.partial(_attn_prompt_kernel, dil=dil),
        out_shape=(shape, shape),
        grid=(batch, nseg),
        in_specs=cur + prev + cur + prev + cur,
        out_specs=(out, out),
        compiler_params=_cparams("parallel", "arbitrary"),
        name=f"attn_prompt_g{group}",
    )(qn, qn, kn, kn, kn, kn, v, v, v, v)


def _cache_attn_kernel(q_ref, kn_ref, vn_ref, knt_ref, vnt_ref, kc_ref, vc_ref, ko_ref, vo_ref, o_ref, lse_ref,
                       *, group, dil, bb):
    win = kc_ref.shape[2]
    nseq = q_ref.shape[0]
    cols = slice(group * GROUP_WIDTH, (group + 1) * GROUP_WIDTH)
    pos = lax.broadcasted_iota(jnp.int32, (1, win), 1)
    in_window = (pos % dil) == 0
    newest = lax.broadcasted_iota(jnp.int32, (GROUP_WIDTH, win), 1) == win - 1
    head_rows = (lax.broadcasted_iota(jnp.int32, (8, GROUP_WIDTH), 0)
                 == lax.broadcasted_iota(jnp.int32, (8, GROUP_WIDTH), 1) // HEAD_DIM)
    seq_lane = lax.broadcasted_iota(jnp.int32, (GROUP_WIDTH, nseq), 1)

    def hi_lo_rows(x):
        hi, lo = _split_bf16(x)
        return jnp.concatenate([hi, lo], axis=0)

    def fold(x):
        return x[0:8] + x[8:16]

    def shifted(cache, new_t_ref, n):
        new_col = jnp.sum(jnp.where(seq_lane == n, new_t_ref[cols, :], 0.0), axis=1, keepdims=True)
        return jnp.where(newest, new_col, pltpu.roll(cache, win - 1, 1))

    for b in range(bb):
        n = pl.program_id(0) * bb + b
        q = q_ref[pl.ds(n, 1), cols] * (HEAD_DIM ** -0.5)
        k_new = kn_ref[pl.ds(n, 1), cols]
        v_new = vn_ref[pl.ds(n, 1), cols]
        q8 = jnp.where(head_rows, q, 0.0)
        kc = kc_ref[b]
        vc = vc_ref[b]
        s = fold(_dot(hi_lo_rows(q8), kc.astype(BF16)))
        s = jnp.where(in_window, s, NEG_BIG)
        s_new = jnp.sum(q8 * k_new, axis=1, keepdims=True)
        m = jnp.maximum(jnp.max(s, axis=1, keepdims=True), s_new)
        p = jnp.exp(s - m)
        p_new = jnp.exp(s_new - m)
        l = jnp.sum(p, axis=1, keepdims=True) + p_new
        pv = fold(lax.dot_general(hi_lo_rows(p), vc.astype(BF16), (((1,), (1,)), ((), ())),
                                  preferred_element_type=F32))
        ov = (pv + p_new * v_new) / l
        o = jnp.sum(jnp.where(head_rows, ov, 0.0), axis=0, keepdims=True)
        lse = jnp.sum(jnp.where(head_rows, m + jnp.log(l), 0.0), axis=0, keepdims=True)
        for half in range(GROUP_WIDTH // LANES):
            lanes = slice(half * LANES, (half + 1) * LANES)
            o_ref[half, pl.ds(n, 1), :] = o[:, lanes]
            lse_ref[half, pl.ds(n, 1), :] = lse[:, lanes]
        ko_ref[b] = shifted(kc, knt_ref, n)
        vo_ref[b] = shifted(vc, vnt_ref, n)


def _cache_attn(qn, kn, v, kn_t, v_t, k_cache, v_cache, group, bb):
    n, _, win = k_cache.shape
    dil = ATT_GROUPS[group][1]
    halves = GROUP_WIDTH // LANES
    cache = pl.BlockSpec((bb, GROUP_WIDTH, win), lambda i: (i, 0, 0))
    rows = _resident((n, ATT_WIDTH))
    rows_t = _resident((ATT_WIDTH, n))
    out = pl.BlockSpec((halves, n, LANES), lambda i: (0, 0, 0))
    cache_shape = jax.ShapeDtypeStruct(k_cache.shape, F32)
    o_shape = jax.ShapeDtypeStruct((halves, n, LANES), F32)
    return pl.pallas_call(
        functools.partial(_cache_attn_kernel, group=group, dil=dil, bb=bb),
        out_shape=(cache_shape, cache_shape, o_shape, o_shape),
        grid=(n // bb,),
        in_specs=[rows, rows, rows, rows_t, rows_t, cache, cache],
        out_specs=(cache, cache, out, out),
        compiler_params=_cparams("arbitrary"),
        name=f"cache_attn_g{group}",
    )(qn, kn, v, kn_t, v_t, k_cache, v_cache)


def _prep_math(p, prev, mu_ref, w0_ref, a0_ref, w2h_ref, w2l_ref, a2h_ref, a2l_ref, g2h_ref, g2l_ref,
               kk_ref, ka_ref, rk_ref, bd_ref, outs):
    na_ref, dec_ref, nb_ref, kp_ref, rp_ref, v_ref, vkr_ref, bonus_ref, g_ref = outs
    bd = bd_ref[...]
    xs = p + mu_ref[...] * (prev - p)
    r = xs[:, 0:RWKV_WIDTH]
    k = xs[:, RWKV_WIDTH:2 * RWKV_WIDTH]
    v = xs[:, 2 * RWKV_WIDTH:3 * RWKV_WIDTH]
    lora = xs[:, LORA_OFF:GATE_OFF]
    gl = xs[:, GATE_OFF:RWKV_PROJ]
    z = w0_ref[...] + _dot3(jnp.tanh(lora), w2h_ref[...], w2l_ref[...])
    softplus_neg = jnp.maximum(-z, 0.0) + jnp.log(1.0 + jnp.exp(-jnp.abs(z)))
    w = -softplus_neg - 0.5
    dec = jnp.exp(-jnp.exp(w))
    a = _sigmoid(a0_ref[...] + _dot3(lora, a2h_ref[...], a2l_ref[...]))
    g = _dot3(_sigmoid(gl), g2h_ref[...], g2l_ref[...])
    kk = k * kk_ref[...]
    kk = kk / jnp.maximum(jnp.sqrt(_seg_sum(kk * kk, bd)), L2_EPS)
    kp = k * (1.0 + (a - 1.0) * ka_ref[...])
    na = -kk
    nb = kk * a
    rp = dec * r + na * _seg_sum(nb * r, bd)
    na_ref[...] = na
    dec_ref[...] = dec
    nb_ref[...] = nb
    kp_ref[...] = kp
    rp_ref[...] = rp
    v_ref[...] = v
    vkr_ref[...] = v * _seg_sum(kp * r, bd)
    bonus_ref[...] = _seg_sum(r * kp * rk_ref[...], bd) * v
    g_ref[...] = g


def _prep_seq_kernel(p_ref, *rest):
    weights, outs, carry_ref = rest[:13], rest[13:22], rest[22]

    @pl.when(pl.program_id(1) == 0)
    def _():
        carry_ref[...] = jnp.zeros_like(carry_ref)

    p = p_ref[...]
    tm = p.shape[0]
    row = lax.broadcasted_iota(jnp.int32, p.shape, 0)
    prev = jnp.where(row == 0, carry_ref[7:8, :], pltpu.roll(p, 1, 0))
    carry_ref[7:8, :] = p[tm - 1:tm, :]
    _prep_math(p, prev, *weights, outs)


def _prep_step_kernel(p_ref, prev_ref, *rest):
    weights, outs = rest[:13], rest[13:22]
    _prep_math(p_ref[...], prev_ref[...], *weights, outs)


def _prep(pr, shift_prev, weights, batch, seq, tm):
    m = pr.shape[0]
    w_specs = [_resident(w.shape) for w in weights]
    out_shape = tuple(jax.ShapeDtypeStruct((m, RWKV_WIDTH), F32) for _ in range(9))
    if shift_prev is None:
        nt = seq // tm
        row = lambda w: pl.BlockSpec((tm, w), lambda b, t: (b * nt + t, 0))
        return pl.pallas_call(
            _prep_seq_kernel, out_shape=out_shape, grid=(batch, nt),
            in_specs=[row(RWKV_PROJ)] + w_specs, out_specs=tuple(row(RWKV_WIDTH) for _ in range(9)),
            scratch_shapes=[pltpu.VMEM((8, RWKV_PROJ), F32)],
            compiler_params=_cparams("parallel", "arbitrary"), name="rwkv_prep_seq",
        )(pr, *weights)
    row = lambda w: pl.BlockSpec((tm, w), lambda i: (i, 0))
    return pl.pallas_call(
        _prep_step_kernel, out_shape=out_shape, grid=(m // tm,),
        in_specs=[row(RWKV_PROJ), row(RWKV_PROJ)] + w_specs, out_specs=tuple(row(RWKV_WIDTH) for _ in range(9)),
        compiler_params=_cparams("parallel"), name="rwkv_prep_step",
    )(pr, shift_prev, *weights)


def _wkv_seq_kernel(w_ref, nb_ref, k_ref, na_ref, rp_ref, v_ref, y_ref, sout_ref, s_ref, e_ref, acc_ref):
    @pl.when(pl.program_id(0) == 0)
    def _():
        s_ref[...] = jnp.zeros_like(s_ref)
        acc_ref[...] = jnp.zeros_like(acc_ref)

    quarter = lax.broadcasted_iota(jnp.int32, (WKV_SUB, LANES), 1) // (LANES // WKV_Q)
    qmask = [quarter == q for q in range(WKV_Q)]

    def expand(i, xc):
        rolled = [xc] + [pltpu.roll(xc, (LANES // WKV_Q) * s, 1) for s in range(1, WKV_Q)]
        for q in range(WKV_Q):
            out = rolled[(0 - q) % WKV_Q]
            for pq in range(1, WKV_Q):
                out = jnp.where(qmask[pq], rolled[(pq - q) % WKV_Q], out)
            e_ref[i, q * WKV_SUB:(q + 1) * WKV_SUB, :] = out

    def step(t, carry):
        sa, y = carry
        y_ref[t] = y
        for i, ref in enumerate((w_ref, nb_ref, k_ref, na_ref, rp_ref)):
            expand(i, ref[t])
        v = v_ref[t]
        n_acc = 4
        sa_acc = [jnp.zeros((WKV_SUB, LANES), F32) for _ in range(n_acc)]
        y_acc = [jnp.zeros((WKV_SUB, LANES), F32) for _ in range(n_acc)]
        for j in range(RWKV_HEAD):
            sn = s_ref[j] * e_ref[0, j:j + 1, :] + sa * e_ref[1, j:j + 1, :] + v * e_ref[2, j:j + 1, :]
            s_ref[j] = sn
            sa_acc[j % n_acc] = sa_acc[j % n_acc] + sn * e_ref[3, j:j + 1, :]
            y_acc[j % n_acc] = y_acc[j % n_acc] + sn * e_ref[4, j:j + 1, :]
        return (sa_acc[0] + sa_acc[1]) + (sa_acc[2] + sa_acc[3]), (y_acc[0] + y_acc[1]) + (y_acc[2] + y_acc[3])

    sa, y = lax.fori_loop(0, WKV_TB, step, (acc_ref[0], acc_ref[1]))
    acc_ref[0] = sa
    acc_ref[1] = y

    @pl.when(pl.program_id(0) == pl.num_programs(0) - 1)
    def _():
        sout_ref[...] = s_ref[...]


def _wkv_seq(w, nb, k, na_next, rp_next, v):
    t = w.shape[0]
    blk = pl.BlockSpec((WKV_TB, WKV_SUB, LANES), lambda i: (i, 0, 0))
    state = pl.BlockSpec((RWKV_HEAD, WKV_SUB, LANES), lambda i: (0, 0, 0))
    return pl.pallas_call(
        _wkv_seq_kernel,
        out_shape=(jax.ShapeDtypeStruct((t, WKV_SUB, LANES), F32),
                   jax.ShapeDtypeStruct((RWKV_HEAD, WKV_SUB, LANES), F32)),
        grid=(t // WKV_TB,),
        in_specs=[blk] * 6,
        out_specs=(blk, state),
        scratch_shapes=[pltpu.VMEM((RWKV_HEAD, WKV_SUB, LANES), F32),
                        pltpu.VMEM((5, RWKV_HEAD, LANES), F32),
                        pltpu.VMEM((2, WKV_SUB, LANES), F32)],
        compiler_params=_cparams("arbitrary"),
        name="wkv_seq",
    )(w, nb, k, na_next, rp_next, v)


def _wkv_step_kernel(s_ref, w_ref, nb_ref, k_ref, na_ref, rp_ref, v_ref, y_ref, sout_ref):
    n_acc = 4
    shape = v_ref.shape
    sa_acc = [jnp.zeros(shape, F32) for _ in range(n_acc)]
    y_acc = [jnp.zeros(shape, F32) for _ in range(n_acc)]
    for j in range(RWKV_HEAD):
        sj = s_ref[j]
        sa_acc[j % n_acc] = sa_acc[j % n_acc] + sj * na_ref[j:j + 1, :]
        y_acc[j % n_acc] = y_acc[j % n_acc] + sj * rp_ref[j:j + 1, :]
    sa = (sa_acc[0] + sa_acc[1]) + (sa_acc[2] + sa_acc[3])
    y_ref[...] = (y_acc[0] + y_acc[1]) + (y_acc[2] + y_acc[3])
    v = v_ref[...]
    for j in range(RWKV_HEAD):
        sout_ref[j] = s_ref[j] * w_ref[j:j + 1, :] + sa * nb_ref[j:j + 1, :] + v * k_ref[j:j + 1, :]


def _wkv_step(state, w, nb, k, na, rp, v):
    n = state.shape[-1]
    st = pl.BlockSpec((RWKV_HEAD, RWKV_HEAD, LANES), lambda i: (0, 0, i))
    vec = pl.BlockSpec((RWKV_HEAD, LANES), lambda i: (0, i))
    return pl.pallas_call(
        _wkv_step_kernel,
        out_shape=(jax.ShapeDtypeStruct((RWKV_HEAD, n), F32), jax.ShapeDtypeStruct(state.shape, F32)),
        grid=(n // LANES,),
        in_specs=[st] + [vec] * 6,
        out_specs=(vec, st),
        compiler_params=_cparams("parallel"),
        name="wkv_step",
    )(state, w, nb, k, na, rp, v)


def _merge_kernel(x_ref, o0_ref, o1_ref, o2_ref, l0_ref, l1_ref, l2_ref, yw_ref, vkr_ref, bonus_ref, g_ref,
                  gate_ref, gng_ref, gnb_ref, bd_ref, wa_ref, wr_ref, wo_ref, out_ref):
    wide = lambda ref: jnp.concatenate([ref[0], ref[1]], axis=1)
    l0, l1, l2 = wide(l0_ref), wide(l1_ref), wide(l2_ref)
    lm = jnp.maximum(jnp.maximum(l0, l1), l2)
    e0, e1, e2 = jnp.exp(l0 - lm), jnp.exp(l1 - lm), jnp.exp(l2 - lm)
    o_att = (e0 * wide(o0_ref) + e1 * wide(o1_ref) + e2 * wide(o2_ref)) / (e0 + e1 + e2)
    y_att = _dot(o_att.astype(BF16), wa_ref[...])

    bd = bd_ref[...]
    y = yw_ref[...] + vkr_ref[...]
    mean = _seg_sum(y, bd) * (1.0 / RWKV_HEAD)
    yc = y - mean
    var = _seg_sum(yc * yc, bd) * (1.0 / RWKV_HEAD)
    yn = yc * lax.rsqrt(var + GN_EPS) * gng_ref[...] + gnb_ref[...]
    o_rwkv = (yn + bonus_ref[...]) * g_ref[...]
    y_rwkv = _dot(o_rwkv.astype(BF16), wr_ref[...])

    merged = _sigmoid(gate_ref[:, 0:D_MODEL]) * y_att + _sigmoid(gate_ref[:, D_MODEL:2 * D_MODEL]) * y_rwkv
    out_ref[...] = x_ref[...] + _dot(merged.astype(BF16), wo_ref[...])


def _merge(x2d, o_groups, lse_groups, yw, vkr, bonus, g, gates, gn_g, gn_b, bd512, wa, wr, wo, tm):
    m = x2d.shape[0]
    row = lambda w: pl.BlockSpec((tm, w), lambda i: (i, 0))
    return pl.pallas_call(
        _merge_kernel,
        out_shape=jax.ShapeDtypeStruct((m, D_MODEL), F32),
        grid=(m // tm,),
        in_specs=[row(D_MODEL)] + [pl.BlockSpec((GROUP_WIDTH // LANES, tm, LANES), lambda i: (0, i, 0))] * 6
                 + [row(RWKV_WIDTH)] * 4 + [row(2 * D_MODEL)]
                 + [_resident((1, RWKV_WIDTH)), _resident((1, RWKV_WIDTH)), _resident((RWKV_WIDTH, RWKV_WIDTH)),
                    _resident((GROUP_WIDTH, D_MODEL)), _resident((RWKV_WIDTH, D_MODEL)),
                    _resident((D_MODEL, D_MODEL))],
        out_specs=row(D_MODEL),
        compiler_params=_cparams("parallel"),
        name="merge",
    )(x2d, *o_groups, *lse_groups, yw, vkr, bonus, g, gates, gn_g, gn_b, bd512, wa, wr, wo)


def _ffn_chunk(xn, u_prev, wup_ref, cw_ref, cb_ref, c):
    c0, c1 = c, c + FF_CHUNK
    u = _dot(xn, wup_ref[:, c0:c1])
    u_m2, u_m1 = u_prev(u, c0, c1)
    conv = cb_ref[:, c0:c1] + u_m2 * cw_ref[0:1, c0:c1] + u_m1 * cw_ref[1:2, c0:c1] + u * cw_ref[2:3, c0:c1]
    return u, conv


def _ffn_body(x_ref, g2_ref, wup_ref, cw_ref, cb_ref, wdn_ref, out_ref, u_prev, u_sink):
    x = x_ref[...]
    ms = jnp.mean(x * x, axis=-1, keepdims=True)
    xn = (x * lax.rsqrt(ms + RMS_EPS) * g2_ref[...]).astype(BF16)
    acc = x
    for c in range(0, D_FF, FF_CHUNK):
        ug, gate = _ffn_chunk(xn, u_prev, wup_ref, cw_ref, cb_ref, c)
        uv, val = _ffn_chunk(xn, u_prev, wup_ref, cw_ref, cb_ref, D_FF + c)
        u_sink(ug, c)
        u_sink(uv, D_FF + c)
        h = gate * _sigmoid(gate) * val
        acc = acc + _dot(h.astype(BF16), wdn_ref[c:c + FF_CHUNK, :])
    out_ref[...] = acc


def _ffn_seq_kernel(x_ref, g2_ref, wup_ref, cw_ref, cb_ref, wdn_ref, out_ref, ust_ref, carry_ref):
    @pl.when(pl.program_id(1) == 0)
    def _():
        carry_ref[...] = jnp.zeros_like(carry_ref)

    tm = x_ref.shape[0]
    row = lax.broadcasted_iota(jnp.int32, (tm, FF_CHUNK), 0)

    def u_prev(u, c0, c1):
        c6 = carry_ref[6:7, c0:c1]
        c7 = carry_ref[7:8, c0:c1]
        u_m1 = jnp.where(row == 0, c7, pltpu.roll(u, 1, 0))
        u_m2 = jnp.where(row == 0, c6, jnp.where(row == 1, c7, pltpu.roll(u, 2, 0)))
        return u_m2, u_m1

    def u_sink(u, c0):
        carry_ref[:, c0:c0 + FF_CHUNK] = u[tm - 8:tm, :]

    _ffn_body(x_ref, g2_ref, wup_ref, cw_ref, cb_ref, wdn_ref, out_ref, u_prev, u_sink)
    ust_ref[...] = carry_ref[...]


def _ffn_step_kernel(x_ref, p0_ref, p1_ref, g2_ref, wup_ref, cw_ref, cb_ref, wdn_ref, out_ref, u_ref):
    def u_prev(u, c0, c1):
        return p0_ref[:, c0:c1], p1_ref[:, c0:c1]

    def u_sink(u, c0):
        u_ref[:, c0:c0 + FF_CHUNK] = u

    _ffn_body(x_ref, g2_ref, wup_ref, cw_ref, cb_ref, wdn_ref, out_ref, u_prev, u_sink)


def _ffn(x2d, conv_prev, ln2_g, wup, cw, cb, wdn, batch, seq, tm):
    m = x2d.shape[0]
    w_specs = [_resident((1, D_MODEL)), _resident((D_MODEL, 2 * D_FF)), _resident((3, 2 * D_FF)),
               _resident((1, 2 * D_FF)), _resident((D_FF, D_MODEL))]
    if conv_prev is None:
        nt = seq // tm
        row = pl.BlockSpec((tm, D_MODEL), lambda b, t: (b * nt + t, 0))
        return pl.pallas_call(
            _ffn_seq_kernel,
            out_shape=(jax.ShapeDtypeStruct((m, D_MODEL), F32), jax.ShapeDtypeStruct((batch, 8, 2 * D_FF), F32)),
            grid=(batch, nt),
            in_specs=[row] + w_specs,
            out_specs=(row, pl.BlockSpec((None, 8, 2 * D_FF), lambda b, t: (b, 0, 0))),
            scratch_shapes=[pltpu.VMEM((8, 2 * D_FF), F32)],
            compiler_params=_cparams("parallel", "arbitrary"), name="ffn_seq",
        )(x2d, ln2_g, wup, cw, cb, wdn)
    row = lambda w: pl.BlockSpec((tm, w), lambda i: (i, 0))
    return pl.pallas_call(
        _ffn_step_kernel,
        out_shape=(jax.ShapeDtypeStruct((m, D_MODEL), F32), jax.ShapeDtypeStruct((m, 2 * D_FF), F32)),
        grid=(m // tm,),
        in_specs=[row(D_MODEL), row(2 * D_FF), row(2 * D_FF)] + w_specs,
        out_specs=(row(D_MODEL), row(2 * D_FF)),
        compiler_params=_cparams("parallel"), name="ffn_step",
    )(x2d, conv_prev[0], conv_prev[1], ln2_g, wup, cw, cb, wdn)


def _to_state_lanes(x, batch, seq):
    x = x.reshape(batch, seq, RWKV_HEADS, WKV_Q, WKV_SUB)
    return x.transpose(1, 4, 3, 0, 2).reshape(seq, WKV_SUB, LANES)


def _from_state_lanes(y, batch, seq):
    y = y.reshape(seq, WKV_SUB, WKV_Q, batch, RWKV_HEADS)
    return y.transpose(3, 0, 4, 2, 1).reshape(batch * seq, RWKV_WIDTH)


def _shift_next(x):
    return jnp.concatenate([x[1:], jnp.zeros_like(x[:1])], axis=0)


def _pad_rows(w, before, total):
    return jnp.zeros((total, w.shape[1]), w.dtype).at[before:before + w.shape[0]].set(w)


def kernel(x_prompt, x_sample, cache_k_w128, cache_v_w128, cache_k_w512, cache_v_w512, cache_k_w2048, cache_v_w2048,
           state_rwkv_shift, state_rwkv_wkv, state_ffn_conv, ln1_g, w_in, q_norm_g, k_norm_g, w_attn_out,
           rwkv_mu, rwkv_w0, rwkv_w2, rwkv_a0, rwkv_a2, rwkv_g2, rwkv_k_k, rwkv_k_a, rwkv_r_k, rwkv_gn_g, rwkv_gn_b,
           w_rwkv_out, w_o, ln2_g, w_up, conv_w, conv_b, w_down):
    batch, seq, _ = x_prompt.shape
    nsamp = x_sample.shape[0]
    assert seq % SEG == 0 and x_sample.shape[1] == 1 and batch * RWKV_HEADS * WKV_Q == LANES
    assert ln1_g.shape[0] == 1, "single layer"
    lyr = 0

    w_in_bf = w_in[lyr].astype(BF16)
    bd768 = _block_diag_ones(ATT_WIDTH)
    bd512 = _block_diag_ones(RWKV_WIDTH)
    gq = jnp.tile(q_norm_g[lyr], ATT_WIDTH // HEAD_DIM)[None]
    gk = jnp.tile(k_norm_g[lyr], ATT_WIDTH // HEAD_DIM)[None]
    row1 = lambda a: a.reshape(1, -1)
    prep_w = (row1(rwkv_mu[lyr]), row1(rwkv_w0[lyr]), row1(rwkv_a0[lyr]),
              *_split_bf16(_pad_rows(rwkv_w2[lyr], 0, DECAY_LORA + ICLR_LORA)),
              *_split_bf16(_pad_rows(rwkv_a2[lyr], DECAY_LORA, DECAY_LORA + ICLR_LORA)),
              *_split_bf16(rwkv_g2[lyr]),
              row1(rwkv_k_k[lyr]), row1(rwkv_k_a[lyr]), row1(rwkv_r_k[lyr]), bd512)
    merge_w = (row1(rwkv_gn_g[lyr]), row1(rwkv_gn_b[lyr]), bd512, w_attn_out[lyr].astype(BF16),
               w_rwkv_out[lyr].astype(BF16), w_o[lyr].astype(BF16))
    ffn_w = (row1(ln2_g[lyr]), w_up[lyr].astype(BF16), conv_w[lyr], row1(conv_b[lyr]), w_down[lyr].astype(BF16))

    xp = x_prompt.reshape(batch * seq, D_MODEL)
    qn, kn, v, pr, gates = _proj(xp, row1(ln1_g[lyr]), w_in_bf, bd768, gq, gk, tm=256)
    att = [_attn_prompt(qn, kn, v, g, batch, seq) for g in range(N_GROUPS)]
    na, dec, nb, kp, rp, rv, vkr, bonus, gg = _prep(pr, None, prep_w, batch, seq, tm=256)
    lanes = lambda a: _to_state_lanes(a, batch, seq)
    yw, s_fin = _wkv_seq(lanes(dec), lanes(nb), lanes(kp), _shift_next(lanes(na)), _shift_next(lanes(rp)), lanes(rv))
    yw = _from_state_lanes(yw, batch, seq)
    x2 = _merge(xp, [a[0] for a in att], [a[1] for a in att], yw, vkr, bonus, gg, gates, *merge_w, tm=256)
    yp, ust = _ffn(x2, None, *ffn_w, batch, seq, tm=256)

    y_prompt = yp.reshape(batch, seq, D_MODEL)
    k5 = kn.reshape(batch, seq, N_GROUPS, HEADS_PER_GROUP, HEAD_DIM)
    v5 = v.reshape(batch, seq, N_GROUPS, HEADS_PER_GROUP, HEAD_DIM)
    kv_prompt = []
    for g, (win, _) in enumerate(ATT_GROUPS):
        keep = min(win, seq)
        kv_prompt += [k5[None, :, seq - keep:, g], v5[None, :, seq - keep:, g]]
    shift_prompt = pr.reshape(batch, seq, RWKV_PROJ)[None, :, -1]
    wkv_prompt = (s_fin.reshape(RWKV_HEAD, WKV_SUB, WKV_Q, batch, RWKV_HEADS)
                  .transpose(3, 4, 2, 1, 0).reshape(1, batch, RWKV_HEADS, RWKV_HEAD, RWKV_HEAD))
    conv_prompt = ust[None, :, 6:8]

    xs = x_sample.reshape(nsamp, D_MODEL)
    qs, ks, vs, prs, gates_s = _proj(xs, row1(ln1_g[lyr]), w_in_bf, bd768, gq, gk, tm=nsamp)
    caches = [c[lyr].transpose(0, 2, 3, 1).reshape(nsamp, GROUP_WIDTH, c.shape[2])
              for c in (cache_k_w128, cache_v_w128, cache_k_w512, cache_v_w512, cache_k_w2048, cache_v_w2048)]
    ks_t, vs_t = ks.T, vs.T
    kv_sample, o_s, lse_s = [], [], []
    for g, (win, dil) in enumerate(ATT_GROUPS):
        assert caches[2 * g].shape[2] == BLOCK * dil, "window caches must be full"
        k_out, v_out, o_g, lse_g = _cache_attn(qs, ks, vs, ks_t, vs_t, caches[2 * g], caches[2 * g + 1], g,
                                               bb=8 if win <= 512 else 2)
        kv_sample += [c.reshape(nsamp, HEADS_PER_GROUP, HEAD_DIM, win).transpose(0, 3, 1, 2)[None]
                      for c in (k_out, v_out)]
        o_s.append(o_g)
        lse_s.append(lse_g)
    na, dec, nb, kp, rp, rv, vkr, bonus, gg = _prep(prs, state_rwkv_shift[lyr], prep_w, nsamp, 1, tm=nsamp)
    cols = lambda a: a.reshape(nsamp * RWKV_HEADS, RWKV_HEAD).T
    state_t = state_rwkv_wkv[lyr].transpose(3, 2, 0, 1).reshape(RWKV_HEAD, RWKV_HEAD, nsamp * RWKV_HEADS)
    yw_s, state_new = _wkv_step(state_t, cols(dec), cols(nb), cols(kp), cols(na), cols(rp), cols(rv))
    yw_s = yw_s.T.reshape(nsamp, RWKV_WIDTH)
    x2s = _merge(xs, o_s, lse_s, yw_s, vkr, bonus, gg, gates_s, *merge_w, tm=nsamp)
    conv_prev = state_ffn_conv[lyr]
    ys, u_s = _ffn(x2s, (conv_prev[:, 0], conv_prev[:, 1]), *ffn_w, nsamp, 1, tm=nsamp)

    y_sample = ys.reshape(nsamp, 1, D_MODEL)
    shift_sample = prs[None]
    wkv_sample = (state_new.reshape(RWKV_HEAD, RWKV_HEAD, nsamp, RWKV_HEADS)
                  .transpose(2, 3, 1, 0)[None])
    conv_sample = jnp.stack([conv_prev[:, 1], u_s], axis=1)[None]

    return (y_prompt, y_sample,
            kv_prompt[0], kv_sample[0], kv_prompt[1], kv_sample[1],
            kv_prompt[2], kv_sample[2], kv_prompt[3], kv_sample[3],
            kv_prompt[4], kv_sample[4], kv_prompt[5], kv_sample[5],
            shift_prompt, shift_sample, wkv_prompt, wkv_sample, conv_prompt, conv_sample)
```

```python
import functools

import jax
import jax.numpy as jnp
from jax import lax
from jax.experimental import pallas as pl
from jax.experimental.pallas import tpu as pltpu

F32 = jnp.float32
BF16 = jnp.bfloat16

D_MODEL = 1024
HEAD_DIM = 64
ATT_GROUPS = ((128, 1), (512, 4), (2048, 16))
N_GROUPS = len(ATT_GROUPS)
HEADS_PER_GROUP = 4
GROUP_WIDTH = HEADS_PER_GROUP * HEAD_DIM
ATT_WIDTH = N_GROUPS * GROUP_WIDTH
BLOCK = 128
SEG = 2048
RWKV_HEADS = 8
RWKV_HEAD = 64
RWKV_WIDTH = RWKV_HEADS * RWKV_HEAD
DECAY_LORA = 64
ICLR_LORA = 64
GATE_LORA = 128
RWKV_PROJ = 3 * RWKV_WIDTH + DECAY_LORA + ICLR_LORA + GATE_LORA
LORA_OFF = 3 * RWKV_WIDTH
GATE_OFF = LORA_OFF + DECAY_LORA + ICLR_LORA
PR_OFF = 3 * ATT_WIDTH
GATES_OFF = PR_OFF + RWKV_PROJ
P_TOTAL = GATES_OFF + 2 * D_MODEL
D_FF = 2816
FF_CHUNK = 256
RMS_EPS = 1e-6
GN_EPS = 64e-5
L2_EPS = 1e-12
NEG_BIG = -1e30

LANES = 128
SEG_TILE = 256
WKV_Q = 4
WKV_SUB = RWKV_HEAD // WKV_Q
WKV_TB = 128
VMEM_LIMIT = 56 * 1024 * 1024


def _cparams(*sem):
    return pltpu.CompilerParams(dimension_semantics=sem, vmem_limit_bytes=VMEM_LIMIT)


def _resident(shape):
    n = len(shape)
    return pl.BlockSpec(shape, lambda *_: (0,) * n, pipeline_mode=pl.Buffered(1))


def _split_bf16(x):
    hi = x.astype(BF16)
    lo = (x - hi.astype(F32)).astype(BF16)
    return hi, lo


def _dot(a, b):
    return jnp.dot(a, b, preferred_element_type=F32)


def _dot3(a, b_hi, b_lo):
    a_hi, a_lo = _split_bf16(a)
    return _dot(a_hi, b_hi) + _dot(a_lo, b_hi) + _dot(a_hi, b_lo)


def _seg_sum_bf16(x_bf, ones_bd):
    return jnp.concatenate([_dot(x_bf[:, c:c + SEG_TILE], ones_bd) for c in range(0, x_bf.shape[1], SEG_TILE)],
                           axis=1)


def _seg_sum(x, ones_bd):
    hi, lo = _split_bf16(x)
    return _seg_sum_bf16(hi, ones_bd) + _seg_sum_bf16(lo, ones_bd)


def _sigmoid(x):
    return 1.0 / (1.0 + jnp.exp(-x))


def _block_diag_ones(width):
    idx = jnp.arange(width) // HEAD_DIM
    return (idx[:, None] == idx[None, :]).astype(BF16)


def _proj_kernel(x_ref, g1_ref, w_ref, bd_ref, gq_ref, gk_ref, q_ref, k_ref, v_ref, pr_ref, gate_ref):
    x = x_ref[...]
    ms = jnp.mean(x * x, axis=-1, keepdims=True)
    xn = (x * lax.rsqrt(ms + RMS_EPS) * g1_ref[...]).astype(BF16)

    def proj(c0, c1):
        return _dot(xn, w_ref[:, c0:c1])

    def head_norm(z, g):
        ss = _seg_sum_bf16((z * z).astype(BF16), bd_ref[...])
        return z * lax.rsqrt(ss * (1.0 / HEAD_DIM) + RMS_EPS) * g

    q_ref[...] = head_norm(proj(0, ATT_WIDTH), gq_ref[...])
    k_ref[...] = head_norm(proj(ATT_WIDTH, 2 * ATT_WIDTH), gk_ref[...])
    v_ref[...] = proj(2 * ATT_WIDTH, PR_OFF)
    pr_ref[...] = proj(PR_OFF, GATES_OFF)
    gate_ref[...] = proj(GATES_OFF, P_TOTAL)


def _proj(x2d, ln1_g, w_in_bf, bd768, gq, gk, tm):
    m = x2d.shape[0]
    row = lambda w: pl.BlockSpec((tm, w), lambda i: (i, 0))
    widths = (ATT_WIDTH, ATT_WIDTH, ATT_WIDTH, RWKV_PROJ, 2 * D_MODEL)
    return pl.pallas_call(
        _proj_kernel,
        out_shape=tuple(jax.ShapeDtypeStruct((m, w), F32) for w in widths),
        grid=(m // tm,),
        in_specs=[row(D_MODEL), _resident((1, D_MODEL)), _resident((D_MODEL, P_TOTAL)),
                  _resident((SEG_TILE, SEG_TILE)), _resident((1, ATT_WIDTH)), _resident((1, ATT_WIDTH))],
        out_specs=tuple(row(w) for w in widths),
        compiler_params=_cparams("parallel"),
        name="proj",
    )(x2d, ln1_g, w_in_bf, bd768, gq, gk)


def _attn_prompt_kernel(*refs, dil):
    q_ref, kp_ref, kc_ref, vp_ref, vc_ref = (refs[2 * i:2 * i + 2] for i in range(5))
    o_ref, lse_ref = refs[10], refs[11]
    not_first_seg = pl.program_id(1) > 0
    nblk = SEG // (BLOCK * dil)
    stacked = (HEADS_PER_GROUP * BLOCK, 2 * BLOCK)
    row = lax.broadcasted_iota(jnp.int32, stacked, 0) % BLOCK
    col = lax.broadcasted_iota(jnp.int32, stacked, 1)
    in_prev = col < BLOCK
    band_prev = in_prev & (col >= row)
    band_cur = jnp.logical_not(in_prev) & (col - BLOCK <= row)
    lane = lax.broadcasted_iota(jnp.int32, (1, GROUP_WIDTH), 1)
    head_masks = [(lane // HEAD_DIM) == h for h in range(HEADS_PER_GROUP)]

    def row_slice(start):
        return pl.ds(start, BLOCK) if dil == 1 else pl.ds(start, BLOCK, stride=dil)

    def rows(halves, start):
        return jnp.concatenate([h[row_slice(start), :] for h in halves], axis=1)

    def block(q_start, k1_ref, k1_start, v1_ref, prev_valid):
        q = rows(q_ref, q_start) * (HEAD_DIM ** -0.5)
        q4 = jnp.concatenate([jnp.where(hm, q, 0.0) for hm in head_masks], axis=0).astype(BF16)
        kt = jnp.concatenate([rows(k1_ref, k1_start), rows(kc_ref, q_start)], axis=0).astype(BF16)
        vt = jnp.concatenate([rows(v1_ref, k1_start), rows(vc_ref, q_start)], axis=0).astype(BF16)
        s = lax.dot_general(q4, kt, (((1,), (1,)), ((), ())), preferred_element_type=F32)
        s = jnp.where(band_cur | (band_prev & prev_valid), s, NEG_BIG)
        m = jnp.max(s, axis=-1, keepdims=True)
        p = jnp.exp(s - m)
        l = jnp.sum(p, axis=-1, keepdims=True)
        ov = _dot(p.astype(BF16), vt) / l
        lse = m + jnp.log(l)
        o = jnp.zeros((BLOCK, GROUP_WIDTH), F32)
        lse_b = jnp.zeros((BLOCK, GROUP_WIDTH), F32)
        for h, hm in enumerate(head_masks):
            sl = slice(h * BLOCK, (h + 1) * BLOCK)
            o = jnp.where(hm, ov[sl], o)
            lse_b = jnp.where(hm, lse[sl], lse_b)
        for half in range(2):
            lanes = slice(half * LANES, (half + 1) * LANES)
            o_ref[half, row_slice(q_start), :] = o[:, lanes]
            lse_ref[half, row_slice(q_start), :] = lse_b[:, lanes]

    def first_blocks(r, carry):
        block(r, kp_ref, SEG - BLOCK * dil + r, vp_ref, not_first_seg)
        return carry

    def later_blocks(i, carry):
        r = i % dil
        nb = 1 + i // dil
        q_start = nb * BLOCK * dil + r
        block(q_start, kc_ref, q_start - BLOCK * dil, vc_ref, True)
        return carry

    lax.fori_loop(0, dil, first_blocks, 0)
    if nblk > 1:
        lax.fori_loop(0, dil * (nblk - 1), later_blocks, 0)


def _attn_prompt(qn, kn, v, group, batch, seq):
    dil = ATT_GROUPS[group][1]
    nseg = seq // SEG
    halves = GROUP_WIDTH // LANES
    cur = [pl.BlockSpec((SEG, LANES), lambda b, s, c=halves * group + h: (b * nseg + s, c)) for h in range(halves)]
    prev = [pl.BlockSpec((SEG, LANES), lambda b, s, c=halves * group + h: (b * nseg + jnp.maximum(s - 1, 0), c))
            for h in range(halves)]
    out = pl.BlockSpec((halves, SEG, LANES), lambda b, s: (0, b * nseg + s, 0))
    shape = jax.ShapeDtypeStruct((halves, batch * seq, LANES), F32)
    return pl.pallas_call(
        functools.partial(_attn_prompt_kernel, dil=dil),
        out_shape=(shape, shape),
        grid=(batch, nseg),
        in_specs=cur + prev + cur + prev + cur,
        out_specs=(out, out),
        compiler_params=_cparams("parallel", "arbitrary"),
        name=f"attn_prompt_g{group}",
    )(qn, qn, kn, kn, kn, kn, v, v, v, v)


def _cache_attn_kernel(q_ref, kn_ref, vn_ref, knt_ref, vnt_ref, kc_ref, vc_ref, ko_ref, vo_ref, o_ref, lse_ref,
                       *, group, dil, bb):
    win = kc_ref.shape[2]
    nseq = q_ref.shape[0]
    cols = slice(group * GROUP_WIDTH, (group + 1) * GROUP_WIDTH)
    pos = lax.broadcasted_iota(jnp.int32, (1, win), 1)
    in_window = (pos % dil) == 0
    newest = lax.broadcasted_iota(jnp.int32, (GROUP_WIDTH, win), 1) == win - 1
    head_rows = (lax.broadcasted_iota(jnp.int32, (8, GROUP_WIDTH), 0)
                 == lax.broadcasted_iota(jnp.int32, (8, GROUP_WIDTH), 1) // HEAD_DIM)
    seq_lane = lax.broadcasted_iota(jnp.int32, (GROUP_WIDTH, nseq), 1)

    def hi_lo_rows(x):
        hi, lo = _split_bf16(x)
        return jnp.concatenate([hi, lo], axis=0)

    def fold(x):
        return x[0:8] + x[8:16]

    def shifted(cache, new_t_ref, n):
        new_col = jnp.sum(jnp.where(seq_lane == n, new_t_ref[cols, :], 0.0), axis=1, keepdims=True)
        return jnp.where(newest, new_col, pltpu.roll(cache, win - 1, 1))

    for b in range(bb):
        n = pl.program_id(0) * bb + b
        q = q_ref[pl.ds(n, 1), cols] * (HEAD_DIM ** -0.5)
        k_new = kn_ref[pl.ds(n, 1), cols]
        v_new = vn_ref[pl.ds(n, 1), cols]
        q8 = jnp.where(head_rows, q, 0.0)
        kc = kc_ref[b]
        vc = vc_ref[b]
        s = fold(_dot(hi_lo_rows(q8), kc.astype(BF16)))
        s = jnp.where(in_window, s, NEG_BIG)
        s_new = jnp.sum(q8 * k_new, axis=1, keepdims=True)
        m = jnp.maximum(jnp.max(s, axis=1, keepdims=True), s_new)
        p = jnp.exp(s - m)
        p_new = jnp.exp(s_new - m)
        l = jnp.sum(p, axis=1, keepdims=True) + p_new
        pv = fold(lax.dot_general(hi_lo_rows(p), vc.astype(BF16), (((1,), (1,)), ((), ())),
                                  preferred_element_type=F32))
        ov = (pv + p_new * v_new) / l
        o = jnp.sum(jnp.where(head_rows, ov, 0.0), axis=0, keepdims=True)
        lse = jnp.sum(jnp.where(head_rows, m + jnp.log(l), 0.0), axis=0, keepdims=True)
        for half in range(GROUP_WIDTH // LANES):
            lanes = slice(half * LANES, (half + 1) * LANES)
            o_ref[half, pl.ds(n, 1), :] = o[:, lanes]
            lse_ref[half, pl.ds(n, 1), :] = lse[:, lanes]
        ko_ref[b] = shifted(kc, knt_ref, n)
        vo_ref[b] = shifted(vc, vnt_ref, n)


def _cache_attn(qn, kn, v, kn_t, v_t, k_cache, v_cache, group, bb):
    n, _, win = k_cache.shape
    dil = ATT_GROUPS[group][1]
    halves = GROUP_WIDTH // LANES
    cache = pl.BlockSpec((bb, GROUP_WIDTH, win), lambda i: (i, 0, 0))
    rows = _resident((n, ATT_WIDTH))
    rows_t = _resident((ATT_WIDTH, n))
    out = pl.BlockSpec((halves, n, LANES), lambda i: (0, 0, 0))
    cache_shape = jax.ShapeDtypeStruct(k_cache.shape, F32)
    o_shape = jax.ShapeDtypeStruct((halves, n, LANES), F32)
    return pl.pallas_call(
        functools.partial(_cache_attn_kernel, group=group, dil=dil, bb=bb),
        out_shape=(cache_shape, cache_shape, o_shape, o_shape),
        grid=(n // bb,),
        in_specs=[rows, rows, rows, rows_t, rows_t, cache, cache],
        out_specs=(cache, cache, out, out),
        compiler_params=_cparams("arbitrary"),
        name=f"cache_attn_g{group}",
    )(qn, kn, v, kn_t, v_t, k_cache, v_cache)


def _prep_math(p, prev, mu_ref, w0_ref, a0_ref, w2h_ref, w2l_ref, a2h_ref, a2l_ref, g2h_ref, g2l_ref,
               kk_ref, ka_ref, rk_ref, bd_ref, outs):
    na_ref, dec_ref, nb_ref, kp_ref, rp_ref, v_ref, vkr_ref, bonus_ref, g_ref = outs
    bd = bd_ref[...]

    def put(ref, val):
        ref[...] = val.T

    xs = p + mu_ref[...] * (prev - p)
    r = xs[:, 0:RWKV_WIDTH]
    k = xs[:, RWKV_WIDTH:2 * RWKV_WIDTH]
    v = xs[:, 2 * RWKV_WIDTH:3 * RWKV_WIDTH]
    lora = xs[:, LORA_OFF:GATE_OFF]
    gl = xs[:, GATE_OFF:RWKV_PROJ]
    z = w0_ref[...] + _dot3(jnp.tanh(lora), w2h_ref[...], w2l_ref[...])
    softplus_neg = jnp.maximum(-z, 0.0) + jnp.log(1.0 + jnp.exp(-jnp.abs(z)))
    w = -softplus_neg - 0.5
    dec = jnp.exp(-jnp.exp(w))
    a = _sigmoid(a0_ref[...] + _dot3(lora, a2h_ref[...], a2l_ref[...]))
    g = _dot3(_sigmoid(gl), g2h_ref[...], g2l_ref[...])
    kk = k * kk_ref[...]
    kk = kk / jnp.maximum(jnp.sqrt(_seg_sum(kk * kk, bd)), L2_EPS)
    kp = k * (1.0 + (a - 1.0) * ka_ref[...])
    na = -kk
    nb = kk * a
    rp = dec * r + na * _seg_sum(nb * r, bd)
    put(na_ref, na)
    put(dec_ref, dec)
    put(nb_ref, nb)
    put(kp_ref, kp)
    put(rp_ref, rp)
    put(v_ref, v)
    vkr_ref[...] = v * _seg_sum(kp * r, bd)
    bonus_ref[...] = _seg_sum(r * kp * rk_ref[...], bd) * v
    g_ref[...] = g


def _prep_seq_kernel(p_ref, *rest):
    weights, outs, carry_ref = rest[:13], rest[13:22], rest[22]

    @pl.when(pl.program_id(1) == 0)
    def _():
        carry_ref[...] = jnp.zeros_like(carry_ref)

    p = p_ref[...]
    tm = p.shape[0]
    row = lax.broadcasted_iota(jnp.int32, p.shape, 0)
    prev = jnp.where(row == 0, carry_ref[7:8, :], pltpu.roll(p, 1, 0))
    carry_ref[7:8, :] = p[tm - 1:tm, :]
    _prep_math(p, prev, *weights, outs)


def _prep_step_kernel(p_ref, prev_ref, *rest):
    weights, outs = rest[:13], rest[13:22]
    _prep_math(p_ref[...], prev_ref[...], *weights, outs)


def _prep(pr, shift_prev, weights, batch, seq, tm):
    m = pr.shape[0]
    w_specs = [_resident(w.shape) for w in weights]
    out_shape = tuple(jax.ShapeDtypeStruct((m, RWKV_WIDTH), F32) for _ in range(9))
    if shift_prev is None:
        nt = seq // tm
        row = lambda w: pl.BlockSpec((tm, w), lambda b, t: (b * nt + t, 0))
        chan = pl.BlockSpec((None, RWKV_WIDTH, tm), lambda b, t: (b, 0, t))
        chan_shape = jax.ShapeDtypeStruct((batch, RWKV_WIDTH, seq), F32)
        return pl.pallas_call(
            _prep_seq_kernel, out_shape=(chan_shape,) * 6 + out_shape[6:], grid=(batch, nt),
            in_specs=[row(RWKV_PROJ)] + w_specs, out_specs=(chan,) * 6 + tuple(row(RWKV_WIDTH) for _ in range(3)),
            scratch_shapes=[pltpu.VMEM((8, RWKV_PROJ), F32)],
            compiler_params=_cparams("parallel", "arbitrary"), name="rwkv_prep_seq",
        )(pr, *weights)
    row = lambda w: pl.BlockSpec((tm, w), lambda i: (i, 0))
    chan = pl.BlockSpec((RWKV_WIDTH, tm), lambda i: (0, i))
    chan_shape = jax.ShapeDtypeStruct((RWKV_WIDTH, m), F32)
    return pl.pallas_call(
        _prep_step_kernel, out_shape=(chan_shape,) * 6 + out_shape[6:], grid=(m // tm,),
        in_specs=[row(RWKV_PROJ), row(RWKV_PROJ)] + w_specs,
        out_specs=(chan,) * 6 + tuple(row(RWKV_WIDTH) for _ in range(3)),
        compiler_params=_cparams("parallel"), name="rwkv_prep_step",
    )(pr, shift_prev, *weights)


def _wkv_seq_kernel(w_ref, nb_ref, k_ref, na_ref, rp_ref, v_ref, y_ref, sout_ref, s_ref, e_ref, vt_ref, yt_ref):
    @pl.when(pl.program_id(0) == 0)
    def _():
        s_ref[...] = jnp.zeros_like(s_ref)

    nbatch = w_ref.shape[0]
    tb = w_ref.shape[2]
    zero = jnp.zeros((WKV_SUB, LANES), F32)

    def head_rows(ref, b, c):
        return ref[b, pl.ds(c, RWKV_HEADS, stride=RWKV_HEAD), :]

    unroll = 4

    def build_e(jg, carry):
        for j in [jg * unroll + u for u in range(unroll)]:
            for o, ref in enumerate((w_ref, nb_ref, k_ref, na_ref, rp_ref)):
                rows = jnp.concatenate([head_rows(ref, b, j) for b in range(nbatch)], axis=0)
                e_ref[o, j] = jnp.concatenate([rows] * WKV_Q, axis=0).T
        return carry

    def build_v(ig, carry):
        for ih in [ig * unroll + u for u in range(unroll)]:
            rows = jnp.concatenate([head_rows(v_ref, b, il * WKV_SUB + ih)
                                    for il in range(WKV_Q) for b in range(nbatch)], axis=0)
            vt_ref[pl.ds(ih, tb, stride=WKV_SUB), :] = rows.T
        return carry

    lax.fori_loop(0, RWKV_HEAD // unroll, build_e, 0)
    lax.fori_loop(0, WKV_SUB // unroll, build_v, 0)

    def first_matvecs(j, carry):
        sa, y = carry
        sj = s_ref[j]
        return sa + sj * e_ref[3, j, 0:1, :], y + sj * e_ref[4, j, 0:1, :]

    def step(t, carry):
        sa, y = carry
        tile = pl.ds(pl.multiple_of(t * WKV_SUB, WKV_SUB), WKV_SUB)
        yt_ref[tile, :] = y
        v = vt_ref[tile, :]
        cur = pl.ds(t, 1)
        nxt = pl.ds(jnp.minimum(t + 1, tb - 1), 1)
        n_acc = 4
        sa_acc = [zero] * n_acc
        y_acc = [zero] * n_acc
        for j in range(RWKV_HEAD):
            sn = s_ref[j] * e_ref[0, j, cur, :] + sa * e_ref[1, j, cur, :] + v * e_ref[2, j, cur, :]
            s_ref[j] = sn
            sa_acc[j % n_acc] = sa_acc[j % n_acc] + sn * e_ref[3, j, nxt, :]
            y_acc[j % n_acc] = y_acc[j % n_acc] + sn * e_ref[4, j, nxt, :]
        return (sa_acc[0] + sa_acc[1]) + (sa_acc[2] + sa_acc[3]), (y_acc[0] + y_acc[1]) + (y_acc[2] + y_acc[3])

    lax.fori_loop(0, tb, step, lax.fori_loop(0, RWKV_HEAD, first_matvecs, (zero, zero)))

    def emit_y(ig, carry):
        for ih in [ig * unroll + u for u in range(unroll)]:
            cols = yt_ref[pl.ds(ih, tb, stride=WKV_SUB), :].T
            for il in range(WKV_Q):
                for b in range(nbatch):
                    r0 = il * (LANES // WKV_Q) + b * RWKV_HEADS
                    y_ref[b, pl.ds(il * WKV_SUB + ih, RWKV_HEADS, stride=RWKV_HEAD), :] = cols[r0:r0 + RWKV_HEADS]
        return carry

    lax.fori_loop(0, WKV_SUB // unroll, emit_y, 0)

    @pl.when(pl.program_id(0) == pl.num_programs(0) - 1)
    def _():
        sout_ref[...] = s_ref[...]


def _wkv_seq(w, nb, k, na, rp, v):
    batch, _, t = w.shape
    blk = pl.BlockSpec((batch, RWKV_WIDTH, WKV_TB), lambda i: (0, 0, i))
    state = pl.BlockSpec((RWKV_HEAD, WKV_SUB, LANES), lambda i: (0, 0, 0))
    return pl.pallas_call(
        _wkv_seq_kernel,
        out_shape=(jax.ShapeDtypeStruct((batch, RWKV_WIDTH, t), F32),
                   jax.ShapeDtypeStruct((RWKV_HEAD, WKV_SUB, LANES), F32)),
        grid=(t // WKV_TB,),
        in_specs=[blk] * 6,
        out_specs=(blk, state),
        scratch_shapes=[pltpu.VMEM((RWKV_HEAD, WKV_SUB, LANES), F32),
                        pltpu.VMEM((5, RWKV_HEAD, WKV_TB, LANES), F32),
                        pltpu.VMEM((WKV_TB * WKV_SUB, LANES), F32),
                        pltpu.VMEM((WKV_TB * WKV_SUB, LANES), F32)],
        compiler_params=_cparams("arbitrary"),
        name="wkv_seq",
    )(w, nb, k, na, rp, v)


def _wkv_step_kernel(s_ref, w_ref, nb_ref, k_ref, na_ref, rp_ref, v_ref, y_ref, sout_ref):
    w, nb, k, na, rp = w_ref[...], nb_ref[...], k_ref[...], na_ref[...], rp_ref[...]
    for i in range(RWKV_HEAD):
        si = s_ref[i]
        sa = jnp.sum(si * na, axis=0, keepdims=True)
        y_ref[i:i + 1, :] = jnp.sum(si * rp, axis=0, keepdims=True)
        sout_ref[i] = si * w + sa * nb + v_ref[i:i + 1, :] * k


def _wkv_step(state, w, nb, k, na, rp, v):
    n = state.shape[-1]
    st = pl.BlockSpec((None, RWKV_HEAD, RWKV_HEAD, n), lambda h: (h, 0, 0, 0))
    vec = pl.BlockSpec((RWKV_HEAD, n), lambda h: (h, 0))
    return pl.pallas_call(
        _wkv_step_kernel,
        out_shape=(jax.ShapeDtypeStruct((RWKV_WIDTH, n), F32), jax.ShapeDtypeStruct(state.shape, F32)),
        grid=(RWKV_HEADS,),
        in_specs=[st] + [vec] * 6,
        out_specs=(vec, st),
        compiler_params=_cparams("parallel"),
        name="wkv_step",
    )(state, w, nb, k, na, rp, v)


def _merge_kernel(x_ref, o0_ref, o1_ref, o2_ref, l0_ref, l1_ref, l2_ref, yw_ref, vkr_ref, bonus_ref, g_ref,
                  gate_ref, gng_ref, gnb_ref, bd_ref, wa_ref, wr_ref, wo_ref, out_ref, *, yw_channels_first):
    wide = lambda ref: jnp.concatenate([ref[0], ref[1]], axis=1)
    l0, l1, l2 = wide(l0_ref), wide(l1_ref), wide(l2_ref)
    lm = jnp.maximum(jnp.maximum(l0, l1), l2)
    e0, e1, e2 = jnp.exp(l0 - lm), jnp.exp(l1 - lm), jnp.exp(l2 - lm)
    o_att = (e0 * wide(o0_ref) + e1 * wide(o1_ref) + e2 * wide(o2_ref)) / (e0 + e1 + e2)
    y_att = _dot(o_att.astype(BF16), wa_ref[...])

    bd = bd_ref[...]
    y = (yw_ref[...].T if yw_channels_first else yw_ref[...]) + vkr_ref[...]
    mean = _seg_sum(y, bd) * (1.0 / RWKV_HEAD)
    yc = y - mean
    var = _seg_sum(yc * yc, bd) * (1.0 / RWKV_HEAD)
    yn = yc * lax.rsqrt(var + GN_EPS) * gng_ref[...] + gnb_ref[...]
    o_rwkv = (yn + bonus_ref[...]) * g_ref[...]
    y_rwkv = _dot(o_rwkv.astype(BF16), wr_ref[...])

    merged = _sigmoid(gate_ref[:, 0:D_MODEL]) * y_att + _sigmoid(gate_ref[:, D_MODEL:2 * D_MODEL]) * y_rwkv
    out_ref[...] = x_ref[...] + _dot(merged.astype(BF16), wo_ref[...])


def _merge(x2d, o_groups, lse_groups, yw, vkr, bonus, g, gates, gn_g, gn_b, bd512, wa, wr, wo, tm):
    m = x2d.shape[0]
    row = lambda w: pl.BlockSpec((tm, w), lambda i: (i, 0))
    channels_first = yw.ndim == 3
    if channels_first:
        nt = yw.shape[2] // tm
        yw_spec = pl.BlockSpec((None, RWKV_WIDTH, tm), lambda i: (i // nt, 0, i % nt))
    else:
        yw_spec = row(RWKV_WIDTH)
    return pl.pallas_call(
        functools.partial(_merge_kernel, yw_channels_first=channels_first),
        out_shape=jax.ShapeDtypeStruct((m, D_MODEL), F32),
        grid=(m // tm,),
        in_specs=[row(D_MODEL)] + [pl.BlockSpec((GROUP_WIDTH // LANES, tm, LANES), lambda i: (0, i, 0))] * 6
                 + [yw_spec] + [row(RWKV_WIDTH)] * 3 + [row(2 * D_MODEL)]
                 + [_resident((1, RWKV_WIDTH)), _resident((1, RWKV_WIDTH)), _resident((SEG_TILE, SEG_TILE)),
                    _resident((GROUP_WIDTH, D_MODEL)), _resident((RWKV_WIDTH, D_MODEL)),
                    _resident((D_MODEL, D_MODEL))],
        out_specs=row(D_MODEL),
        compiler_params=_cparams("parallel"),
        name="merge",
    )(x2d, *o_groups, *lse_groups, yw, vkr, bonus, g, gates, gn_g, gn_b, bd512, wa, wr, wo)


def _ffn_body(x_ref, g2_ref, wup_ref, cw_ref, cb_ref, wdn_ref, out_ref, u_prev, u_sink):
    x = x_ref[...]
    ms = jnp.mean(x * x, axis=-1, keepdims=True)
    xn = (x * lax.rsqrt(ms + RMS_EPS) * g2_ref[...]).astype(BF16)

    def up(c):
        return tuple(_dot(xn, wup_ref[:, c0:c0 + FF_CHUNK]) for c0 in (c, D_FF + c))

    def conv(u, c0):
        c1 = c0 + FF_CHUNK
        u_m2, u_m1 = u_prev(u, c0, c1)
        u_sink(u, c0)
        return cb_ref[:, c0:c1] + u_m2 * cw_ref[0:1, c0:c1] + u_m1 * cw_ref[1:2, c0:c1] + u * cw_ref[2:3, c0:c1]

    acc = x
    nxt = up(0)
    for c in range(0, D_FF, FF_CHUNK):
        ug, uv = nxt
        if c + FF_CHUNK < D_FF:
            nxt = up(c + FF_CHUNK)
        gate = conv(ug, c)
        val = conv(uv, D_FF + c)
        h = gate * _sigmoid(gate) * val
        acc = acc + _dot(h.astype(BF16), wdn_ref[c:c + FF_CHUNK, :])
    out_ref[...] = acc


def _ffn_seq_kernel(x_ref, g2_ref, wup_ref, cw_ref, cb_ref, wdn_ref, out_ref, ust_ref, carry_ref):
    @pl.when(pl.program_id(1) == 0)
    def _():
        carry_ref[...] = jnp.zeros_like(carry_ref)

    tm = x_ref.shape[0]
    row = lax.broadcasted_iota(jnp.int32, (tm, FF_CHUNK), 0)

    def u_prev(u, c0, c1):
        c6 = carry_ref[6:7, c0:c1]
        c7 = carry_ref[7:8, c0:c1]
        u_m1 = jnp.where(row == 0, c7, pltpu.roll(u, 1, 0))
        u_m2 = jnp.where(row == 0, c6, jnp.where(row == 1, c7, pltpu.roll(u, 2, 0)))
        return u_m2, u_m1

    def u_sink(u, c0):
        carry_ref[:, c0:c0 + FF_CHUNK] = u[tm - 8:tm, :]

    _ffn_body(x_ref, g2_ref, wup_ref, cw_ref, cb_ref, wdn_ref, out_ref, u_prev, u_sink)
    ust_ref[...] = carry_ref[...]


def _ffn_step_kernel(x_ref, p0_ref, p1_ref, g2_ref, wup_ref, cw_ref, cb_ref, wdn_ref, out_ref, u_ref):
    def u_prev(u, c0, c1):
        return p0_ref[:, c0:c1], p1_ref[:, c0:c1]

    def u_sink(u, c0):
        u_ref[:, c0:c0 + FF_CHUNK] = u

    _ffn_body(x_ref, g2_ref, wup_ref, cw_ref, cb_ref, wdn_ref, out_ref, u_prev, u_sink)


def _ffn(x2d, conv_prev, ln2_g, wup, cw, cb, wdn, batch, seq, tm):
    m = x2d.shape[0]
    w_specs = [_resident((1, D_MODEL)), _resident((D_MODEL, 2 * D_FF)), _resident((3, 2 * D_FF)),
               _resident((1, 2 * D_FF)), _resident((D_FF, D_MODEL))]
    if conv_prev is None:
        nt = seq // tm
        row = pl.BlockSpec((tm, D_MODEL), lambda b, t: (b * nt + t, 0))
        return pl.pallas_call(
            _ffn_seq_kernel,
            out_shape=(jax.ShapeDtypeStruct((m, D_MODEL), F32), jax.ShapeDtypeStruct((batch, 8, 2 * D_FF), F32)),
            grid=(batch, nt),
            in_specs=[row] + w_specs,
            out_specs=(row, pl.BlockSpec((None, 8, 2 * D_FF), lambda b, t: (b, 0, 0))),
            scratch_shapes=[pltpu.VMEM((8, 2 * D_FF), F32)],
            compiler_params=_cparams("parallel", "arbitrary"), name="ffn_seq",
        )(x2d, ln2_g, wup, cw, cb, wdn)
    row = lambda w: pl.BlockSpec((tm, w), lambda i: (i, 0))
    return pl.pallas_call(
        _ffn_step_kernel,
        out_shape=(jax.ShapeDtypeStruct((m, D_MODEL), F32), jax.ShapeDtypeStruct((m, 2 * D_FF), F32)),
        grid=(m // tm,),
        in_specs=[row(D_MODEL), row(2 * D_FF), row(2 * D_FF)] + w_specs,
        out_specs=(row(D_MODEL), row(2 * D_FF)),
        compiler_params=_cparams("parallel"), name="ffn_step",
    )(x2d, conv_prev[0], conv_prev[1], ln2_g, wup, cw, cb, wdn)


def _pad_rows(w, before, total):
    return jnp.zeros((total, w.shape[1]), w.dtype).at[before:before + w.shape[0]].set(w)


def kernel(x_prompt, x_sample, cache_k_w128, cache_v_w128, cache_k_w512, cache_v_w512, cache_k_w2048, cache_v_w2048,
           state_rwkv_shift, state_rwkv_wkv, state_ffn_conv, ln1_g, w_in, q_norm_g, k_norm_g, w_attn_out,
           rwkv_mu, rwkv_w0, rwkv_w2, rwkv_a0, rwkv_a2, rwkv_g2, rwkv_k_k, rwkv_k_a, rwkv_r_k, rwkv_gn_g, rwkv_gn_b,
           w_rwkv_out, w_o, ln2_g, w_up, conv_w, conv_b, w_down):
    batch, seq, _ = x_prompt.shape
    nsamp = x_sample.shape[0]
    assert seq % SEG == 0 and x_sample.shape[1] == 1 and batch * RWKV_HEADS * WKV_Q == LANES
    assert ln1_g.shape[0] == 1, "single layer"
    lyr = 0

    w_in_bf = w_in[lyr].astype(BF16)
    bd768 = bd512 = _block_diag_ones(SEG_TILE)
    gq = jnp.tile(q_norm_g[lyr], ATT_WIDTH // HEAD_DIM)[None]
    gk = jnp.tile(k_norm_g[lyr], ATT_WIDTH // HEAD_DIM)[None]
    row1 = lambda a: a.reshape(1, -1)
    prep_w = (row1(rwkv_mu[lyr]), row1(rwkv_w0[lyr]), row1(rwkv_a0[lyr]),
              *_split_bf16(_pad_rows(rwkv_w2[lyr], 0, DECAY_LORA + ICLR_LORA)),
              *_split_bf16(_pad_rows(rwkv_a2[lyr], DECAY_LORA, DECAY_LORA + ICLR_LORA)),
              *_split_bf16(rwkv_g2[lyr]),
              row1(rwkv_k_k[lyr]), row1(rwkv_k_a[lyr]), row1(rwkv_r_k[lyr]), bd512)
    merge_w = (row1(rwkv_gn_g[lyr]), row1(rwkv_gn_b[lyr]), bd512, w_attn_out[lyr].astype(BF16),
               w_rwkv_out[lyr].astype(BF16), w_o[lyr].astype(BF16))
    ffn_w = (row1(ln2_g[lyr]), w_up[lyr].astype(BF16), conv_w[lyr], row1(conv_b[lyr]), w_down[lyr].astype(BF16))

    xp = x_prompt.reshape(batch * seq, D_MODEL)
    qn, kn, v, pr, gates = _proj(xp, row1(ln1_g[lyr]), w_in_bf, bd768, gq, gk, tm=256)
    att = [_attn_prompt(qn, kn, v, g, batch, seq) for g in range(N_GROUPS)]
    na, dec, nb, kp, rp, rv, vkr, bonus, gg = _prep(pr, None, prep_w, batch, seq, tm=256)
    yw, s_fin = _wkv_seq(dec, nb, kp, na, rp, rv)
    x2 = _merge(xp, [a[0] for a in att], [a[1] for a in att], yw, vkr, bonus, gg, gates, *merge_w, tm=256)
    yp, ust = _ffn(x2, None, *ffn_w, batch, seq, tm=256)

    y_prompt = yp.reshape(batch, seq, D_MODEL)
    k5 = kn.reshape(batch, seq, N_GROUPS, HEADS_PER_GROUP, HEAD_DIM)
    v5 = v.reshape(batch, seq, N_GROUPS, HEADS_PER_GROUP, HEAD_DIM)
    kv_prompt = []
    for g, (win, _) in enumerate(ATT_GROUPS):
        keep = min(win, seq)
        kv_prompt += [k5[None, :, seq - keep:, g], v5[None, :, seq - keep:, g]]
    shift_prompt = pr.reshape(batch, seq, RWKV_PROJ)[None, :, -1]
    wkv_prompt = (s_fin.reshape(RWKV_HEAD, WKV_SUB, WKV_Q, batch, RWKV_HEADS)
                  .transpose(3, 4, 2, 1, 0).reshape(1, batch, RWKV_HEADS, RWKV_HEAD, RWKV_HEAD))
    conv_prompt = ust[None, :, 6:8]

    xs = x_sample.reshape(nsamp, D_MODEL)
    qs, ks, vs, prs, gates_s = _proj(xs, row1(ln1_g[lyr]), w_in_bf, bd768, gq, gk, tm=nsamp)
    caches = [c[lyr].transpose(0, 2, 3, 1).reshape(nsamp, GROUP_WIDTH, c.shape[2])
              for c in (cache_k_w128, cache_v_w128, cache_k_w512, cache_v_w512, cache_k_w2048, cache_v_w2048)]
    ks_t, vs_t = ks.T, vs.T
    kv_sample, o_s, lse_s = [], [], []
    for g, (win, dil) in enumerate(ATT_GROUPS):
        assert caches[2 * g].shape[2] == BLOCK * dil, "window caches must be full"
        k_out, v_out, o_g, lse_g = _cache_attn(qs, ks, vs, ks_t, vs_t, caches[2 * g], caches[2 * g + 1], g,
                                               bb=8 if win <= 512 else 2)
        kv_sample += [c.reshape(nsamp, HEADS_PER_GROUP, HEAD_DIM, win).transpose(0, 3, 1, 2)[None]
                      for c in (k_out, v_out)]
        o_s.append(o_g)
        lse_s.append(lse_g)
    na, dec, nb, kp, rp, rv, vkr, bonus, gg = _prep(prs, state_rwkv_shift[lyr], prep_w, nsamp, 1, tm=nsamp)
    state_t = state_rwkv_wkv[lyr].transpose(1, 2, 3, 0)
    yw_s, state_new = _wkv_step(state_t, dec, nb, kp, na, rp, rv)
    x2s = _merge(xs, o_s, lse_s, yw_s[None], vkr, bonus, gg, gates_s, *merge_w, tm=nsamp)
    conv_prev = state_ffn_conv[lyr]
    ys, u_s = _ffn(x2s, (conv_prev[:, 0], conv_prev[:, 1]), *ffn_w, nsamp, 1, tm=nsamp)

    y_sample = ys.reshape(nsamp, 1, D_MODEL)
    shift_sample = prs[None]
    wkv_sample = state_new.transpose(3, 0, 1, 2)[None]
    conv_sample = jnp.stack([conv_prev[:, 1], u_s], axis=1)[None]

    return (y_prompt, y_sample,
            kv_prompt[0], kv_sample[0], kv_prompt[1], kv_sample[1],
            kv_prompt[2], kv_sample[2], kv_prompt[3], kv_sample[3],
            kv_prompt[4], kv_sample[4], kv_prompt[5], kv_sample[5],
            shift_prompt, shift_sample, wkv_prompt, wkv_sample, conv_prompt, conv_sample)
```

```python
import functools

import jax
import jax.numpy as jnp
from jax import lax
from jax.experimental import pallas as pl
from jax.experimental.pallas import tpu as pltpu

F32 = jnp.float32
BF16 = jnp.bfloat16

D_MODEL = 1024
HEAD_DIM = 64
ATT_GROUPS = ((128, 1), (512, 4), (2048, 16))
N_GROUPS = len(ATT_GROUPS)
HEADS_PER_GROUP = 4
GROUP_WIDTH = HEADS_PER_GROUP * HEAD_DIM
ATT_WIDTH = N_GROUPS * GROUP_WIDTH
BLOCK = 128
SEG = 2048
RWKV_HEADS = 8
RWKV_HEAD = 64
RWKV_WIDTH = RWKV_HEADS * RWKV_HEAD
DECAY_LORA = 64
ICLR_LORA = 64
GATE_LORA = 128
RWKV_PROJ = 3 * RWKV_WIDTH + DECAY_LORA + ICLR_LORA + GATE_LORA
LORA_OFF = 3 * RWKV_WIDTH
GATE_OFF = LORA_OFF + DECAY_LORA + ICLR_LORA
PR_OFF = 3 * ATT_WIDTH
GATES_OFF = PR_OFF + RWKV_PROJ
P_TOTAL = GATES_OFF + 2 * D_MODEL
D_FF = 2816
FF_CHUNK = 256
RMS_EPS = 1e-6
GN_EPS = 64e-5
L2_EPS = 1e-12
NEG_BIG = -1e30

LANES = 128
SEG_TILE = 256
WKV_Q = 4
WKV_SUB = RWKV_HEAD // WKV_Q
WKV_TB = 128
VMEM_LIMIT = 56 * 1024 * 1024


def _cparams(*sem):
    return pltpu.CompilerParams(dimension_semantics=sem, vmem_limit_bytes=VMEM_LIMIT)


def _resident(shape):
    n = len(shape)
    return pl.BlockSpec(shape, lambda *_: (0,) * n, pipeline_mode=pl.Buffered(1))


def _split_bf16(x):
    hi = x.astype(BF16)
    lo = (x - hi.astype(F32)).astype(BF16)
    return hi, lo


def _dot(a, b):
    return jnp.dot(a, b, preferred_element_type=F32)


def _dot3(a, b_hi, b_lo):
    a_hi, a_lo = _split_bf16(a)
    return _dot(a_hi, b_hi) + _dot(a_lo, b_hi) + _dot(a_hi, b_lo)


def _seg_sum_bf16(x_bf, ones_bd):
    return jnp.concatenate([_dot(x_bf[:, c:c + SEG_TILE], ones_bd) for c in range(0, x_bf.shape[1], SEG_TILE)],
                           axis=1)


def _seg_sum(x, ones_bd):
    hi, lo = _split_bf16(x)
    return _seg_sum_bf16(hi, ones_bd) + _seg_sum_bf16(lo, ones_bd)


def _sigmoid(x):
    return 1.0 / (1.0 + jnp.exp(-x))


def _block_diag_ones(width):
    idx = jnp.arange(width) // HEAD_DIM
    return (idx[:, None] == idx[None, :]).astype(BF16)


def _proj_kernel(x_ref, g1_ref, w_ref, bd_ref, gq_ref, gk_ref, q_ref, k_ref, v_ref, pr_ref, gate_ref):
    x = x_ref[...]
    ms = jnp.mean(x * x, axis=-1, keepdims=True)
    xn = (x * lax.rsqrt(ms + RMS_EPS) * g1_ref[...]).astype(BF16)

    def proj(c0, c1):
        return _dot(xn, w_ref[:, c0:c1])

    def head_norm(z, g):
        ss = _seg_sum_bf16((z * z).astype(BF16), bd_ref[...])
        return z * lax.rsqrt(ss * (1.0 / HEAD_DIM) + RMS_EPS) * g

    q_ref[...] = head_norm(proj(0, ATT_WIDTH), gq_ref[...])
    k_ref[...] = head_norm(proj(ATT_WIDTH, 2 * ATT_WIDTH), gk_ref[...])
    v_ref[...] = proj(2 * ATT_WIDTH, PR_OFF)
    pr_ref[...] = proj(PR_OFF, GATES_OFF)
    gate_ref[...] = proj(GATES_OFF, P_TOTAL).astype(gate_ref.dtype)


def _proj(x2d, ln1_g, w_in_bf, bd768, gq, gk, tm):
    m = x2d.shape[0]
    row = lambda w: pl.BlockSpec((tm, w), lambda i: (i, 0))
    widths = (ATT_WIDTH, ATT_WIDTH, ATT_WIDTH, RWKV_PROJ, 2 * D_MODEL)
    return pl.pallas_call(
        _proj_kernel,
        out_shape=tuple(jax.ShapeDtypeStruct((m, w), BF16 if i == len(widths) - 1 else F32)
                        for i, w in enumerate(widths)),
        grid=(m // tm,),
        in_specs=[row(D_MODEL), _resident((1, D_MODEL)), _resident((D_MODEL, P_TOTAL)),
                  _resident((SEG_TILE, SEG_TILE)), _resident((1, ATT_WIDTH)), _resident((1, ATT_WIDTH))],
        out_specs=tuple(row(w) for w in widths),
        compiler_params=_cparams("parallel"),
        name="proj",
    )(x2d, ln1_g, w_in_bf, bd768, gq, gk)


def _tail_kernel(k_ref, v_ref, kt_ref, vt_ref):
    kt_ref[...] = k_ref[...].T
    vt_ref[...] = v_ref[...].T


def _tail_channels_first(kn, v, batch, seq, keep, tm):
    assert keep % tm == 0 and seq % tm == 0
    first = (seq - keep) // tm
    rows = pl.BlockSpec((tm, ATT_WIDTH), lambda b, t: (b * (seq // tm) + first + t, 0))
    cols = pl.BlockSpec((None, ATT_WIDTH, tm), lambda b, t: (b, 0, t))
    shape = jax.ShapeDtypeStruct((batch, ATT_WIDTH, keep), F32)
    return pl.pallas_call(
        _tail_kernel, out_shape=(shape, shape), grid=(batch, keep // tm),
        in_specs=[rows, rows], out_specs=(cols, cols),
        compiler_params=_cparams("parallel", "parallel"), name="kv_tail",
    )(kn, v)


def _attn_prompt_kernel(*refs, dil):
    q_ref, kp_ref, kc_ref, vp_ref, vc_ref = (refs[2 * i:2 * i + 2] for i in range(5))
    o_ref, lse_ref = refs[10], refs[11]
    not_first_seg = pl.program_id(1) > 0
    nblk = SEG // (BLOCK * dil)
    stacked = (HEADS_PER_GROUP * BLOCK, 2 * BLOCK)
    row = lax.broadcasted_iota(jnp.int32, stacked, 0) % BLOCK
    col = lax.broadcasted_iota(jnp.int32, stacked, 1)
    in_prev = col < BLOCK
    band_prev = in_prev & (col >= row)
    band_cur = jnp.logical_not(in_prev) & (col - BLOCK <= row)
    lane = lax.broadcasted_iota(jnp.int32, (1, GROUP_WIDTH), 1)
    head_masks = [(lane // HEAD_DIM) == h for h in range(HEADS_PER_GROUP)]

    def row_slice(start):
        return pl.ds(start, BLOCK) if dil == 1 else pl.ds(start, BLOCK, stride=dil)

    def rows(halves, start):
        return jnp.concatenate([h[row_slice(start), :] for h in halves], axis=1)

    def block(q_start, k1_ref, k1_start, v1_ref, prev_valid):
        q = rows(q_ref, q_start) * (HEAD_DIM ** -0.5)
        q4 = jnp.concatenate([jnp.where(hm, q, 0.0) for hm in head_masks], axis=0).astype(BF16)
        kt = jnp.concatenate([rows(k1_ref, k1_start), rows(kc_ref, q_start)], axis=0).astype(BF16)
        vt = jnp.concatenate([rows(v1_ref, k1_start), rows(vc_ref, q_start)], axis=0).astype(BF16)
        s = lax.dot_general(q4, kt, (((1,), (1,)), ((), ())), preferred_element_type=F32)
        s = jnp.where(band_cur | (band_prev & prev_valid), s, NEG_BIG)
        m = jnp.max(s, axis=-1, keepdims=True)
        p = jnp.exp(s - m)
        l = jnp.sum(p, axis=-1, keepdims=True)
        ov = _dot(p.astype(BF16), vt) / l
        lse = m + jnp.log(l)
        o = jnp.zeros((BLOCK, GROUP_WIDTH), F32)
        lse_b = jnp.zeros((BLOCK, GROUP_WIDTH), F32)
        for h, hm in enumerate(head_masks):
            sl = slice(h * BLOCK, (h + 1) * BLOCK)
            o = jnp.where(hm, ov[sl], o)
            lse_b = jnp.where(hm, lse[sl], lse_b)
        for half in range(2):
            lanes = slice(half * LANES, (half + 1) * LANES)
            o_ref[half, row_slice(q_start), :] = o[:, lanes]
            lse_ref[half, row_slice(q_start), :] = lse_b[:, lanes]

    def first_blocks(r, carry):
        block(r, kp_ref, SEG - BLOCK * dil + r, vp_ref, not_first_seg)
        return carry

    def later_blocks(i, carry):
        r = i % dil
        nb = 1 + i // dil
        q_start = nb * BLOCK * dil + r
        block(q_start, kc_ref, q_start - BLOCK * dil, vc_ref, True)
        return carry

    lax.fori_loop(0, dil, first_blocks, 0, unroll=min(2, dil))
    if nblk > 1:
        lax.fori_loop(0, dil * (nblk - 1), later_blocks, 0, unroll=2)


def _attn_prompt(qn, kn, v, group, batch, seq):
    dil = ATT_GROUPS[group][1]
    nseg = seq // SEG
    halves = GROUP_WIDTH // LANES
    cur = [pl.BlockSpec((SEG, LANES), lambda b, s, c=halves * group + h: (b * nseg + s, c)) for h in range(halves)]
    prev = [pl.BlockSpec((SEG, LANES), lambda b, s, c=halves * group + h: (b * nseg + jnp.maximum(s - 1, 0), c))
            for h in range(halves)]
    out = pl.BlockSpec((halves, SEG, LANES), lambda b, s: (0, b * nseg + s, 0))
    shape = jax.ShapeDtypeStruct((halves, batch * seq, LANES), F32)
    return pl.pallas_call(
        functools.partial(_attn_prompt_kernel, dil=dil),
        out_shape=(shape, shape),
        grid=(batch, nseg),
        in_specs=cur + prev + cur + prev + cur,
        out_specs=(out, out),
        compiler_params=_cparams("parallel", "arbitrary"),
        name=f"attn_prompt_g{group}",
    )(qn, qn, kn, kn, kn, kn, v, v, v, v)


def _cache_attn_kernel(q_ref, kn_ref, vn_ref, knt_ref, vnt_ref, kc_ref, vc_ref, ko_ref, vo_ref, o_ref, lse_ref,
                       *, group, dil, bb):
    win = kc_ref.shape[2]
    nseq = q_ref.shape[0]
    cols = slice(group * GROUP_WIDTH, (group + 1) * GROUP_WIDTH)
    pos = lax.broadcasted_iota(jnp.int32, (1, win), 1)
    in_window = (pos % dil) == 0
    newest = lax.broadcasted_iota(jnp.int32, (GROUP_WIDTH, win), 1) == win - 1
    head_rows = (lax.broadcasted_iota(jnp.int32, (8, GROUP_WIDTH), 0)
                 == lax.broadcasted_iota(jnp.int32, (8, GROUP_WIDTH), 1) // HEAD_DIM)
    seq_lane = lax.broadcasted_iota(jnp.int32, (GROUP_WIDTH, nseq), 1)

    def hi_lo_rows(x):
        hi, lo = _split_bf16(x)
        return jnp.concatenate([hi, lo], axis=0)

    def fold(x):
        return x[0:8] + x[8:16]

    def shifted(cache, new_t_ref, n):
        new_col = jnp.sum(jnp.where(seq_lane == n, new_t_ref[cols, :], 0.0), axis=1, keepdims=True)
        return jnp.where(newest, new_col, pltpu.roll(cache, win - 1, 1))

    for b in range(bb):
        n = pl.program_id(0) * bb + b
        q = q_ref[pl.ds(n, 1), cols] * (HEAD_DIM ** -0.5)
        k_new = kn_ref[pl.ds(n, 1), cols]
        v_new = vn_ref[pl.ds(n, 1), cols]
        q8 = jnp.where(head_rows, q, 0.0)
        kc = kc_ref[b]
        vc = vc_ref[b]
        s = fold(_dot(hi_lo_rows(q8), kc.astype(BF16)))
        s = jnp.where(in_window, s, NEG_BIG)
        s_new = jnp.sum(q8 * k_new, axis=1, keepdims=True)
        m = jnp.maximum(jnp.max(s, axis=1, keepdims=True), s_new)
        p = jnp.exp(s - m)
        p_new = jnp.exp(s_new - m)
        l = jnp.sum(p, axis=1, keepdims=True) + p_new
        pv = fold(lax.dot_general(hi_lo_rows(p), vc.astype(BF16), (((1,), (1,)), ((), ())),
                                  preferred_element_type=F32))
        ov = (pv + p_new * v_new) / l
        o = jnp.sum(jnp.where(head_rows, ov, 0.0), axis=0, keepdims=True)
        lse = jnp.sum(jnp.where(head_rows, m + jnp.log(l), 0.0), axis=0, keepdims=True)
        for half in range(GROUP_WIDTH // LANES):
            lanes = slice(half * LANES, (half + 1) * LANES)
            o_ref[half, pl.ds(n, 1), :] = o[:, lanes]
            lse_ref[half, pl.ds(n, 1), :] = lse[:, lanes]
        ko_ref[b] = shifted(kc, knt_ref, n)
        vo_ref[b] = shifted(vc, vnt_ref, n)


def _cache_attn(qn, kn, v, kn_t, v_t, k_cache, v_cache, group, bb):
    n, _, win = k_cache.shape
    dil = ATT_GROUPS[group][1]
    halves = GROUP_WIDTH // LANES
    cache = pl.BlockSpec((bb, GROUP_WIDTH, win), lambda i: (i, 0, 0))
    rows = _resident((n, ATT_WIDTH))
    rows_t = _resident((ATT_WIDTH, n))
    out = pl.BlockSpec((halves, n, LANES), lambda i: (0, 0, 0))
    cache_shape = jax.ShapeDtypeStruct(k_cache.shape, F32)
    o_shape = jax.ShapeDtypeStruct((halves, n, LANES), F32)
    return pl.pallas_call(
        functools.partial(_cache_attn_kernel, group=group, dil=dil, bb=bb),
        out_shape=(cache_shape, cache_shape, o_shape, o_shape),
        grid=(n // bb,),
        in_specs=[rows, rows, rows, rows_t, rows_t, cache, cache],
        out_specs=(cache, cache, out, out),
        compiler_params=_cparams("arbitrary"),
        name=f"cache_attn_g{group}",
    )(qn, kn, v, kn_t, v_t, k_cache, v_cache)


def _prep_math(p, prev, mu_ref, w0_ref, a0_ref, w2h_ref, w2l_ref, a2h_ref, a2l_ref, g2h_ref, g2l_ref,
               kk_ref, ka_ref, rk_ref, bd_ref, outs):
    na_ref, dec_ref, nb_ref, kp_ref, rp_ref, v_ref, vkr_ref, bonus_ref, g_ref = outs
    bd = bd_ref[...]

    def put(ref, val):
        ref[...] = val.T

    xs = p + mu_ref[...] * (prev - p)
    r = xs[:, 0:RWKV_WIDTH]
    k = xs[:, RWKV_WIDTH:2 * RWKV_WIDTH]
    v = xs[:, 2 * RWKV_WIDTH:3 * RWKV_WIDTH]
    lora = xs[:, LORA_OFF:GATE_OFF]
    gl = xs[:, GATE_OFF:RWKV_PROJ]
    z = w0_ref[...] + _dot3(jnp.tanh(lora), w2h_ref[...], w2l_ref[...])
    softplus_neg = jnp.maximum(-z, 0.0) + jnp.log(1.0 + jnp.exp(-jnp.abs(z)))
    w = -softplus_neg - 0.5
    dec = jnp.exp(-jnp.exp(w))
    a = _sigmoid(a0_ref[...] + _dot3(lora, a2h_ref[...], a2l_ref[...]))
    g = _dot3(_sigmoid(gl), g2h_ref[...], g2l_ref[...])
    kk = k * kk_ref[...]
    kk = kk / jnp.maximum(jnp.sqrt(_seg_sum(kk * kk, bd)), L2_EPS)
    kp = k * (1.0 + (a - 1.0) * ka_ref[...])
    na = -kk
    nb = kk * a
    rp = dec * r + na * _seg_sum(nb * r, bd)
    put(na_ref, na)
    put(dec_ref, dec)
    put(nb_ref, nb)
    put(kp_ref, kp)
    put(rp_ref, rp)
    put(v_ref, v)
    vkr_ref[...] = v * _seg_sum(kp * r, bd)
    bonus_ref[...] = _seg_sum(r * kp * rk_ref[...], bd) * v
    g_ref[...] = g


def _prep_seq_kernel(p_ref, *rest):
    weights, outs, carry_ref = rest[:13], rest[13:22], rest[22]

    @pl.when(pl.program_id(1) == 0)
    def _():
        carry_ref[...] = jnp.zeros_like(carry_ref)

    p = p_ref[...]
    tm = p.shape[0]
    row = lax.broadcasted_iota(jnp.int32, p.shape, 0)
    prev = jnp.where(row == 0, carry_ref[7:8, :], pltpu.roll(p, 1, 0))
    carry_ref[7:8, :] = p[tm - 1:tm, :]
    _prep_math(p, prev, *weights, outs)


def _prep_step_kernel(p_ref, prev_ref, *rest):
    weights, outs = rest[:13], rest[13:22]
    _prep_math(p_ref[...], prev_ref[...], *weights, outs)


def _prep(pr, shift_prev, weights, batch, seq, tm):
    m = pr.shape[0]
    w_specs = [_resident(w.shape) for w in weights]
    out_shape = tuple(jax.ShapeDtypeStruct((m, RWKV_WIDTH), F32) for _ in range(9))
    if shift_prev is None:
        nt = seq // tm
        row = lambda w: pl.BlockSpec((tm, w), lambda b, t: (b * nt + t, 0))
        chan = pl.BlockSpec((None, RWKV_WIDTH, tm), lambda b, t: (b, 0, t))
        chan_shape = jax.ShapeDtypeStruct((batch, RWKV_WIDTH, seq), F32)
        return pl.pallas_call(
            _prep_seq_kernel, out_shape=(chan_shape,) * 6 + out_shape[6:], grid=(batch, nt),
            in_specs=[row(RWKV_PROJ)] + w_specs, out_specs=(chan,) * 6 + tuple(row(RWKV_WIDTH) for _ in range(3)),
            scratch_shapes=[pltpu.VMEM((8, RWKV_PROJ), F32)],
            compiler_params=_cparams("parallel", "arbitrary"), name="rwkv_prep_seq",
        )(pr, *weights)
    row = lambda w: pl.BlockSpec((tm, w), lambda i: (i, 0))
    chan = pl.BlockSpec((RWKV_WIDTH, tm), lambda i: (0, i))
    chan_shape = jax.ShapeDtypeStruct((RWKV_WIDTH, m), F32)
    return pl.pallas_call(
        _prep_step_kernel, out_shape=(chan_shape,) * 6 + out_shape[6:], grid=(m // tm,),
        in_specs=[row(RWKV_PROJ), row(RWKV_PROJ)] + w_specs,
        out_specs=(chan,) * 6 + tuple(row(RWKV_WIDTH) for _ in range(3)),
        compiler_params=_cparams("parallel"), name="rwkv_prep_step",
    )(pr, shift_prev, *weights)


def _wkv_seq_kernel(w_ref, nb_ref, k_ref, na_ref, rp_ref, v_ref, y_ref, sout_ref, s_ref, e_ref, vt_ref, yt_ref):
    @pl.when(pl.program_id(0) == 0)
    def _():
        s_ref[...] = jnp.zeros_like(s_ref)

    nbatch = w_ref.shape[0]
    tb = w_ref.shape[2]
    zero = jnp.zeros((WKV_SUB, LANES), F32)

    def head_rows(ref, b, c):
        return ref[b, pl.ds(c, RWKV_HEADS, stride=RWKV_HEAD), :]

    unroll = 4

    def build_e(jg, carry):
        for j in [jg * unroll + u for u in range(unroll)]:
            for o, ref in enumerate((w_ref, nb_ref, k_ref, na_ref, rp_ref)):
                rows = jnp.concatenate([head_rows(ref, b, j) for b in range(nbatch)], axis=0)
                e_ref[o, j] = jnp.concatenate([rows] * WKV_Q, axis=0).T
        return carry

    def build_v(ig, carry):
        for ih in [ig * unroll + u for u in range(unroll)]:
            rows = jnp.concatenate([head_rows(v_ref, b, il * WKV_SUB + ih)
                                    for il in range(WKV_Q) for b in range(nbatch)], axis=0)
            vt_ref[pl.ds(ih, tb, stride=WKV_SUB), :] = rows.T
        return carry

    lax.fori_loop(0, RWKV_HEAD // unroll, build_e, 0)
    lax.fori_loop(0, WKV_SUB // unroll, build_v, 0)

    def first_matvecs(j, carry):
        sa, y = carry
        sj = s_ref[j]
        return sa + sj * e_ref[3, j, 0:1, :], y + sj * e_ref[4, j, 0:1, :]

    def step(t, carry):
        sa, y = carry
        tile = pl.ds(pl.multiple_of(t * WKV_SUB, WKV_SUB), WKV_SUB)
        yt_ref[tile, :] = y
        v = vt_ref[tile, :]
        cur = pl.ds(t, 1)
        nxt = pl.ds(jnp.minimum(t + 1, tb - 1), 1)
        n_acc = 4
        sa_acc = [zero] * n_acc
        y_acc = [zero] * n_acc
        for j in range(RWKV_HEAD):
            sn = s_ref[j] * e_ref[0, j, cur, :] + sa * e_ref[1, j, cur, :] + v * e_ref[2, j, cur, :]
            s_ref[j] = sn
            sa_acc[j % n_acc] = sa_acc[j % n_acc] + sn * e_ref[3, j, nxt, :]
            y_acc[j % n_acc] = y_acc[j % n_acc] + sn * e_ref[4, j, nxt, :]
        return (sa_acc[0] + sa_acc[1]) + (sa_acc[2] + sa_acc[3]), (y_acc[0] + y_acc[1]) + (y_acc[2] + y_acc[3])

    lax.fori_loop(0, tb, step, lax.fori_loop(0, RWKV_HEAD, first_matvecs, (zero, zero)))

    def emit_y(ig, carry):
        for ih in [ig * unroll + u for u in range(unroll)]:
            cols = yt_ref[pl.ds(ih, tb, stride=WKV_SUB), :].T
            for il in range(WKV_Q):
                for b in range(nbatch):
                    r0 = il * (LANES // WKV_Q) + b * RWKV_HEADS
                    y_ref[b, pl.ds(il * WKV_SUB + ih, RWKV_HEADS, stride=RWKV_HEAD), :] = cols[r0:r0 + RWKV_HEADS]
        return carry

    lax.fori_loop(0, WKV_SUB // unroll, emit_y, 0)

    @pl.when(pl.program_id(0) == pl.num_programs(0) - 1)
    def _():
        sout_ref[...] = s_ref[...]


def _wkv_seq(w, nb, k, na, rp, v):
    batch, _, t = w.shape
    blk = pl.BlockSpec((batch, RWKV_WIDTH, WKV_TB), lambda i: (0, 0, i))
    state = pl.BlockSpec((RWKV_HEAD, WKV_SUB, LANES), lambda i: (0, 0, 0))
    return pl.pallas_call(
        _wkv_seq_kernel,
        out_shape=(jax.ShapeDtypeStruct((batch, RWKV_WIDTH, t), F32),
                   jax.ShapeDtypeStruct((RWKV_HEAD, WKV_SUB, LANES), F32)),
        grid=(t // WKV_TB,),
        in_specs=[blk] * 6,
        out_specs=(blk, state),
        scratch_shapes=[pltpu.VMEM((RWKV_HEAD, WKV_SUB, LANES), F32),
                        pltpu.VMEM((5, RWKV_HEAD, WKV_TB, LANES), F32),
                        pltpu.VMEM((WKV_TB * WKV_SUB, LANES), F32),
                        pltpu.VMEM((WKV_TB * WKV_SUB, LANES), F32)],
        compiler_params=_cparams("arbitrary"),
        name="wkv_seq",
    )(w, nb, k, na, rp, v)


def _wkv_step_kernel(s_ref, w_ref, nb_ref, k_ref, na_ref, rp_ref, v_ref, y_ref, sout_ref):
    w, nb, k, na, rp = w_ref[...], nb_ref[...], k_ref[...], na_ref[...], rp_ref[...]
    for i in range(RWKV_HEAD):
        si = s_ref[i]
        sa = jnp.sum(si * na, axis=0, keepdims=True)
        y_ref[i:i + 1, :] = jnp.sum(si * rp, axis=0, keepdims=True)
        sout_ref[i] = si * w + sa * nb + v_ref[i:i + 1, :] * k


def _wkv_step(state, w, nb, k, na, rp, v):
    n = state.shape[-1]
    st = pl.BlockSpec((None, RWKV_HEAD, RWKV_HEAD, n), lambda h: (h, 0, 0, 0))
    vec = pl.BlockSpec((RWKV_HEAD, n), lambda h: (h, 0))
    return pl.pallas_call(
        _wkv_step_kernel,
        out_shape=(jax.ShapeDtypeStruct((RWKV_WIDTH, n), F32), jax.ShapeDtypeStruct(state.shape, F32)),
        grid=(RWKV_HEADS,),
        in_specs=[st] + [vec] * 6,
        out_specs=(vec, st),
        compiler_params=_cparams("parallel"),
        name="wkv_step",
    )(state, w, nb, k, na, rp, v)


def _merge_kernel(x_ref, o0_ref, o1_ref, o2_ref, l0_ref, l1_ref, l2_ref, yw_ref, vkr_ref, bonus_ref, g_ref,
                  gate_ref, gng_ref, gnb_ref, bd_ref, wa_ref, wr_ref, wo_ref, out_ref, *, yw_channels_first):
    wide = lambda ref: jnp.concatenate([ref[0], ref[1]], axis=1)
    l0, l1, l2 = wide(l0_ref), wide(l1_ref), wide(l2_ref)
    lm = jnp.maximum(jnp.maximum(l0, l1), l2)
    e0, e1, e2 = jnp.exp(l0 - lm), jnp.exp(l1 - lm), jnp.exp(l2 - lm)
    o_att = (e0 * wide(o0_ref) + e1 * wide(o1_ref) + e2 * wide(o2_ref)) / (e0 + e1 + e2)
    y_att = _dot(o_att.astype(BF16), wa_ref[...])

    bd = bd_ref[...]
    y = (yw_ref[...].T if yw_channels_first else yw_ref[...]) + vkr_ref[...]
    mean = _seg_sum(y, bd) * (1.0 / RWKV_HEAD)
    yc = y - mean
    var = _seg_sum(yc * yc, bd) * (1.0 / RWKV_HEAD)
    yn = yc * lax.rsqrt(var + GN_EPS) * gng_ref[...] + gnb_ref[...]
    o_rwkv = (yn + bonus_ref[...]) * g_ref[...]
    y_rwkv = _dot(o_rwkv.astype(BF16), wr_ref[...])

    gate_att = gate_ref[:, 0:D_MODEL].astype(F32)
    gate_rwkv = gate_ref[:, D_MODEL:2 * D_MODEL].astype(F32)
    merged = _sigmoid(gate_att) * y_att + _sigmoid(gate_rwkv) * y_rwkv
    out_ref[...] = x_ref[...] + _dot(merged.astype(BF16), wo_ref[...])


def _merge(x2d, o_groups, lse_groups, yw, vkr, bonus, g, gates, gn_g, gn_b, bd512, wa, wr, wo, tm):
    m = x2d.shape[0]
    row = lambda w: pl.BlockSpec((tm, w), lambda i: (i, 0))
    channels_first = yw.ndim == 3
    if channels_first:
        nt = yw.shape[2] // tm
        yw_spec = pl.BlockSpec((None, RWKV_WIDTH, tm), lambda i: (i // nt, 0, i % nt))
    else:
        yw_spec = row(RWKV_WIDTH)
    return pl.pallas_call(
        functools.partial(_merge_kernel, yw_channels_first=channels_first),
        out_shape=jax.ShapeDtypeStruct((m, D_MODEL), F32),
        grid=(m // tm,),
        in_specs=[row(D_MODEL)] + [pl.BlockSpec((GROUP_WIDTH // LANES, tm, LANES), lambda i: (0, i, 0))] * 6
                 + [yw_spec] + [row(RWKV_WIDTH)] * 3 + [row(2 * D_MODEL)]
                 + [_resident((1, RWKV_WIDTH)), _resident((1, RWKV_WIDTH)), _resident((SEG_TILE, SEG_TILE)),
                    _resident((GROUP_WIDTH, D_MODEL)), _resident((RWKV_WIDTH, D_MODEL)),
                    _resident((D_MODEL, D_MODEL))],
        out_specs=row(D_MODEL),
        compiler_params=_cparams("parallel"),
        name="merge",
    )(x2d, *o_groups, *lse_groups, yw, vkr, bonus, g, gates, gn_g, gn_b, bd512, wa, wr, wo)


def _ffn_body(x_ref, g2_ref, wup_ref, cw_ref, cb_ref, wdn_ref, out_ref, u_prev, u_sink):
    x = x_ref[...]
    ms = jnp.mean(x * x, axis=-1, keepdims=True)
    xn = (x * lax.rsqrt(ms + RMS_EPS) * g2_ref[...]).astype(BF16)

    def up(c):
        return tuple(_dot(xn, wup_ref[:, c0:c0 + FF_CHUNK]) for c0 in (c, D_FF + c))

    def conv(u, c0):
        c1 = c0 + FF_CHUNK
        u_m2, u_m1 = u_prev(u, c0, c1)
        u_sink(u, c0)
        return cb_ref[:, c0:c1] + u_m2 * cw_ref[0:1, c0:c1] + u_m1 * cw_ref[1:2, c0:c1] + u * cw_ref[2:3, c0:c1]

    acc = x
    nxt = up(0)
    for c in range(0, D_FF, FF_CHUNK):
        ug, uv = nxt
        if c + FF_CHUNK < D_FF:
            nxt = up(c + FF_CHUNK)
        gate = conv(ug, c)
        val = conv(uv, D_FF + c)
        h = gate * _sigmoid(gate) * val
        acc = acc + _dot(h.astype(BF16), wdn_ref[c:c + FF_CHUNK, :])
    out_ref[...] = acc


def _ffn_seq_kernel(x_ref, g2_ref, wup_ref, cw_ref, cb_ref, wdn_ref, out_ref, ust_ref, carry_ref):
    @pl.when(pl.program_id(1) == 0)
    def _():
        carry_ref[...] = jnp.zeros_like(carry_ref)

    tm = x_ref.shape[0]
    row = lax.broadcasted_iota(jnp.int32, (tm, FF_CHUNK), 0)

    def u_prev(u, c0, c1):
        c6 = carry_ref[6:7, c0:c1]
        c7 = carry_ref[7:8, c0:c1]
        u_m1 = jnp.where(row == 0, c7, pltpu.roll(u, 1, 0))
        u_m2 = jnp.where(row == 0, c6, jnp.where(row == 1, c7, pltpu.roll(u, 2, 0)))
        return u_m2, u_m1

    def u_sink(u, c0):
        carry_ref[:, c0:c0 + FF_CHUNK] = u[tm - 8:tm, :]

    _ffn_body(x_ref, g2_ref, wup_ref, cw_ref, cb_ref, wdn_ref, out_ref, u_prev, u_sink)
    ust_ref[...] = carry_ref[...]


def _ffn_step_kernel(x_ref, p0_ref, p1_ref, g2_ref, wup_ref, cw_ref, cb_ref, wdn_ref, out_ref, u_ref):
    def u_prev(u, c0, c1):
        return p0_ref[:, c0:c1], p1_ref[:, c0:c1]

    def u_sink(u, c0):
        u_ref[:, c0:c0 + FF_CHUNK] = u

    _ffn_body(x_ref, g2_ref, wup_ref, cw_ref, cb_ref, wdn_ref, out_ref, u_prev, u_sink)


def _ffn(x2d, conv_prev, ln2_g, wup, cw, cb, wdn, batch, seq, tm):
    m = x2d.shape[0]
    w_specs = [_resident((1, D_MODEL)), _resident((D_MODEL, 2 * D_FF)), _resident((3, 2 * D_FF)),
               _resident((1, 2 * D_FF)), _resident((D_FF, D_MODEL))]
    if conv_prev is None:
        nt = seq // tm
        row = pl.BlockSpec((tm, D_MODEL), lambda b, t: (b * nt + t, 0))
        return pl.pallas_call(
            _ffn_seq_kernel,
            out_shape=(jax.ShapeDtypeStruct((m, D_MODEL), F32), jax.ShapeDtypeStruct((batch, 8, 2 * D_FF), F32)),
            grid=(batch, nt),
            in_specs=[row] + w_specs,
            out_specs=(row, pl.BlockSpec((None, 8, 2 * D_FF), lambda b, t: (b, 0, 0))),
            scratch_shapes=[pltpu.VMEM((8, 2 * D_FF), F32)],
            compiler_params=_cparams("parallel", "arbitrary"), name="ffn_seq",
        )(x2d, ln2_g, wup, cw, cb, wdn)
    row = lambda w: pl.BlockSpec((tm, w), lambda i: (i, 0))
    return pl.pallas_call(
        _ffn_step_kernel,
        out_shape=(jax.ShapeDtypeStruct((m, D_MODEL), F32), jax.ShapeDtypeStruct((m, 2 * D_FF), F32)),
        grid=(m // tm,),
        in_specs=[row(D_MODEL), row(2 * D_FF), row(2 * D_FF)] + w_specs,
        out_specs=(row(D_MODEL), row(2 * D_FF)),
        compiler_params=_cparams("parallel"), name="ffn_step",
    )(x2d, conv_prev[0], conv_prev[1], ln2_g, wup, cw, cb, wdn)


def _pad_rows(w, before, total):
    return jnp.zeros((total, w.shape[1]), w.dtype).at[before:before + w.shape[0]].set(w)


def kernel(x_prompt, x_sample, cache_k_w128, cache_v_w128, cache_k_w512, cache_v_w512, cache_k_w2048, cache_v_w2048,
           state_rwkv_shift, state_rwkv_wkv, state_ffn_conv, ln1_g, w_in, q_norm_g, k_norm_g, w_attn_out,
           rwkv_mu, rwkv_w0, rwkv_w2, rwkv_a0, rwkv_a2, rwkv_g2, rwkv_k_k, rwkv_k_a, rwkv_r_k, rwkv_gn_g, rwkv_gn_b,
           w_rwkv_out, w_o, ln2_g, w_up, conv_w, conv_b, w_down):
    batch, seq, _ = x_prompt.shape
    nsamp = x_sample.shape[0]
    assert seq % SEG == 0 and x_sample.shape[1] == 1 and batch * RWKV_HEADS * WKV_Q == LANES
    assert ln1_g.shape[0] == 1, "single layer"
    lyr = 0

    w_in_bf = w_in[lyr].astype(BF16)
    bd768 = bd512 = _block_diag_ones(SEG_TILE)
    gq = jnp.tile(q_norm_g[lyr], ATT_WIDTH // HEAD_DIM)[None]
    gk = jnp.tile(k_norm_g[lyr], ATT_WIDTH // HEAD_DIM)[None]
    row1 = lambda a: a.reshape(1, -1)
    prep_w = (row1(rwkv_mu[lyr]), row1(rwkv_w0[lyr]), row1(rwkv_a0[lyr]),
              *_split_bf16(_pad_rows(rwkv_w2[lyr], 0, DECAY_LORA + ICLR_LORA)),
              *_split_bf16(_pad_rows(rwkv_a2[lyr], DECAY_LORA, DECAY_LORA + ICLR_LORA)),
              *_split_bf16(rwkv_g2[lyr]),
              row1(rwkv_k_k[lyr]), row1(rwkv_k_a[lyr]), row1(rwkv_r_k[lyr]), bd512)
    merge_w = (row1(rwkv_gn_g[lyr]), row1(rwkv_gn_b[lyr]), bd512, w_attn_out[lyr].astype(BF16),
               w_rwkv_out[lyr].astype(BF16), w_o[lyr].astype(BF16))
    ffn_w = (row1(ln2_g[lyr]), w_up[lyr].astype(BF16), conv_w[lyr], row1(conv_b[lyr]), w_down[lyr].astype(BF16))

    xp = x_prompt.reshape(batch * seq, D_MODEL)
    qn, kn, v, pr, gates = _proj(xp, row1(ln1_g[lyr]), w_in_bf, bd768, gq, gk, tm=256)
    att = [_attn_prompt(qn, kn, v, g, batch, seq) for g in range(N_GROUPS)]
    na, dec, nb, kp, rp, rv, vkr, bonus, gg = _prep(pr, None, prep_w, batch, seq, tm=256)
    yw, s_fin = _wkv_seq(dec, nb, kp, na, rp, rv)
    x2 = _merge(xp, [a[0] for a in att], [a[1] for a in att], yw, vkr, bonus, gg, gates, *merge_w, tm=256)
    yp, ust = _ffn(x2, None, *ffn_w, batch, seq, tm=256)

    y_prompt = yp.reshape(batch, seq, D_MODEL)
    keep_max = min(max(w for w, _ in ATT_GROUPS), seq)
    k_tail, v_tail = _tail_channels_first(kn, v, batch, seq, keep_max, tm=256)
    kv_prompt = []
    for g, (win, _) in enumerate(ATT_GROUPS):
        keep = min(win, seq)
        for tail in (k_tail, v_tail):
            c = tail[:, g * GROUP_WIDTH:(g + 1) * GROUP_WIDTH, keep_max - keep:]
            kv_prompt.append(c.reshape(batch, HEADS_PER_GROUP, HEAD_DIM, keep).transpose(0, 3, 1, 2)[None])
    shift_prompt = pr.reshape(batch, seq, RWKV_PROJ)[None, :, -1]
    wkv_prompt = (s_fin.reshape(RWKV_HEAD, WKV_SUB, WKV_Q, batch, RWKV_HEADS)
                  .transpose(3, 4, 2, 1, 0).reshape(1, batch, RWKV_HEADS, RWKV_HEAD, RWKV_HEAD))
    conv_prompt = ust[None, :, 6:8]

    xs = x_sample.reshape(nsamp, D_MODEL)
    qs, ks, vs, prs, gates_s = _proj(xs, row1(ln1_g[lyr]), w_in_bf, bd768, gq, gk, tm=nsamp)
    caches = [c[lyr].transpose(0, 2, 3, 1).reshape(nsamp, GROUP_WIDTH, c.shape[2])
              for c in (cache_k_w128, cache_v_w128, cache_k_w512, cache_v_w512, cache_k_w2048, cache_v_w2048)]
    ks_t, vs_t = ks.T, vs.T
    kv_sample, o_s, lse_s = [], [], []
    for g, (win, dil) in enumerate(ATT_GROUPS):
        assert caches[2 * g].shape[2] == BLOCK * dil, "window caches must be full"
        k_out, v_out, o_g, lse_g = _cache_attn(qs, ks, vs, ks_t, vs_t, caches[2 * g], caches[2 * g + 1], g,
                                               bb=8 if win <= 512 else 2)
        kv_sample += [c.reshape(nsamp, HEADS_PER_GROUP, HEAD_DIM, win).transpose(0, 3, 1, 2)[None]
                      for c in (k_out, v_out)]
        o_s.append(o_g)
        lse_s.append(lse_g)
    na, dec, nb, kp, rp, rv, vkr, bonus, gg = _prep(prs, state_rwkv_shift[lyr], prep_w, nsamp, 1, tm=nsamp)
    state_t = state_rwkv_wkv[lyr].transpose(1, 2, 3, 0)
    yw_s, state_new = _wkv_step(state_t, dec, nb, kp, na, rp, rv)
    x2s = _merge(xs, o_s, lse_s, yw_s[None], vkr, bonus, gg, gates_s, *merge_w, tm=nsamp)
    conv_prev = state_ffn_conv[lyr]
    ys, u_s = _ffn(x2s, (conv_prev[:, 0], conv_prev[:, 1]), *ffn_w, nsamp, 1, tm=nsamp)

    y_sample = ys.reshape(nsamp, 1, D_MODEL)
    shift_sample = prs[None]
    wkv_sample = state_new.transpose(3, 0, 1, 2)[None]
    conv_sample = jnp.stack([conv_prev[:, 1], u_s], axis=1)[None]

    return (y_prompt, y_sample,
            kv_prompt[0], kv_sample[0], kv_prompt[1], kv_sample[1],
            kv_prompt[2], kv_sample[2], kv_prompt[3], kv_sample[3],
            kv_prompt[4], kv_sample[4], kv_prompt[5], kv_sample[5],
            shift_prompt, shift_sample, wkv_prompt, wkv_sample, conv_prompt, conv_sample)
```

```python
import functools

import jax
import jax.numpy as jnp
from jax import lax
from jax.experimental import pallas as pl
from jax.experimental.pallas import tpu as pltpu

F32 = jnp.float32
BF16 = jnp.bfloat16

D_MODEL = 1024
HEAD_DIM = 64
ATT_GROUPS = ((128, 1), (512, 4), (2048, 16))
N_GROUPS = len(ATT_GROUPS)
HEADS_PER_GROUP = 4
GROUP_WIDTH = HEADS_PER_GROUP * HEAD_DIM
ATT_WIDTH = N_GROUPS * GROUP_WIDTH
BLOCK = 128
SEG = 2048
RWKV_HEADS = 8
RWKV_HEAD = 64
RWKV_WIDTH = RWKV_HEADS * RWKV_HEAD
DECAY_LORA = 64
ICLR_LORA = 64
GATE_LORA = 128
RWKV_PROJ = 3 * RWKV_WIDTH + DECAY_LORA + ICLR_LORA + GATE_LORA
LORA_OFF = 3 * RWKV_WIDTH
GATE_OFF = LORA_OFF + DECAY_LORA + ICLR_LORA
PR_OFF = 3 * ATT_WIDTH
GATES_OFF = PR_OFF + RWKV_PROJ
P_TOTAL = GATES_OFF + 2 * D_MODEL
D_FF = 2816
FF_CHUNK = 256
RMS_EPS = 1e-6
GN_EPS = 64e-5
L2_EPS = 1e-12
NEG_BIG = -1e30

LANES = 128
SEG_TILE = 256
WKV_Q = 4
WKV_SUB = RWKV_HEAD // WKV_Q
WKV_TB = 128
WKV_GROUP = 8
VMEM_LIMIT = 56 * 1024 * 1024


def _cparams(*sem):
    return pltpu.CompilerParams(dimension_semantics=sem, vmem_limit_bytes=VMEM_LIMIT)


def _resident(shape):
    n = len(shape)
    return pl.BlockSpec(shape, lambda *_: (0,) * n, pipeline_mode=pl.Buffered(1))


def _split_bf16(x):
    hi = x.astype(BF16)
    lo = (x - hi.astype(F32)).astype(BF16)
    return hi, lo


def _dot(a, b):
    return jnp.dot(a, b, preferred_element_type=F32)


def _dot3(a, b_hi, b_lo):
    a_hi, a_lo = _split_bf16(a)
    return _dot(a_hi, b_hi) + _dot(a_lo, b_hi) + _dot(a_hi, b_lo)


def _seg_sum_bf16(x_bf, ones_bd):
    return jnp.concatenate([_dot(x_bf[:, c:c + SEG_TILE], ones_bd) for c in range(0, x_bf.shape[1], SEG_TILE)],
                           axis=1)


def _seg_sum(x, ones_bd):
    hi, lo = _split_bf16(x)
    return _seg_sum_bf16(hi, ones_bd) + _seg_sum_bf16(lo, ones_bd)


def _sigmoid(x):
    return 1.0 / (1.0 + jnp.exp(-x))


def _block_diag_ones(width):
    idx = jnp.arange(width) // HEAD_DIM
    return (idx[:, None] == idx[None, :]).astype(BF16)


def _proj_kernel(*refs, cache_group, cache_bb):
    x_ref, g1_ref, w_ref, bd_ref, gq_ref, gk_ref = refs[:6]
    n_in = 6 + (N_CACHE_IN if cache_group is not None else 0)
    q_ref, k_ref, v_ref, pr_ref, gate_ref = refs[n_in:n_in + 5]
    if cache_group is not None:
        _cache_attn_body(refs[6:n_in] + refs[n_in + 5:], pl.program_id(0) * cache_bb, cache_group, cache_bb)
    x = x_ref[...]
    ms = jnp.mean(x * x, axis=-1, keepdims=True)
    xn = (x * lax.rsqrt(ms + RMS_EPS) * g1_ref[...]).astype(BF16)

    def proj(c0, c1):
        return _dot(xn, w_ref[:, c0:c1])

    def head_norm(z, g):
        ss = _seg_sum_bf16((z * z).astype(BF16), bd_ref[...])
        return z * lax.rsqrt(ss * (1.0 / HEAD_DIM) + RMS_EPS) * g

    q_ref[...] = head_norm(proj(0, ATT_WIDTH), gq_ref[...])
    k_ref[...] = head_norm(proj(ATT_WIDTH, 2 * ATT_WIDTH), gk_ref[...])
    v_ref[...] = proj(2 * ATT_WIDTH, PR_OFF)
    pr_ref[...] = proj(PR_OFF, GATES_OFF)
    gate_ref[...] = proj(GATES_OFF, P_TOTAL).astype(gate_ref.dtype)


def _proj(x2d, ln1_g, w_in_bf, bd768, gq, gk, tm, cache_rider=None):
    m = x2d.shape[0]
    row = lambda w: pl.BlockSpec((tm, w), lambda i: (i, 0))
    widths = (ATT_WIDTH, ATT_WIDTH, ATT_WIDTH, RWKV_PROJ, 2 * D_MODEL)
    in_specs = [row(D_MODEL), _resident((1, D_MODEL)), _resident((D_MODEL, P_TOTAL)),
                _resident((SEG_TILE, SEG_TILE)), _resident((1, ATT_WIDTH)), _resident((1, ATT_WIDTH))]
    out_specs = [row(w) for w in widths]
    out_shape = [jax.ShapeDtypeStruct((m, w), BF16 if i == len(widths) - 1 else F32) for i, w in enumerate(widths)]
    cache_group, cache_args, cache_bb = None, (), 0
    if cache_rider is not None:
        cache_group, cache_args = cache_rider
        cache_bb = cache_args[5].shape[0] * tm // m
        assert cache_bb * (m // tm) == cache_args[5].shape[0]
        c_in, c_out, c_shape = _cache_attn_specs(cache_args, cache_bb, lambda i: i)
        in_specs, out_specs, out_shape = in_specs + c_in, out_specs + c_out, out_shape + c_shape
    return pl.pallas_call(
        functools.partial(_proj_kernel, cache_group=cache_group, cache_bb=cache_bb),
        out_shape=tuple(out_shape),
        grid=(m // tm,),
        in_specs=in_specs,
        out_specs=tuple(out_specs),
        compiler_params=_cparams("arbitrary"),
        name="proj",
    )(x2d, ln1_g, w_in_bf, bd768, gq, gk, *cache_args)


def _tail_kernel(k_ref, v_ref, kt_ref, vt_ref):
    kt_ref[...] = k_ref[...].T
    vt_ref[...] = v_ref[...].T


def _tail_channels_first(kn, v, batch, seq, keep, tm):
    assert keep % tm == 0 and seq % tm == 0
    first = (seq - keep) // tm
    rows = pl.BlockSpec((tm, ATT_WIDTH), lambda b, t: (b * (seq // tm) + first + t, 0))
    cols = pl.BlockSpec((None, ATT_WIDTH, tm), lambda b, t: (b, 0, t))
    shape = jax.ShapeDtypeStruct((batch, ATT_WIDTH, keep), F32)
    return pl.pallas_call(
        _tail_kernel, out_shape=(shape, shape), grid=(batch, keep // tm),
        in_specs=[rows, rows], out_specs=(cols, cols),
        compiler_params=_cparams("parallel", "parallel"), name="kv_tail",
    )(kn, v)


def _attn_prompt_kernel(*refs, dil):
    q_ref, kp_ref, kc_ref, vp_ref, vc_ref = (refs[2 * i:2 * i + 2] for i in range(5))
    o_ref, lse_ref = refs[10], refs[11]
    not_first_seg = pl.program_id(1) > 0
    nblk = SEG // (BLOCK * dil)
    stacked = (HEADS_PER_GROUP * BLOCK, 2 * BLOCK)
    row = lax.broadcasted_iota(jnp.int32, stacked, 0) % BLOCK
    col = lax.broadcasted_iota(jnp.int32, stacked, 1)
    in_prev = col < BLOCK
    band_prev = in_prev & (col >= row)
    band_cur = jnp.logical_not(in_prev) & (col - BLOCK <= row)
    lane = lax.broadcasted_iota(jnp.int32, (1, GROUP_WIDTH), 1)
    head_masks = [(lane // HEAD_DIM) == h for h in range(HEADS_PER_GROUP)]

    def row_slice(start):
        return pl.ds(start, BLOCK) if dil == 1 else pl.ds(start, BLOCK, stride=dil)

    def rows(halves, start):
        return jnp.concatenate([h[row_slice(start), :] for h in halves], axis=1)

    def block(q_start, k1_ref, k1_start, v1_ref, prev_valid):
        q = rows(q_ref, q_start) * (HEAD_DIM ** -0.5)
        q4 = jnp.concatenate([jnp.where(hm, q, 0.0) for hm in head_masks], axis=0).astype(BF16)
        kt = jnp.concatenate([rows(k1_ref, k1_start), rows(kc_ref, q_start)], axis=0).astype(BF16)
        vt = jnp.concatenate([rows(v1_ref, k1_start), rows(vc_ref, q_start)], axis=0).astype(BF16)
        s = lax.dot_general(q4, kt, (((1,), (1,)), ((), ())), preferred_element_type=F32)
        s = jnp.where(band_cur | (band_prev & prev_valid), s, NEG_BIG)
        m = jnp.max(s, axis=-1, keepdims=True)
        p = jnp.exp(s - m)
        l = jnp.sum(p, axis=-1, keepdims=True)
        ov = _dot(p.astype(BF16), vt) / l
        lse = m + jnp.log(l)
        o = jnp.zeros((BLOCK, GROUP_WIDTH), F32)
        lse_b = jnp.zeros((BLOCK, GROUP_WIDTH), F32)
        for h, hm in enumerate(head_masks):
            sl = slice(h * BLOCK, (h + 1) * BLOCK)
            o = jnp.where(hm, ov[sl], o)
            lse_b = jnp.where(hm, lse[sl], lse_b)
        for half in range(2):
            lanes = slice(half * LANES, (half + 1) * LANES)
            o_ref[half, row_slice(q_start), :] = o[:, lanes]
            lse_ref[half, row_slice(q_start), :] = lse_b[:, lanes]

    def first_blocks(r, carry):
        block(r, kp_ref, SEG - BLOCK * dil + r, vp_ref, not_first_seg)
        return carry

    def later_blocks(i, carry):
        r = i % dil
        nb = 1 + i // dil
        q_start = nb * BLOCK * dil + r
        block(q_start, kc_ref, q_start - BLOCK * dil, vc_ref, True)
        return carry

    lax.fori_loop(0, dil, first_blocks, 0, unroll=min(2, dil))
    if nblk > 1:
        lax.fori_loop(0, dil * (nblk - 1), later_blocks, 0, unroll=2)


def _attn_prompt(qn, kn, v, group, batch, seq):
    dil = ATT_GROUPS[group][1]
    nseg = seq // SEG
    halves = GROUP_WIDTH // LANES
    cur = [pl.BlockSpec((SEG, LANES), lambda b, s, c=halves * group + h: (b * nseg + s, c)) for h in range(halves)]
    prev = [pl.BlockSpec((SEG, LANES), lambda b, s, c=halves * group + h: (b * nseg + jnp.maximum(s - 1, 0), c))
            for h in range(halves)]
    out = pl.BlockSpec((halves, SEG, LANES), lambda b, s: (0, b * nseg + s, 0))
    shape = jax.ShapeDtypeStruct((halves, batch * seq, LANES), F32)
    return pl.pallas_call(
        functools.partial(_attn_prompt_kernel, dil=dil),
        out_shape=(shape, shape),
        grid=(batch, nseg),
        in_specs=cur + prev + cur + prev + cur,
        out_specs=(out, out),
        compiler_params=_cparams("parallel", "arbitrary"),
        name=f"attn_prompt_g{group}",
    )(qn, qn, kn, kn, kn, kn, v, v, v, v)


def _cache_attn_body(refs, first, group, bb):
    q_ref, kn_ref, vn_ref, knt_ref, vnt_ref, kc_ref, vc_ref, ko_ref, vo_ref, o_ref, lse_ref = refs
    dil = ATT_GROUPS[group][1]
    win = kc_ref.shape[2]
    nseq = q_ref.shape[0]
    cols = slice(group * GROUP_WIDTH, (group + 1) * GROUP_WIDTH)
    pos = lax.broadcasted_iota(jnp.int32, (1, win), 1)
    in_window = (pos % dil) == 0
    newest = lax.broadcasted_iota(jnp.int32, (GROUP_WIDTH, win), 1) == win - 1
    head_rows = (lax.broadcasted_iota(jnp.int32, (8, GROUP_WIDTH), 0)
                 == lax.broadcasted_iota(jnp.int32, (8, GROUP_WIDTH), 1) // HEAD_DIM)
    seq_lane = lax.broadcasted_iota(jnp.int32, (GROUP_WIDTH, nseq), 1)

    def hi_lo_rows(x):
        hi, lo = _split_bf16(x)
        return jnp.concatenate([hi, lo], axis=0)

    def fold(x):
        return x[0:8] + x[8:16]

    def shifted(cache, new_t_ref, n):
        new_col = jnp.sum(jnp.where(seq_lane == n, new_t_ref[cols, :], 0.0), axis=1, keepdims=True)
        return jnp.where(newest, new_col, pltpu.roll(cache, win - 1, 1))

    for b in range(bb):
        n = first + b
        q = q_ref[pl.ds(n, 1), cols] * (HEAD_DIM ** -0.5)
        k_new = kn_ref[pl.ds(n, 1), cols]
        v_new = vn_ref[pl.ds(n, 1), cols]
        q8 = jnp.where(head_rows, q, 0.0)
        kc = kc_ref[b]
        vc = vc_ref[b]
        s = fold(_dot(hi_lo_rows(q8), kc.astype(BF16)))
        s = jnp.where(in_window, s, NEG_BIG)
        s_new = jnp.sum(q8 * k_new, axis=1, keepdims=True)
        m = jnp.maximum(jnp.max(s, axis=1, keepdims=True), s_new)
        p = jnp.exp(s - m)
        p_new = jnp.exp(s_new - m)
        l = jnp.sum(p, axis=1, keepdims=True) + p_new
        pv = fold(lax.dot_general(hi_lo_rows(p), vc.astype(BF16), (((1,), (1,)), ((), ())),
                                  preferred_element_type=F32))
        ov = (pv + p_new * v_new) / l
        o = jnp.sum(jnp.where(head_rows, ov, 0.0), axis=0, keepdims=True)
        lse = jnp.sum(jnp.where(head_rows, m + jnp.log(l), 0.0), axis=0, keepdims=True)
        for half in range(GROUP_WIDTH // LANES):
            lanes = slice(half * LANES, (half + 1) * LANES)
            o_ref[half, pl.ds(n, 1), :] = o[:, lanes]
            lse_ref[half, pl.ds(n, 1), :] = lse[:, lanes]
        ko_ref[b] = shifted(kc, knt_ref, n)
        vo_ref[b] = shifted(vc, vnt_ref, n)


N_CACHE_IN, N_CACHE_OUT = 7, 4


def _cache_attn_specs(args, bb, block_of):
    n, _, win = args[5].shape
    halves = GROUP_WIDTH // LANES
    cache = pl.BlockSpec((bb, GROUP_WIDTH, win), lambda *g: (block_of(*g), 0, 0))
    rows = _resident((n, ATT_WIDTH))
    rows_t = _resident((ATT_WIDTH, n))
    out = pl.BlockSpec((halves, n, LANES), lambda *g: (0, 0, 0))
    cache_shape = jax.ShapeDtypeStruct(args[5].shape, F32)
    o_shape = jax.ShapeDtypeStruct((halves, n, LANES), F32)
    return [rows] * 3 + [rows_t] * 2 + [cache] * 2, [cache, cache, out, out], [cache_shape] * 2 + [o_shape] * 2


def _cache_attn_kernel(*refs, group, bb):
    _cache_attn_body(refs, pl.program_id(0) * bb, group, bb)


def _cache_attn(args, group, bb):
    in_specs, out_specs, out_shape = _cache_attn_specs(args, bb, lambda i: i)
    return pl.pallas_call(
        functools.partial(_cache_attn_kernel, group=group, bb=bb),
        out_shape=tuple(out_shape),
        grid=(args[5].shape[0] // bb,),
        in_specs=in_specs,
        out_specs=tuple(out_specs),
        compiler_params=_cparams("arbitrary"),
        name=f"cache_attn_g{group}",
    )(*args)


def _prep_math(p, prev, mu_ref, w0_ref, a0_ref, w2h_ref, w2l_ref, a2h_ref, a2l_ref, g2h_ref, g2l_ref,
               kk_ref, ka_ref, rk_ref, bd_ref, outs):
    na_ref, dec_ref, nb_ref, kp_ref, rp_ref, v_ref, vkr_ref, bonus_ref, g_ref = outs
    bd = bd_ref[...]

    def put(ref, val):
        ref[...] = val.T

    xs = p + mu_ref[...] * (prev - p)
    r = xs[:, 0:RWKV_WIDTH]
    k = xs[:, RWKV_WIDTH:2 * RWKV_WIDTH]
    v = xs[:, 2 * RWKV_WIDTH:3 * RWKV_WIDTH]
    lora = xs[:, LORA_OFF:GATE_OFF]
    gl = xs[:, GATE_OFF:RWKV_PROJ]
    z = w0_ref[...] + _dot3(jnp.tanh(lora), w2h_ref[...], w2l_ref[...])
    softplus_neg = jnp.maximum(-z, 0.0) + jnp.log(1.0 + jnp.exp(-jnp.abs(z)))
    w = -softplus_neg - 0.5
    dec = jnp.exp(-jnp.exp(w))
    a = _sigmoid(a0_ref[...] + _dot3(lora, a2h_ref[...], a2l_ref[...]))
    g = _dot3(_sigmoid(gl), g2h_ref[...], g2l_ref[...])
    kk = k * kk_ref[...]
    kk = kk / jnp.maximum(jnp.sqrt(_seg_sum(kk * kk, bd)), L2_EPS)
    kp = k * (1.0 + (a - 1.0) * ka_ref[...])
    na = -kk
    nb = kk * a
    rp = dec * r + na * _seg_sum(nb * r, bd)
    put(na_ref, na)
    put(dec_ref, dec)
    put(nb_ref, nb)
    put(kp_ref, kp)
    put(rp_ref, rp)
    put(v_ref, v)
    vkr_ref[...] = v * _seg_sum(kp * r, bd)
    bonus_ref[...] = _seg_sum(r * kp * rk_ref[...], bd) * v
    g_ref[...] = g


def _prep_seq_kernel(p_ref, *rest):
    weights, outs, carry_ref = rest[:13], rest[13:22], rest[22]

    @pl.when(pl.program_id(1) == 0)
    def _():
        carry_ref[...] = jnp.zeros_like(carry_ref)

    p = p_ref[...]
    tm = p.shape[0]
    row = lax.broadcasted_iota(jnp.int32, p.shape, 0)
    prev = jnp.where(row == 0, carry_ref[7:8, :], pltpu.roll(p, 1, 0))
    carry_ref[7:8, :] = p[tm - 1:tm, :]
    _prep_math(p, prev, *weights, outs)


def _prep_step_kernel(p_ref, prev_ref, *rest):
    weights, outs = rest[:13], rest[13:22]
    _prep_math(p_ref[...], prev_ref[...], *weights, outs)


def _prep(pr, shift_prev, weights, batch, seq, tm):
    m = pr.shape[0]
    w_specs = [_resident(w.shape) for w in weights]
    out_shape = tuple(jax.ShapeDtypeStruct((m, RWKV_WIDTH), F32) for _ in range(9))
    if shift_prev is None:
        nt = seq // tm
        row = lambda w: pl.BlockSpec((tm, w), lambda b, t: (b * nt + t, 0))
        chan = pl.BlockSpec((None, RWKV_WIDTH, tm), lambda b, t: (b, 0, t))
        chan_shape = jax.ShapeDtypeStruct((batch, RWKV_WIDTH, seq), F32)
        return pl.pallas_call(
            _prep_seq_kernel, out_shape=(chan_shape,) * 6 + out_shape[6:], grid=(batch, nt),
            in_specs=[row(RWKV_PROJ)] + w_specs, out_specs=(chan,) * 6 + tuple(row(RWKV_WIDTH) for _ in range(3)),
            scratch_shapes=[pltpu.VMEM((8, RWKV_PROJ), F32)],
            compiler_params=_cparams("parallel", "arbitrary"), name="rwkv_prep_seq",
        )(pr, *weights)
    row = lambda w: pl.BlockSpec((tm, w), lambda i: (i, 0))
    chan = pl.BlockSpec((RWKV_WIDTH, tm), lambda i: (0, i))
    chan_shape = jax.ShapeDtypeStruct((RWKV_WIDTH, m), F32)
    return pl.pallas_call(
        _prep_step_kernel, out_shape=(chan_shape,) * 6 + out_shape[6:], grid=(m // tm,),
        in_specs=[row(RWKV_PROJ), row(RWKV_PROJ)] + w_specs,
        out_specs=(chan,) * 6 + tuple(row(RWKV_WIDTH) for _ in range(3)),
        compiler_params=_cparams("parallel"), name="rwkv_prep_step",
    )(pr, shift_prev, *weights)


def _wkv_seq_kernel(w_ref, nb_ref, k_ref, na_ref, rp_ref, v_ref, y_ref, sout_ref,
                    s_ref, e_ref, first_ref, stage_ref, vt_ref, yt_ref):
    @pl.when(pl.program_id(0) == 0)
    def _():
        s_ref[...] = jnp.zeros_like(s_ref)

    nbatch = w_ref.shape[0]
    tb = w_ref.shape[2]
    zero = jnp.zeros((WKV_SUB, LANES), F32)

    def head_rows(ref, b, c):
        return ref[b, pl.ds(c, RWKV_HEADS, stride=RWKV_HEAD), :]

    def time_major(ref, j):
        rows = jnp.concatenate([head_rows(ref, b, j) for b in range(nbatch)], axis=0)
        return jnp.concatenate([rows] * WKV_Q, axis=0).T

    unroll = 4

    def build_e(jg, carry):
        for j in [jg * unroll + u for u in range(unroll)]:
            for o, ref in enumerate((w_ref, nb_ref, k_ref)):
                e_ref[o, j] = time_major(ref, j)
            for o, ref in ((3, na_ref), (4, rp_ref)):
                rows = time_major(ref, j)
                first_ref[o - 3, j] = rows[0:WKV_GROUP]
                e_ref[o, j] = pltpu.roll(rows, tb - 1, 0)
        return carry

    def build_v(ig, carry):
        for ih in [ig * unroll + u for u in range(unroll)]:
            rows = jnp.concatenate([head_rows(v_ref, b, il * WKV_SUB + ih)
                                    for il in range(WKV_Q) for b in range(nbatch)], axis=0)
            vt_ref[pl.ds(ih, tb, stride=WKV_SUB), :] = rows.T
        return carry

    lax.fori_loop(0, RWKV_HEAD // unroll, build_e, 0)
    lax.fori_loop(0, WKV_SUB // unroll, build_v, 0)

    n_ops = 5
    pairs = [(o, j) for o in range(n_ops) for j in range(RWKV_HEAD)]
    per_step = len(pairs) // WKV_GROUP

    def stage_tiles(slot, group, which):
        rows = pl.ds(pl.multiple_of(jnp.minimum(group, tb // WKV_GROUP - 1) * WKV_GROUP, WKV_GROUP), WKV_GROUP)
        for o, j in which:
            stage_ref[slot, o, j] = e_ref[o, j, rows, :]

    def step(group, u, slot, carry):
        sa, y = carry
        t = group * WKV_GROUP + u
        tile = pl.ds(pl.multiple_of(t * WKV_SUB, WKV_SUB), WKV_SUB)
        yt_ref[tile, :] = y
        v = vt_ref[tile, :]
        stage_tiles(1 - slot, group + 1, pairs[u * per_step:(u + 1) * per_step])
        n_acc = 4
        sa_acc = [zero] * n_acc
        y_acc = [zero] * n_acc
        for j in range(RWKV_HEAD):
            row = lambda o: stage_ref[slot, o, j, u:u + 1, :]
            sn = s_ref[j] * row(0) + sa * row(1) + v * row(2)
            s_ref[j] = sn
            sa_acc[j % n_acc] = sa_acc[j % n_acc] + sn * row(3)
            y_acc[j % n_acc] = y_acc[j % n_acc] + sn * row(4)
        return (sa_acc[0] + sa_acc[1]) + (sa_acc[2] + sa_acc[3]), (y_acc[0] + y_acc[1]) + (y_acc[2] + y_acc[3])

    region = 1

    def two_groups(gg, carry):
        for slot in range(2):
            for u0 in range(0, WKV_GROUP, region):
                def do(c, slot=slot, u0=u0):
                    for u in range(u0, u0 + region):
                        c = step(2 * gg + slot, u, slot, c)
                    return c
                carry = do(carry) if (slot, u0) == (0, 0) else lax.cond(gg >= 0, do, lambda c: c, carry)
        return carry

    def first_matvecs(j, carry):
        sa, y = carry
        sj = s_ref[j]
        return sa + sj * first_ref[0, j, 0:1, :], y + sj * first_ref[1, j, 0:1, :]

    stage_tiles(0, 0, pairs)
    lax.fori_loop(0, tb // (2 * WKV_GROUP), two_groups, lax.fori_loop(0, RWKV_HEAD, first_matvecs, (zero, zero)))

    def emit_y(ig, carry):
        for ih in [ig * unroll + u for u in range(unroll)]:
            cols = yt_ref[pl.ds(ih, tb, stride=WKV_SUB), :].T
            for il in range(WKV_Q):
                for b in range(nbatch):
                    r0 = il * (LANES // WKV_Q) + b * RWKV_HEADS
                    y_ref[b, pl.ds(il * WKV_SUB + ih, RWKV_HEADS, stride=RWKV_HEAD), :] = cols[r0:r0 + RWKV_HEADS]
        return carry

    lax.fori_loop(0, WKV_SUB // unroll, emit_y, 0)

    @pl.when(pl.program_id(0) == pl.num_programs(0) - 1)
    def _():
        sout_ref[...] = s_ref[...]


def _wkv_seq(w, nb, k, na, rp, v):
    batch, _, t = w.shape
    blk = pl.BlockSpec((batch, RWKV_WIDTH, WKV_TB), lambda i: (0, 0, i))
    state = pl.BlockSpec((RWKV_HEAD, WKV_SUB, LANES), lambda i: (0, 0, 0))
    return pl.pallas_call(
        _wkv_seq_kernel,
        out_shape=(jax.ShapeDtypeStruct((batch, RWKV_WIDTH, t), F32),
                   jax.ShapeDtypeStruct((RWKV_HEAD, WKV_SUB, LANES), F32)),
        grid=(t // WKV_TB,),
        in_specs=[blk] * 6,
        out_specs=(blk, state),
        scratch_shapes=[pltpu.VMEM((RWKV_HEAD, WKV_SUB, LANES), F32),
                        pltpu.VMEM((5, RWKV_HEAD, WKV_TB, LANES), F32),
                        pltpu.VMEM((2, RWKV_HEAD, WKV_GROUP, LANES), F32),
                        pltpu.VMEM((2, 5, RWKV_HEAD, WKV_GROUP, LANES), F32),
                        pltpu.VMEM((WKV_TB * WKV_SUB, LANES), F32),
                        pltpu.VMEM((WKV_TB * WKV_SUB, LANES), F32)],
        compiler_params=_cparams("arbitrary"),
        name="wkv_seq",
    )(w, nb, k, na, rp, v)


def _wkv_step_kernel(s_ref, w_ref, nb_ref, k_ref, na_ref, rp_ref, v_ref, y_ref, sout_ref):
    w, nb, k, na, rp = w_ref[...], nb_ref[...], k_ref[...], na_ref[...], rp_ref[...]
    for i in range(RWKV_HEAD):
        si = s_ref[i]
        sa = jnp.sum(si * na, axis=0, keepdims=True)
        y_ref[i:i + 1, :] = jnp.sum(si * rp, axis=0, keepdims=True)
        sout_ref[i] = si * w + sa * nb + v_ref[i:i + 1, :] * k


def _wkv_step(state, w, nb, k, na, rp, v):
    n = state.shape[-1]
    st = pl.BlockSpec((None, RWKV_HEAD, RWKV_HEAD, n), lambda h: (h, 0, 0, 0))
    vec = pl.BlockSpec((RWKV_HEAD, n), lambda h: (h, 0))
    return pl.pallas_call(
        _wkv_step_kernel,
        out_shape=(jax.ShapeDtypeStruct((RWKV_WIDTH, n), F32), jax.ShapeDtypeStruct(state.shape, F32)),
        grid=(RWKV_HEADS,),
        in_specs=[st] + [vec] * 6,
        out_specs=(vec, st),
        compiler_params=_cparams("parallel"),
        name="wkv_step",
    )(state, w, nb, k, na, rp, v)


def _merge_kernel(x_ref, o0_ref, o1_ref, o2_ref, l0_ref, l1_ref, l2_ref, yw_ref, vkr_ref, bonus_ref, g_ref,
                  gate_ref, gng_ref, gnb_ref, bd_ref, wa_ref, wr_ref, wo_ref, out_ref, *, yw_channels_first):
    wide = lambda ref: jnp.concatenate([ref[0], ref[1]], axis=1)
    l0, l1, l2 = wide(l0_ref), wide(l1_ref), wide(l2_ref)
    lm = jnp.maximum(jnp.maximum(l0, l1), l2)
    e0, e1, e2 = jnp.exp(l0 - lm), jnp.exp(l1 - lm), jnp.exp(l2 - lm)
    o_att = (e0 * wide(o0_ref) + e1 * wide(o1_ref) + e2 * wide(o2_ref)) / (e0 + e1 + e2)
    y_att = _dot(o_att.astype(BF16), wa_ref[...])

    bd = bd_ref[...]
    y = (yw_ref[...].T if yw_channels_first else yw_ref[...]) + vkr_ref[...]
    mean = _seg_sum(y, bd) * (1.0 / RWKV_HEAD)
    yc = y - mean
    var = _seg_sum(yc * yc, bd) * (1.0 / RWKV_HEAD)
    yn = yc * lax.rsqrt(var + GN_EPS) * gng_ref[...] + gnb_ref[...]
    o_rwkv = (yn + bonus_ref[...]) * g_ref[...]
    y_rwkv = _dot(o_rwkv.astype(BF16), wr_ref[...])

    gate_att = gate_ref[:, 0:D_MODEL].astype(F32)
    gate_rwkv = gate_ref[:, D_MODEL:2 * D_MODEL].astype(F32)
    merged = _sigmoid(gate_att) * y_att + _sigmoid(gate_rwkv) * y_rwkv
    out_ref[...] = x_ref[...] + _dot(merged.astype(BF16), wo_ref[...])


def _merge(x2d, o_groups, lse_groups, yw, vkr, bonus, g, gates, gn_g, gn_b, bd512, wa, wr, wo, tm):
    m = x2d.shape[0]
    row = lambda w: pl.BlockSpec((tm, w), lambda i: (i, 0))
    channels_first = yw.ndim == 3
    if channels_first:
        nt = yw.shape[2] // tm
        yw_spec = pl.BlockSpec((None, RWKV_WIDTH, tm), lambda i: (i // nt, 0, i % nt))
    else:
        yw_spec = row(RWKV_WIDTH)
    return pl.pallas_call(
        functools.partial(_merge_kernel, yw_channels_first=channels_first),
        out_shape=jax.ShapeDtypeStruct((m, D_MODEL), F32),
        grid=(m // tm,),
        in_specs=[row(D_MODEL)] + [pl.BlockSpec((GROUP_WIDTH // LANES, tm, LANES), lambda i: (0, i, 0))] * 6
                 + [yw_spec] + [row(RWKV_WIDTH)] * 3 + [row(2 * D_MODEL)]
                 + [_resident((1, RWKV_WIDTH)), _resident((1, RWKV_WIDTH)), _resident((SEG_TILE, SEG_TILE)),
                    _resident((GROUP_WIDTH, D_MODEL)), _resident((RWKV_WIDTH, D_MODEL)),
                    _resident((D_MODEL, D_MODEL))],
        out_specs=row(D_MODEL),
        compiler_params=_cparams("parallel"),
        name="merge",
    )(x2d, *o_groups, *lse_groups, yw, vkr, bonus, g, gates, gn_g, gn_b, bd512, wa, wr, wo)


def _ffn_body(x_ref, g2_ref, wup_ref, cw_ref, cb_ref, wdn_ref, out_ref, u_prev, u_sink):
    x = x_ref[...]
    ms = jnp.mean(x * x, axis=-1, keepdims=True)
    xn = (x * lax.rsqrt(ms + RMS_EPS) * g2_ref[...]).astype(BF16)

    def up(c):
        return tuple(_dot(xn, wup_ref[:, c0:c0 + FF_CHUNK]) for c0 in (c, D_FF + c))

    def conv(u, c0):
        c1 = c0 + FF_CHUNK
        u_m2, u_m1 = u_prev(u, c0, c1)
        u_sink(u, c0)
        return cb_ref[:, c0:c1] + u_m2 * cw_ref[0:1, c0:c1] + u_m1 * cw_ref[1:2, c0:c1] + u * cw_ref[2:3, c0:c1]

    acc = x
    nxt = up(0)
    for c in range(0, D_FF, FF_CHUNK):
        ug, uv = nxt
        if c + FF_CHUNK < D_FF:
            nxt = up(c + FF_CHUNK)
        gate = conv(ug, c)
        val = conv(uv, D_FF + c)
        h = gate * _sigmoid(gate) * val
        acc = acc + _dot(h.astype(BF16), wdn_ref[c:c + FF_CHUNK, :])
    out_ref[...] = acc


def _ffn_seq_kernel(*refs, cache_group):
    x_ref, g2_ref, wup_ref, cw_ref, cb_ref, wdn_ref = refs[:6]
    n_in = 6 + (N_CACHE_IN if cache_group is not None else 0)
    out_ref, ust_ref = refs[n_in:n_in + 2]
    carry_ref = refs[-1]
    if cache_group is not None:
        step = pl.program_id(0) * pl.num_programs(1) + pl.program_id(1)
        _cache_attn_body(refs[6:n_in] + refs[n_in + 2:-1], step, cache_group, bb=1)

    @pl.when(pl.program_id(1) == 0)
    def _():
        carry_ref[...] = jnp.zeros_like(carry_ref)

    tm = x_ref.shape[0]
    row = lax.broadcasted_iota(jnp.int32, (tm, FF_CHUNK), 0)

    def u_prev(u, c0, c1):
        c6 = carry_ref[6:7, c0:c1]
        c7 = carry_ref[7:8, c0:c1]
        u_m1 = jnp.where(row == 0, c7, pltpu.roll(u, 1, 0))
        u_m2 = jnp.where(row == 0, c6, jnp.where(row == 1, c7, pltpu.roll(u, 2, 0)))
        return u_m2, u_m1

    def u_sink(u, c0):
        carry_ref[:, c0:c0 + FF_CHUNK] = u[tm - 8:tm, :]

    _ffn_body(x_ref, g2_ref, wup_ref, cw_ref, cb_ref, wdn_ref, out_ref, u_prev, u_sink)
    ust_ref[...] = carry_ref[...]


def _ffn_step_kernel(x_ref, p0_ref, p1_ref, g2_ref, wup_ref, cw_ref, cb_ref, wdn_ref, out_ref, u_ref):
    def u_prev(u, c0, c1):
        return p0_ref[:, c0:c1], p1_ref[:, c0:c1]

    def u_sink(u, c0):
        u_ref[:, c0:c0 + FF_CHUNK] = u

    _ffn_body(x_ref, g2_ref, wup_ref, cw_ref, cb_ref, wdn_ref, out_ref, u_prev, u_sink)


def _ffn(x2d, conv_prev, ln2_g, wup, cw, cb, wdn, batch, seq, tm, cache_rider=None):
    m = x2d.shape[0]
    w_specs = [_resident((1, D_MODEL)), _resident((D_MODEL, 2 * D_FF)), _resident((3, 2 * D_FF)),
               _resident((1, 2 * D_FF)), _resident((D_FF, D_MODEL))]
    if conv_prev is None:
        nt = seq // tm
        row = pl.BlockSpec((tm, D_MODEL), lambda b, t: (b * nt + t, 0))
        in_specs = [row] + w_specs
        out_specs = [row, pl.BlockSpec((None, 8, 2 * D_FF), lambda b, t: (b, 0, 0))]
        out_shape = [jax.ShapeDtypeStruct((m, D_MODEL), F32), jax.ShapeDtypeStruct((batch, 8, 2 * D_FF), F32)]
        cache_group, cache_args = cache_rider if cache_rider is not None else (None, ())
        if cache_rider is not None:
            assert cache_args[5].shape[0] == batch * nt, "one sample sequence per grid step"
            c_in, c_out, c_shape = _cache_attn_specs(cache_args, 1, lambda b, t: b * nt + t)
            in_specs, out_specs, out_shape = in_specs + c_in, out_specs + c_out, out_shape + c_shape
        return pl.pallas_call(
            functools.partial(_ffn_seq_kernel, cache_group=cache_group),
            out_shape=tuple(out_shape),
            grid=(batch, nt),
            in_specs=in_specs,
            out_specs=tuple(out_specs),
            scratch_shapes=[pltpu.VMEM((8, 2 * D_FF), F32)],
            compiler_params=_cparams("arbitrary", "arbitrary"), name="ffn_seq",
        )(x2d, ln2_g, wup, cw, cb, wdn, *cache_args)
    row = lambda w: pl.BlockSpec((tm, w), lambda i: (i, 0))
    return pl.pallas_call(
        _ffn_step_kernel,
        out_shape=(jax.ShapeDtypeStruct((m, D_MODEL), F32), jax.ShapeDtypeStruct((m, 2 * D_FF), F32)),
        grid=(m // tm,),
        in_specs=[row(D_MODEL), row(2 * D_FF), row(2 * D_FF)] + w_specs,
        out_specs=(row(D_MODEL), row(2 * D_FF)),
        compiler_params=_cparams("parallel"), name="ffn_step",
    )(x2d, conv_prev[0], conv_prev[1], ln2_g, wup, cw, cb, wdn)


def _pad_rows(w, before, total):
    return jnp.zeros((total, w.shape[1]), w.dtype).at[before:before + w.shape[0]].set(w)


def kernel(x_prompt, x_sample, cache_k_w128, cache_v_w128, cache_k_w512, cache_v_w512, cache_k_w2048, cache_v_w2048,
           state_rwkv_shift, state_rwkv_wkv, state_ffn_conv, ln1_g, w_in, q_norm_g, k_norm_g, w_attn_out,
           rwkv_mu, rwkv_w0, rwkv_w2, rwkv_a0, rwkv_a2, rwkv_g2, rwkv_k_k, rwkv_k_a, rwkv_r_k, rwkv_gn_g, rwkv_gn_b,
           w_rwkv_out, w_o, ln2_g, w_up, conv_w, conv_b, w_down):
    batch, seq, _ = x_prompt.shape
    nsamp = x_sample.shape[0]
    assert seq % SEG == 0 and x_sample.shape[1] == 1 and batch * RWKV_HEADS * WKV_Q == LANES
    assert ln1_g.shape[0] == 1, "single layer"
    lyr = 0

    w_in_bf = w_in[lyr].astype(BF16)
    bd768 = bd512 = _block_diag_ones(SEG_TILE)
    gq = jnp.tile(q_norm_g[lyr], ATT_WIDTH // HEAD_DIM)[None]
    gk = jnp.tile(k_norm_g[lyr], ATT_WIDTH // HEAD_DIM)[None]
    row1 = lambda a: a.reshape(1, -1)
    prep_w = (row1(rwkv_mu[lyr]), row1(rwkv_w0[lyr]), row1(rwkv_a0[lyr]),
              *_split_bf16(_pad_rows(rwkv_w2[lyr], 0, DECAY_LORA + ICLR_LORA)),
              *_split_bf16(_pad_rows(rwkv_a2[lyr], DECAY_LORA, DECAY_LORA + ICLR_LORA)),
              *_split_bf16(rwkv_g2[lyr]),
              row1(rwkv_k_k[lyr]), row1(rwkv_k_a[lyr]), row1(rwkv_r_k[lyr]), bd512)
    merge_w = (row1(rwkv_gn_g[lyr]), row1(rwkv_gn_b[lyr]), bd512, w_attn_out[lyr].astype(BF16),
               w_rwkv_out[lyr].astype(BF16), w_o[lyr].astype(BF16))
    ffn_w = (row1(ln2_g[lyr]), w_up[lyr].astype(BF16), conv_w[lyr], row1(conv_b[lyr]), w_down[lyr].astype(BF16))

    xs = x_sample.reshape(nsamp, D_MODEL)
    qs, ks, vs, prs, gates_s = _proj(xs, row1(ln1_g[lyr]), w_in_bf, bd768, gq, gk, tm=nsamp)
    caches = [c[lyr].transpose(0, 2, 3, 1).reshape(nsamp, GROUP_WIDTH, c.shape[2])
              for c in (cache_k_w128, cache_v_w128, cache_k_w512, cache_v_w512, cache_k_w2048, cache_v_w2048)]
    for g, (win, dil) in enumerate(ATT_GROUPS):
        assert caches[2 * g].shape[2] == BLOCK * dil, "window caches must be full"
    ks_t, vs_t = ks.T, vs.T
    cache_args = lambda g: (qs, ks, vs, ks_t, vs_t, caches[2 * g], caches[2 * g + 1])
    big, mid = N_GROUPS - 1, N_GROUPS - 2
    cache_done = {}

    xp = x_prompt.reshape(batch * seq, D_MODEL)
    qn, kn, v, pr, gates, *cache_done[mid] = _proj(xp, row1(ln1_g[lyr]), w_in_bf, bd768, gq, gk, tm=256,
                                                   cache_rider=(mid, cache_args(mid)))
    att = [_attn_prompt(qn, kn, v, g, batch, seq) for g in range(N_GROUPS)]
    na, dec, nb, kp, rp, rv, vkr, bonus, gg = _prep(pr, None, prep_w, batch, seq, tm=256)
    yw, s_fin = _wkv_seq(dec, nb, kp, na, rp, rv)
    x2 = _merge(xp, [a[0] for a in att], [a[1] for a in att], yw, vkr, bonus, gg, gates, *merge_w, tm=256)
    yp, ust, *cache_done[big] = _ffn(x2, None, *ffn_w, batch, seq, tm=seq * batch // nsamp,
                                     cache_rider=(big, cache_args(big)))

    y_prompt = yp.reshape(batch, seq, D_MODEL)
    keep_max = min(max(w for w, _ in ATT_GROUPS), seq)
    k_tail, v_tail = _tail_channels_first(kn, v, batch, seq, keep_max, tm=256)
    kv_prompt = []
    for g, (win, _) in enumerate(ATT_GROUPS):
        keep = min(win, seq)
        for tail in (k_tail, v_tail):
            c = tail[:, g * GROUP_WIDTH:(g + 1) * GROUP_WIDTH, keep_max - keep:]
            kv_prompt.append(c.reshape(batch, HEADS_PER_GROUP, HEAD_DIM, keep).transpose(0, 3, 1, 2)[None])
    shift_prompt = pr.reshape(batch, seq, RWKV_PROJ)[None, :, -1]
    wkv_prompt = (s_fin.reshape(RWKV_HEAD, WKV_SUB, WKV_Q, batch, RWKV_HEADS)
                  .transpose(3, 4, 2, 1, 0).reshape(1, batch, RWKV_HEADS, RWKV_HEAD, RWKV_HEAD))
    conv_prompt = ust[None, :, 6:8]

    kv_sample, o_s, lse_s = [], [], []
    for g, (win, dil) in enumerate(ATT_GROUPS):
        k_out, v_out, o_g, lse_g = cache_done[g] if g in cache_done else _cache_attn(cache_args(g), g, bb=8)
        kv_sample += [c.reshape(nsamp, HEADS_PER_GROUP, HEAD_DIM, win).transpose(0, 3, 1, 2)[None]
                      for c in (k_out, v_out)]
        o_s.append(o_g)
        lse_s.append(lse_g)
    na, dec, nb, kp, rp, rv, vkr, bonus, gg = _prep(prs, state_rwkv_shift[lyr], prep_w, nsamp, 1, tm=nsamp)
    state_t = state_rwkv_wkv[lyr].transpose(1, 2, 3, 0)
    yw_s, state_new = _wkv_step(state_t, dec, nb, kp, na, rp, rv)
    x2s = _merge(xs, o_s, lse_s, yw_s[None], vkr, bonus, gg, gates_s, *merge_w, tm=nsamp)
    conv_prev = state_ffn_conv[lyr]
    ys, u_s = _ffn(x2s, (conv_prev[:, 0], conv_prev[:, 1]), *ffn_w, nsamp, 1, tm=nsamp)

    y_sample = ys.reshape(nsamp, 1, D_MODEL)
    shift_sample = prs[None]
    wkv_sample = state_new.transpose(3, 0, 1, 2)[None]
    conv_sample = jnp.stack([conv_prev[:, 1], u_s], axis=1)[None]

    return (y_prompt, y_sample,
            kv_prompt[0], kv_sample[0], kv_prompt[1], kv_sample[1],
            kv_prompt[2], kv_sample[2], kv_prompt[3], kv_sample[3],
            kv_prompt[4], kv_sample[4], kv_prompt[5], kv_sample[5],
            shift_prompt, shift_sample, wkv_prompt, wkv_sample, conv_prompt, conv_sample)
```

```python
import functools

import jax
import jax.numpy as jnp
from jax import lax
from jax.experimental import pallas as pl
from jax.experimental.pallas import tpu as pltpu

F32 = jnp.float32
BF16 = jnp.bfloat16

D_MODEL = 1024
HEAD_DIM = 64
ATT_GROUPS = ((128, 1), (512, 4), (2048, 16))
N_GROUPS = len(ATT_GROUPS)
HEADS_PER_GROUP = 4
GROUP_WIDTH = HEADS_PER_GROUP * HEAD_DIM
ATT_WIDTH = N_GROUPS * GROUP_WIDTH
BLOCK = 128
SEG = 2048
RWKV_HEADS = 8
RWKV_HEAD = 64
RWKV_WIDTH = RWKV_HEADS * RWKV_HEAD
DECAY_LORA = 64
ICLR_LORA = 64
GATE_LORA = 128
RWKV_PROJ = 3 * RWKV_WIDTH + DECAY_LORA + ICLR_LORA + GATE_LORA
LORA_OFF = 3 * RWKV_WIDTH
GATE_OFF = LORA_OFF + DECAY_LORA + ICLR_LORA
PR_OFF = 3 * ATT_WIDTH
GATES_OFF = PR_OFF + RWKV_PROJ
P_TOTAL = GATES_OFF + 2 * D_MODEL
D_FF = 2816
FF_CHUNK = 256
RMS_EPS = 1e-6
GN_EPS = 64e-5
L2_EPS = 1e-12
NEG_BIG = -1e30

LANES = 128
SEG_TILE = 256
WKV_Q = 4
WKV_SUB = RWKV_HEAD // WKV_Q
WKV_TB = 128
WKV_GROUP = 8
VMEM_LIMIT = 56 * 1024 * 1024


def _cparams(*sem):
    return pltpu.CompilerParams(dimension_semantics=sem, vmem_limit_bytes=VMEM_LIMIT)


def _resident(shape):
    n = len(shape)
    return pl.BlockSpec(shape, lambda *_: (0,) * n, pipeline_mode=pl.Buffered(1))


def _split_bf16(x):
    hi = x.astype(BF16)
    lo = (x - hi.astype(F32)).astype(BF16)
    return hi, lo


def _dot(a, b):
    return jnp.dot(a, b, preferred_element_type=F32)


def _dot3(a, b_hi, b_lo):
    a_hi, a_lo = _split_bf16(a)
    return _dot(a_hi, b_hi) + _dot(a_lo, b_hi) + _dot(a_hi, b_lo)


def _seg_sum_bf16(x_bf, ones_bd):
    return jnp.concatenate([_dot(x_bf[:, c:c + SEG_TILE], ones_bd) for c in range(0, x_bf.shape[1], SEG_TILE)],
                           axis=1)


def _seg_sum(x, ones_bd):
    hi, lo = _split_bf16(x)
    return _seg_sum_bf16(hi, ones_bd) + _seg_sum_bf16(lo, ones_bd)


def _sigmoid(x):
    return 1.0 / (1.0 + jnp.exp(-x))


def _block_diag_ones(width):
    idx = jnp.arange(width) // HEAD_DIM
    return (idx[:, None] == idx[None, :]).astype(BF16)


N_PREP_W, N_PREP_OUT = 13, 9


def _proj_kernel(*refs, cache_group, cache_bb, tiles_per_seq):
    with_prep = tiles_per_seq is not None
    x_ref, g1_ref, w_ref, bd_ref, gq_ref, gk_ref = refs[:6]
    n_w = 6 + (N_PREP_W if with_prep else 0)
    n_in = n_w + (N_CACHE_IN if cache_group is not None else 0)
    q_ref, k_ref, v_ref, pr_ref, gate_ref = refs[n_in:n_in + 5]
    n_out = n_in + 5 + (N_PREP_OUT if with_prep else 0)
    if with_prep:
        carry_ref = refs[-1]

        @pl.when(pl.program_id(0) % tiles_per_seq == 0)
        def _():
            carry_ref[...] = jnp.zeros_like(carry_ref)

    if cache_group is not None:
        cache_refs = refs[n_w:n_in] + refs[n_out:n_out + N_CACHE_OUT]
        _cache_attn_body(cache_refs, pl.program_id(0) * cache_bb, cache_group, cache_bb)
    x = x_ref[...]
    ms = jnp.mean(x * x, axis=-1, keepdims=True)
    xn = (x * lax.rsqrt(ms + RMS_EPS) * g1_ref[...]).astype(BF16)

    def proj(c0, c1):
        return _dot(xn, w_ref[:, c0:c1])

    def head_norm(z, g):
        ss = _seg_sum_bf16((z * z).astype(BF16), bd_ref[...])
        return z * lax.rsqrt(ss * (1.0 / HEAD_DIM) + RMS_EPS) * g

    q_ref[...] = head_norm(proj(0, ATT_WIDTH), gq_ref[...])
    k_ref[...] = head_norm(proj(ATT_WIDTH, 2 * ATT_WIDTH), gk_ref[...])
    v_ref[...] = proj(2 * ATT_WIDTH, PR_OFF)
    gate_ref[...] = proj(GATES_OFF, P_TOTAL).astype(gate_ref.dtype)
    pr = proj(PR_OFF, GATES_OFF)
    if not with_prep:
        pr_ref[...] = pr
        return
    tm = pr.shape[0]
    row = lax.broadcasted_iota(jnp.int32, pr.shape, 0)
    prev = jnp.where(row == 0, carry_ref[7:8, :], pltpu.roll(pr, 1, 0))
    carry_ref[...] = pr[tm - 8:tm, :]
    pr_ref[...] = pr[tm - 8:tm, :]
    _prep_math(pr, prev, *refs[6:n_w], refs[n_in + 5:n_out])


def _proj(x2d, ln1_g, w_in_bf, bd768, gq, gk, tm, prep=None, cache_rider=None):
    m = x2d.shape[0]
    row = lambda w: pl.BlockSpec((tm, w), lambda i: (i, 0))
    widths = (ATT_WIDTH, ATT_WIDTH, ATT_WIDTH, RWKV_PROJ, 2 * D_MODEL)
    in_specs = [row(D_MODEL), _resident((1, D_MODEL)), _resident((D_MODEL, P_TOTAL)),
                _resident((SEG_TILE, SEG_TILE)), _resident((1, ATT_WIDTH)), _resident((1, ATT_WIDTH))]
    out_specs = [row(w) for w in widths]
    out_shape = [jax.ShapeDtypeStruct((m, w), BF16 if i == len(widths) - 1 else F32) for i, w in enumerate(widths)]
    prep_w, tiles_per_seq, scratch = (), None, []
    if prep is not None:
        prep_w, batch, seq = prep
        tiles_per_seq = seq // tm
        in_specs += [_resident(w.shape) for w in prep_w]
        out_specs[3] = pl.BlockSpec((None, 8, RWKV_PROJ), lambda i: (i // tiles_per_seq, 0, 0))
        out_shape[3] = jax.ShapeDtypeStruct((batch, 8, RWKV_PROJ), F32)
        chan = pl.BlockSpec((None, RWKV_WIDTH, tm), lambda i: (i // tiles_per_seq, 0, i % tiles_per_seq))
        out_specs += [chan] * 6 + [row(RWKV_WIDTH)] * 3
        out_shape += [jax.ShapeDtypeStruct((batch, RWKV_WIDTH, seq), F32)] * 6
        out_shape += [jax.ShapeDtypeStruct((m, RWKV_WIDTH), F32)] * 3
        scratch = [pltpu.VMEM((8, RWKV_PROJ), F32)]
    cache_group, cache_args, cache_bb = None, (), 0
    if cache_rider is not None:
        cache_group, cache_args = cache_rider
        cache_bb = cache_args[5].shape[0] * tm // m
        assert cache_bb * (m // tm) == cache_args[5].shape[0]
        c_in, c_out, c_shape = _cache_attn_specs(cache_args, cache_bb, lambda i: i)
        in_specs, out_specs, out_shape = in_specs + c_in, out_specs + c_out, out_shape + c_shape
    return pl.pallas_call(
        functools.partial(_proj_kernel, cache_group=cache_group, cache_bb=cache_bb, tiles_per_seq=tiles_per_seq),
        out_shape=tuple(out_shape),
        grid=(m // tm,),
        in_specs=in_specs,
        out_specs=tuple(out_specs),
        scratch_shapes=scratch,
        compiler_params=_cparams("arbitrary"),
        name="proj",
    )(x2d, ln1_g, w_in_bf, bd768, gq, gk, *prep_w, *cache_args)


def _tail_kernel(k_ref, v_ref, kt_ref, vt_ref):
    kt_ref[...] = k_ref[...].T
    vt_ref[...] = v_ref[...].T


def _tail_channels_first(kn, v, batch, seq, keep, tm):
    assert keep % tm == 0 and seq % tm == 0
    first = (seq - keep) // tm
    rows = pl.BlockSpec((tm, ATT_WIDTH), lambda b, t: (b * (seq // tm) + first + t, 0))
    cols = pl.BlockSpec((None, ATT_WIDTH, tm), lambda b, t: (b, 0, t))
    shape = jax.ShapeDtypeStruct((batch, ATT_WIDTH, keep), F32)
    return pl.pallas_call(
        _tail_kernel, out_shape=(shape, shape), grid=(batch, keep // tm),
        in_specs=[rows, rows], out_specs=(cols, cols),
        compiler_params=_cparams("parallel", "parallel"), name="kv_tail",
    )(kn, v)


def _attn_prompt_kernel(*refs, dil):
    q_ref, kp_ref, kc_ref, vp_ref, vc_ref = (refs[2 * i:2 * i + 2] for i in range(5))
    o_ref, lse_ref = refs[10], refs[11]
    not_first_seg = pl.program_id(1) > 0
    nblk = SEG // (BLOCK * dil)
    stacked = (HEADS_PER_GROUP * BLOCK, 2 * BLOCK)
    row = lax.broadcasted_iota(jnp.int32, stacked, 0) % BLOCK
    col = lax.broadcasted_iota(jnp.int32, stacked, 1)
    in_prev = col < BLOCK
    band_prev = in_prev & (col >= row)
    band_cur = jnp.logical_not(in_prev) & (col - BLOCK <= row)
    lane = lax.broadcasted_iota(jnp.int32, (1, GROUP_WIDTH), 1)
    head_masks = [(lane // HEAD_DIM) == h for h in range(HEADS_PER_GROUP)]

    def row_slice(start):
        return pl.ds(start, BLOCK) if dil == 1 else pl.ds(start, BLOCK, stride=dil)

    def rows(halves, start):
        return jnp.concatenate([h[row_slice(start), :] for h in halves], axis=1)

    def block(q_start, k1_ref, k1_start, v1_ref, prev_valid):
        q = rows(q_ref, q_start) * (HEAD_DIM ** -0.5)
        q4 = jnp.concatenate([jnp.where(hm, q, 0.0) for hm in head_masks], axis=0).astype(BF16)
        kt = jnp.concatenate([rows(k1_ref, k1_start), rows(kc_ref, q_start)], axis=0).astype(BF16)
        vt = jnp.concatenate([rows(v1_ref, k1_start), rows(vc_ref, q_start)], axis=0).astype(BF16)
        s = lax.dot_general(q4, kt, (((1,), (1,)), ((), ())), preferred_element_type=F32)
        s = jnp.where(band_cur | (band_prev & prev_valid), s, NEG_BIG)
        m = jnp.max(s, axis=-1, keepdims=True)
        p = jnp.exp(s - m)
        l = jnp.sum(p, axis=-1, keepdims=True)
        ov = _dot(p.astype(BF16), vt) / l
        lse = m + jnp.log(l)
        o = jnp.zeros((BLOCK, GROUP_WIDTH), F32)
        lse_b = jnp.zeros((BLOCK, GROUP_WIDTH), F32)
        for h, hm in enumerate(head_masks):
            sl = slice(h * BLOCK, (h + 1) * BLOCK)
            o = jnp.where(hm, ov[sl], o)
            lse_b = jnp.where(hm, lse[sl], lse_b)
        for half in range(2):
            lanes = slice(half * LANES, (half + 1) * LANES)
            o_ref[half, row_slice(q_start), :] = o[:, lanes]
            lse_ref[half, row_slice(q_start), :] = lse_b[:, lanes]

    def first_blocks(r, carry):
        block(r, kp_ref, SEG - BLOCK * dil + r, vp_ref, not_first_seg)
        return carry

    def later_blocks(i, carry):
        r = i % dil
        nb = 1 + i // dil
        q_start = nb * BLOCK * dil + r
        block(q_start, kc_ref, q_start - BLOCK * dil, vc_ref, True)
        return carry

    lax.fori_loop(0, dil, first_blocks, 0, unroll=min(2, dil))
    if nblk > 1:
        lax.fori_loop(0, dil * (nblk - 1), later_blocks, 0, unroll=2)


def _attn_prompt(qn, kn, v, group, batch, seq):
    dil = ATT_GROUPS[group][1]
    nseg = seq // SEG
    halves = GROUP_WIDTH // LANES
    cur = [pl.BlockSpec((SEG, LANES), lambda b, s, c=halves * group + h: (b * nseg + s, c)) for h in range(halves)]
    prev = [pl.BlockSpec((SEG, LANES), lambda b, s, c=halves * group + h: (b * nseg + jnp.maximum(s - 1, 0), c))
            for h in range(halves)]
    out = pl.BlockSpec((halves, SEG, LANES), lambda b, s: (0, b * nseg + s, 0))
    shape = jax.ShapeDtypeStruct((halves, batch * seq, LANES), F32)
    return pl.pallas_call(
        functools.partial(_attn_prompt_kernel, dil=dil),
        out_shape=(shape, shape),
        grid=(batch, nseg),
        in_specs=cur + prev + cur + prev + cur,
        out_specs=(out, out),
        compiler_params=_cparams("parallel", "arbitrary"),
        name=f"attn_prompt_g{group}",
    )(qn, qn, kn, kn, kn, kn, v, v, v, v)


def _cache_attn_body(refs, first, group, bb):
    q_ref, kn_ref, vn_ref, knt_ref, vnt_ref, kc_ref, vc_ref, ko_ref, vo_ref, o_ref, lse_ref = refs
    dil = ATT_GROUPS[group][1]
    win = kc_ref.shape[2]
    nseq = q_ref.shape[0]
    cols = slice(group * GROUP_WIDTH, (group + 1) * GROUP_WIDTH)
    pos = lax.broadcasted_iota(jnp.int32, (1, win), 1)
    in_window = (pos % dil) == 0
    newest = lax.broadcasted_iota(jnp.int32, (GROUP_WIDTH, LANES), 1) == LANES - 1
    head_rows = (lax.broadcasted_iota(jnp.int32, (8, GROUP_WIDTH), 0)
                 == lax.broadcasted_iota(jnp.int32, (8, GROUP_WIDTH), 1) // HEAD_DIM)
    seq_lane = lax.broadcasted_iota(jnp.int32, (GROUP_WIDTH, nseq), 1)

    def hi_lo_rows(x):
        hi, lo = _split_bf16(x)
        return jnp.concatenate([hi, lo], axis=0)

    def fold(x):
        return x[0:8] + x[8:16]

    def shifted(cache, new_t_ref, n):
        new_col = jnp.sum(jnp.where(seq_lane == n, new_t_ref[cols, :], 0.0), axis=1, keepdims=True)
        rolled = pltpu.roll(cache, win - 1, 1)
        last_tile = jnp.where(newest, new_col, rolled[:, win - LANES:])
        return last_tile if win == LANES else jnp.concatenate([rolled[:, :win - LANES], last_tile], axis=1)

    for b in range(bb):
        n = first + b
        q = q_ref[pl.ds(n, 1), cols] * (HEAD_DIM ** -0.5)
        k_new = kn_ref[pl.ds(n, 1), cols]
        v_new = vn_ref[pl.ds(n, 1), cols]
        q8 = jnp.where(head_rows, q, 0.0)
        kc = kc_ref[b]
        vc = vc_ref[b]
        s = fold(_dot(hi_lo_rows(q8), kc.astype(BF16)))
        s = jnp.where(in_window, s, NEG_BIG)
        s_new = jnp.sum(q8 * k_new, axis=1, keepdims=True)
        m = jnp.maximum(jnp.max(s, axis=1, keepdims=True), s_new)
        p = jnp.exp(s - m)
        p_new = jnp.exp(s_new - m)
        l = jnp.sum(p, axis=1, keepdims=True) + p_new
        pv = fold(lax.dot_general(hi_lo_rows(p), vc.astype(BF16), (((1,), (1,)), ((), ())),
                                  preferred_element_type=F32))
        ov = (pv + p_new * v_new) / l
        o = jnp.sum(jnp.where(head_rows, ov, 0.0), axis=0, keepdims=True)
        lse = jnp.sum(jnp.where(head_rows, m + jnp.log(l), 0.0), axis=0, keepdims=True)
        for half in range(GROUP_WIDTH // LANES):
            lanes = slice(half * LANES, (half + 1) * LANES)
            o_ref[half, pl.ds(n, 1), :] = o[:, lanes]
            lse_ref[half, pl.ds(n, 1), :] = lse[:, lanes]
        ko_ref[b] = shifted(kc, knt_ref, n)
        vo_ref[b] = shifted(vc, vnt_ref, n)


N_CACHE_IN, N_CACHE_OUT = 7, 4


def _cache_attn_specs(args, bb, block_of):
    n, _, win = args[5].shape
    halves = GROUP_WIDTH // LANES
    cache = pl.BlockSpec((bb, GROUP_WIDTH, win), lambda *g: (block_of(*g), 0, 0))
    rows = _resident((n, ATT_WIDTH))
    rows_t = _resident((ATT_WIDTH, n))
    out = pl.BlockSpec((halves, n, LANES), lambda *g: (0, 0, 0))
    cache_shape = jax.ShapeDtypeStruct(args[5].shape, F32)
    o_shape = jax.ShapeDtypeStruct((halves, n, LANES), F32)
    return [rows] * 3 + [rows_t] * 2 + [cache] * 2, [cache, cache, out, out], [cache_shape] * 2 + [o_shape] * 2


def _cache_attn_kernel(*refs, group, bb):
    _cache_attn_body(refs, pl.program_id(0) * bb, group, bb)


def _cache_attn(args, group, bb):
    in_specs, out_specs, out_shape = _cache_attn_specs(args, bb, lambda i: i)
    return pl.pallas_call(
        functools.partial(_cache_attn_kernel, group=group, bb=bb),
        out_shape=tuple(out_shape),
        grid=(args[5].shape[0] // bb,),
        in_specs=in_specs,
        out_specs=tuple(out_specs),
        compiler_params=_cparams("arbitrary"),
        name=f"cache_attn_g{group}",
    )(*args)


def _prep_math(p, prev, mu_ref, w0_ref, a0_ref, w2h_ref, w2l_ref, a2h_ref, a2l_ref, g2h_ref, g2l_ref,
               kk_ref, ka_ref, rk_ref, bd_ref, outs):
    na_ref, dec_ref, nb_ref, kp_ref, rp_ref, v_ref, vkr_ref, bonus_ref, g_ref = outs
    bd = bd_ref[...]

    def put(ref, val):
        ref[...] = val.T

    xs = p + mu_ref[...] * (prev - p)
    r = xs[:, 0:RWKV_WIDTH]
    k = xs[:, RWKV_WIDTH:2 * RWKV_WIDTH]
    v = xs[:, 2 * RWKV_WIDTH:3 * RWKV_WIDTH]
    lora = xs[:, LORA_OFF:GATE_OFF]
    gl = xs[:, GATE_OFF:RWKV_PROJ]
    z = w0_ref[...] + _dot3(jnp.tanh(lora), w2h_ref[...], w2l_ref[...])
    softplus_neg = jnp.maximum(-z, 0.0) + jnp.log(1.0 + jnp.exp(-jnp.abs(z)))
    w = -softplus_neg - 0.5
    dec = jnp.exp(-jnp.exp(w))
    a = _sigmoid(a0_ref[...] + _dot3(lora, a2h_ref[...], a2l_ref[...]))
    g = _dot3(_sigmoid(gl), g2h_ref[...], g2l_ref[...])
    kk = k * kk_ref[...]
    kk = kk / jnp.maximum(jnp.sqrt(_seg_sum(kk * kk, bd)), L2_EPS)
    kp = k * (1.0 + (a - 1.0) * ka_ref[...])
    na = -kk
    nb = kk * a
    rp = dec * r + na * _seg_sum(nb * r, bd)
    put(na_ref, na)
    put(dec_ref, dec)
    put(nb_ref, nb)
    put(kp_ref, kp)
    put(rp_ref, rp)
    put(v_ref, v)
    vkr_ref[...] = v * _seg_sum(kp * r, bd)
    bonus_ref[...] = _seg_sum(r * kp * rk_ref[...], bd) * v
    g_ref[...] = g


def _prep_step_kernel(p_ref, prev_ref, *rest):
    weights, outs = rest[:N_PREP_W], rest[N_PREP_W:N_PREP_W + N_PREP_OUT]
    _prep_math(p_ref[...], prev_ref[...], *weights, outs)


def _prep_step(pr, shift_prev, weights, tm):
    m = pr.shape[0]
    w_specs = [_resident(w.shape) for w in weights]
    out_shape = tuple(jax.ShapeDtypeStruct((m, RWKV_WIDTH), F32) for _ in range(9))
    row = lambda w: pl.BlockSpec((tm, w), lambda i: (i, 0))
    chan = pl.BlockSpec((RWKV_WIDTH, tm), lambda i: (0, i))
    chan_shape = jax.ShapeDtypeStruct((RWKV_WIDTH, m), F32)
    return pl.pallas_call(
        _prep_step_kernel, out_shape=(chan_shape,) * 6 + out_shape[6:], grid=(m // tm,),
        in_specs=[row(RWKV_PROJ), row(RWKV_PROJ)] + w_specs,
        out_specs=(chan,) * 6 + tuple(row(RWKV_WIDTH) for _ in range(3)),
        compiler_params=_cparams("parallel"), name="rwkv_prep_step",
    )(pr, shift_prev, *weights)


def _wkv_seq_kernel(w_ref, nb_ref, k_ref, na_ref, rp_ref, v_ref, y_ref, sout_ref,
                    s_ref, e_ref, first_ref, stage_ref, vt_ref, yt_ref):
    @pl.when(pl.program_id(0) == 0)
    def _():
        s_ref[...] = jnp.zeros_like(s_ref)

    nbatch = w_ref.shape[0]
    tb = w_ref.shape[2]
    zero = jnp.zeros((WKV_SUB, LANES), F32)

    def head_rows(ref, b, c):
        return ref[b, pl.ds(c, RWKV_HEADS, stride=RWKV_HEAD), :]

    def time_major(ref, j):
        rows = jnp.concatenate([head_rows(ref, b, j) for b in range(nbatch)], axis=0)
        return jnp.concatenate([rows] * WKV_Q, axis=0).T

    unroll = 4

    def build_e(jg, carry):
        for j in [jg * unroll + u for u in range(unroll)]:
            for o, ref in enumerate((w_ref, nb_ref, k_ref)):
                e_ref[o, j] = time_major(ref, j)
            for o, ref in ((3, na_ref), (4, rp_ref)):
                rows = time_major(ref, j)
                first_ref[o - 3, j] = rows[0:WKV_GROUP]
                e_ref[o, j] = pltpu.roll(rows, tb - 1, 0)
        return carry

    def build_v(ig, carry):
        for ih in [ig * unroll + u for u in range(unroll)]:
            rows = jnp.concatenate([head_rows(v_ref, b, il * WKV_SUB + ih)
                                    for il in range(WKV_Q) for b in range(nbatch)], axis=0)
            vt_ref[pl.ds(ih, tb, stride=WKV_SUB), :] = rows.T
        return carry

    lax.fori_loop(0, RWKV_HEAD // unroll, build_e, 0)
    lax.fori_loop(0, WKV_SUB // unroll, build_v, 0)

    n_ops = 5
    pairs = [(o, j) for o in range(n_ops) for j in range(RWKV_HEAD)]
    per_step = len(pairs) // WKV_GROUP

    def stage_tiles(slot, group, which):
        rows = pl.ds(pl.multiple_of(jnp.minimum(group, tb // WKV_GROUP - 1) * WKV_GROUP, WKV_GROUP), WKV_GROUP)
        for o, j in which:
            stage_ref[slot, o, j] = e_ref[o, j, rows, :]

    def step(group, u, slot, carry):
        sa, y = carry
        t = group * WKV_GROUP + u
        tile = pl.ds(pl.multiple_of(t * WKV_SUB, WKV_SUB), WKV_SUB)
        yt_ref[tile, :] = y
        v = vt_ref[tile, :]
        stage_tiles(1 - slot, group + 1, pairs[u * per_step:(u + 1) * per_step])
        n_acc = 4
        sa_acc = [zero] * n_acc
        y_acc = [zero] * n_acc
        for j in range(RWKV_HEAD):
            row = lambda o: stage_ref[slot, o, j, u:u + 1, :]
            sn = s_ref[j] * row(0) + sa * row(1) + v * row(2)
            s_ref[j] = sn
            sa_acc[j % n_acc] = sa_acc[j % n_acc] + sn * row(3)
            y_acc[j % n_acc] = y_acc[j % n_acc] + sn * row(4)
        return (sa_acc[0] + sa_acc[1]) + (sa_acc[2] + sa_acc[3]), (y_acc[0] + y_acc[1]) + (y_acc[2] + y_acc[3])

    region = 1

    def two_groups(gg, carry):
        for slot in range(2):
            for u0 in range(0, WKV_GROUP, region):
                def do(c, slot=slot, u0=u0):
                    for u in range(u0, u0 + region):
                        c = step(2 * gg + slot, u, slot, c)
                    return c
                carry = do(carry) if (slot, u0) == (0, 0) else lax.cond(gg >= 0, do, lambda c: c, carry)
        return carry

    def first_matvecs(j, carry):
        sa, y = carry
        sj = s_ref[j]
        return sa + sj * first_ref[0, j, 0:1, :], y + sj * first_ref[1, j, 0:1, :]

    stage_tiles(0, 0, pairs)
    lax.fori_loop(0, tb // (2 * WKV_GROUP), two_groups, lax.fori_loop(0, RWKV_HEAD, first_matvecs, (zero, zero)))

    def emit_y(ig, carry):
        for ih in [ig * unroll + u for u in range(unroll)]:
            cols = yt_ref[pl.ds(ih, tb, stride=WKV_SUB), :].T
            for il in range(WKV_Q):
                for b in range(nbatch):
                    r0 = il * (LANES // WKV_Q) + b * RWKV_HEADS
                    y_ref[b, pl.ds(il * WKV_SUB + ih, RWKV_HEADS, stride=RWKV_HEAD), :] = cols[r0:r0 + RWKV_HEADS]
        return carry

    lax.fori_loop(0, WKV_SUB // unroll, emit_y, 0)

    @pl.when(pl.program_id(0) == pl.num_programs(0) - 1)
    def _():
        sout_ref[...] = s_ref[...]


def _wkv_seq(w, nb, k, na, rp, v):
    batch, _, t = w.shape
    blk = pl.BlockSpec((batch, RWKV_WIDTH, WKV_TB), lambda i: (0, 0, i))
    state = pl.BlockSpec((RWKV_HEAD, WKV_SUB, LANES), lambda i: (0, 0, 0))
    return pl.pallas_call(
        _wkv_seq_kernel,
        out_shape=(jax.ShapeDtypeStruct((batch, RWKV_WIDTH, t), F32),
                   jax.ShapeDtypeStruct((RWKV_HEAD, WKV_SUB, LANES), F32)),
        grid=(t // WKV_TB,),
        in_specs=[blk] * 6,
        out_specs=(blk, state),
        scratch_shapes=[pltpu.VMEM((RWKV_HEAD, WKV_SUB, LANES), F32),
                        pltpu.VMEM((5, RWKV_HEAD, WKV_TB, LANES), F32),
                        pltpu.VMEM((2, RWKV_HEAD, WKV_GROUP, LANES), F32),
                        pltpu.VMEM((2, 5, RWKV_HEAD, WKV_GROUP, LANES), F32),
                        pltpu.VMEM((WKV_TB * WKV_SUB, LANES), F32),
                        pltpu.VMEM((WKV_TB * WKV_SUB, LANES), F32)],
        compiler_params=_cparams("arbitrary"),
        name="wkv_seq",
    )(w, nb, k, na, rp, v)


def _wkv_step_kernel(s_ref, w_ref, nb_ref, k_ref, na_ref, rp_ref, v_ref, y_ref, sout_ref):
    w, nb, k, na, rp = w_ref[...], nb_ref[...], k_ref[...], na_ref[...], rp_ref[...]
    for i in range(RWKV_HEAD):
        si = s_ref[i]
        sa = jnp.sum(si * na, axis=0, keepdims=True)
        y_ref[i:i + 1, :] = jnp.sum(si * rp, axis=0, keepdims=True)
        sout_ref[i] = si * w + sa * nb + v_ref[i:i + 1, :] * k


def _wkv_step(state, w, nb, k, na, rp, v):
    n = state.shape[-1]
    st = pl.BlockSpec((None, RWKV_HEAD, RWKV_HEAD, n), lambda h: (h, 0, 0, 0))
    vec = pl.BlockSpec((RWKV_HEAD, n), lambda h: (h, 0))
    return pl.pallas_call(
        _wkv_step_kernel,
        out_shape=(jax.ShapeDtypeStruct((RWKV_WIDTH, n), F32), jax.ShapeDtypeStruct(state.shape, F32)),
        grid=(RWKV_HEADS,),
        in_specs=[st] + [vec] * 6,
        out_specs=(vec, st),
        compiler_params=_cparams("parallel"),
        name="wkv_step",
    )(state, w, nb, k, na, rp, v)


N_MERGE_IN = 18


def _merge_tile(refs, yw_channels_first):
    (x_ref, o0_ref, o1_ref, o2_ref, l0_ref, l1_ref, l2_ref, yw_ref, vkr_ref, bonus_ref, g_ref,
     gate_ref, gng_ref, gnb_ref, bd_ref, wa_ref, wr_ref, wo_ref) = refs
    wide = lambda ref: jnp.concatenate([ref[0], ref[1]], axis=1)
    l0, l1, l2 = wide(l0_ref), wide(l1_ref), wide(l2_ref)
    lm = jnp.maximum(jnp.maximum(l0, l1), l2)
    e0, e1, e2 = jnp.exp(l0 - lm), jnp.exp(l1 - lm), jnp.exp(l2 - lm)
    o_att = (e0 * wide(o0_ref) + e1 * wide(o1_ref) + e2 * wide(o2_ref)) / (e0 + e1 + e2)
    y_att = _dot(o_att.astype(BF16), wa_ref[...])

    bd = bd_ref[...]
    y = (yw_ref[...].T if yw_channels_first else yw_ref[...]) + vkr_ref[...]
    mean = _seg_sum(y, bd) * (1.0 / RWKV_HEAD)
    yc = y - mean
    var = _seg_sum(yc * yc, bd) * (1.0 / RWKV_HEAD)
    yn = yc * lax.rsqrt(var + GN_EPS) * gng_ref[...] + gnb_ref[...]
    o_rwkv = (yn + bonus_ref[...]) * g_ref[...]
    y_rwkv = _dot(o_rwkv.astype(BF16), wr_ref[...])

    gate_att = gate_ref[:, 0:D_MODEL].astype(F32)
    gate_rwkv = gate_ref[:, D_MODEL:2 * D_MODEL].astype(F32)
    merged = _sigmoid(gate_att) * y_att + _sigmoid(gate_rwkv) * y_rwkv
    return x_ref[...] + _dot(merged.astype(BF16), wo_ref[...])


def _merge_kernel(*refs, yw_channels_first):
    refs[N_MERGE_IN][...] = _merge_tile(refs[:N_MERGE_IN], yw_channels_first)


def _merge_specs(yw, tm, tile_of):
    row = lambda w: pl.BlockSpec((tm, w), lambda *g: (tile_of(*g), 0))
    if yw.ndim == 3:
        nt = yw.shape[2] // tm
        yw_spec = pl.BlockSpec((None, RWKV_WIDTH, tm), lambda *g: (tile_of(*g) // nt, 0, tile_of(*g) % nt))
    else:
        yw_spec = row(RWKV_WIDTH)
    halves = pl.BlockSpec((GROUP_WIDTH // LANES, tm, LANES), lambda *g: (0, tile_of(*g), 0))
    return ([row(D_MODEL)] + [halves] * 6 + [yw_spec] + [row(RWKV_WIDTH)] * 3 + [row(2 * D_MODEL)]
            + [_resident((1, RWKV_WIDTH)), _resident((1, RWKV_WIDTH)), _resident((SEG_TILE, SEG_TILE)),
               _resident((GROUP_WIDTH, D_MODEL)), _resident((RWKV_WIDTH, D_MODEL)), _resident((D_MODEL, D_MODEL))])


def _merge(x2d, o_groups, lse_groups, yw, vkr, bonus, g, gates, gn_g, gn_b, bd512, wa, wr, wo, tm):
    m = x2d.shape[0]
    return pl.pallas_call(
        functools.partial(_merge_kernel, yw_channels_first=yw.ndim == 3),
        out_shape=jax.ShapeDtypeStruct((m, D_MODEL), F32),
        grid=(m // tm,),
        in_specs=_merge_specs(yw, tm, lambda i: i),
        out_specs=pl.BlockSpec((tm, D_MODEL), lambda i: (i, 0)),
        compiler_params=_cparams("parallel"),
        name="merge",
    )(x2d, *o_groups, *lse_groups, yw, vkr, bonus, g, gates, gn_g, gn_b, bd512, wa, wr, wo)


def _ffn_body(x, g2_ref, wup_ref, cw_ref, cb_ref, wdn_ref, out_ref, u_prev, u_sink):
    ms = jnp.mean(x * x, axis=-1, keepdims=True)
    xn = (x * lax.rsqrt(ms + RMS_EPS) * g2_ref[...]).astype(BF16)

    def up(c):
        return tuple(_dot(xn, wup_ref[:, c0:c0 + FF_CHUNK]) for c0 in (c, D_FF + c))

    def conv(u, c0):
        c1 = c0 + FF_CHUNK
        u_m2, u_m1 = u_prev(u, c0, c1)
        u_sink(u, c0)
        return cb_ref[:, c0:c1] + u_m2 * cw_ref[0:1, c0:c1] + u_m1 * cw_ref[1:2, c0:c1] + u * cw_ref[2:3, c0:c1]

    acc = x
    nxt = up(0)
    for c in range(0, D_FF, FF_CHUNK):
        ug, uv = nxt
        if c + FF_CHUNK < D_FF:
            nxt = up(c + FF_CHUNK)
        gate = conv(ug, c)
        val = conv(uv, D_FF + c)
        h = gate * _sigmoid(gate) * val
        acc = acc + _dot(h.astype(BF16), wdn_ref[c:c + FF_CHUNK, :])
    out_ref[...] = acc


def _ffn_seq_kernel(*refs, cache_group, yw_channels_first):
    g2_ref, wup_ref, cw_ref, cb_ref, wdn_ref = refs[N_MERGE_IN:N_MERGE_IN + 5]
    n_in = N_MERGE_IN + 5 + (N_CACHE_IN if cache_group is not None else 0)
    out_ref, ust_ref = refs[n_in:n_in + 2]
    carry_ref = refs[-1]

    @pl.when(pl.program_id(1) == 0)
    def _():
        carry_ref[...] = jnp.zeros_like(carry_ref)

    if cache_group is not None:
        step = pl.program_id(0) * pl.num_programs(1) + pl.program_id(1)
        _cache_attn_body(refs[N_MERGE_IN + 5:n_in] + refs[n_in + 2:-1], step, cache_group, bb=1)

    x = _merge_tile(refs[:N_MERGE_IN], yw_channels_first)
    tm = x.shape[0]
    row = lax.broadcasted_iota(jnp.int32, (tm, FF_CHUNK), 0)

    def u_prev(u, c0, c1):
        c6 = carry_ref[6:7, c0:c1]
        c7 = carry_ref[7:8, c0:c1]
        u_m1 = jnp.where(row == 0, c7, pltpu.roll(u, 1, 0))
        u_m2 = jnp.where(row == 0, c6, jnp.where(row == 1, c7, pltpu.roll(u, 2, 0)))
        return u_m2, u_m1

    def u_sink(u, c0):
        carry_ref[:, c0:c0 + FF_CHUNK] = u[tm - 8:tm, :]

    _ffn_body(x, g2_ref, wup_ref, cw_ref, cb_ref, wdn_ref, out_ref, u_prev, u_sink)
    ust_ref[...] = carry_ref[...]


def _ffn_step_kernel(x_ref, p0_ref, p1_ref, g2_ref, wup_ref, cw_ref, cb_ref, wdn_ref, out_ref, u_ref):
    def u_prev(u, c0, c1):
        return p0_ref[:, c0:c1], p1_ref[:, c0:c1]

    def u_sink(u, c0):
        u_ref[:, c0:c0 + FF_CHUNK] = u

    _ffn_body(x_ref[...], g2_ref, wup_ref, cw_ref, cb_ref, wdn_ref, out_ref, u_prev, u_sink)


_FFN_W_SPECS = lambda: [_resident((1, D_MODEL)), _resident((D_MODEL, 2 * D_FF)), _resident((3, 2 * D_FF)),
                        _resident((1, 2 * D_FF)), _resident((D_FF, D_MODEL))]


def _merge_ffn_seq(merge_args, ffn_w, batch, seq, tm, cache_rider=None):
    m = merge_args[0].shape[0]
    nt = seq // tm
    tile_of = lambda b, t: b * nt + t
    row = pl.BlockSpec((tm, D_MODEL), lambda b, t: (tile_of(b, t), 0))
    yw = merge_args[7]
    in_specs = _merge_specs(yw, tm, tile_of) + _FFN_W_SPECS()
    out_specs = [row, pl.BlockSpec((None, 8, 2 * D_FF), lambda b, t: (b, 0, 0))]
    out_shape = [jax.ShapeDtypeStruct((m, D_MODEL), F32), jax.ShapeDtypeStruct((batch, 8, 2 * D_FF), F32)]
    cache_group, cache_args = cache_rider if cache_rider is not None else (None, ())
    if cache_rider is not None:
        assert cache_args[5].shape[0] == batch * nt, "one sample sequence per grid step"
        c_in, c_out, c_shape = _cache_attn_specs(cache_args, 1, tile_of)
        in_specs, out_specs, out_shape = in_specs + c_in, out_specs + c_out, out_shape + c_shape
    return pl.pallas_call(
        functools.partial(_ffn_seq_kernel, cache_group=cache_group, yw_channels_first=yw.ndim == 3),
        out_shape=tuple(out_shape),
        grid=(batch, nt),
        in_specs=in_specs,
        out_specs=tuple(out_specs),
        scratch_shapes=[pltpu.VMEM((8, 2 * D_FF), F32)],
        compiler_params=_cparams("arbitrary", "arbitrary"), name="merge_ffn_seq",
    )(*merge_args, *ffn_w, *cache_args)


def _ffn_step(x2d, conv_prev, ln2_g, wup, cw, cb, wdn, tm):
    m = x2d.shape[0]
    w_specs = _FFN_W_SPECS()
    row = lambda w: pl.BlockSpec((tm, w), lambda i: (i, 0))
    return pl.pallas_call(
        _ffn_step_kernel,
        out_shape=(jax.ShapeDtypeStruct((m, D_MODEL), F32), jax.ShapeDtypeStruct((m, 2 * D_FF), F32)),
        grid=(m // tm,),
        in_specs=[row(D_MODEL), row(2 * D_FF), row(2 * D_FF)] + w_specs,
        out_specs=(row(D_MODEL), row(2 * D_FF)),
        compiler_params=_cparams("parallel"), name="ffn_step",
    )(x2d, conv_prev[0], conv_prev[1], ln2_g, wup, cw, cb, wdn)


def _pad_rows(w, before, total):
    return jnp.zeros((total, w.shape[1]), w.dtype).at[before:before + w.shape[0]].set(w)


def kernel(x_prompt, x_sample, cache_k_w128, cache_v_w128, cache_k_w512, cache_v_w512, cache_k_w2048, cache_v_w2048,
           state_rwkv_shift, state_rwkv_wkv, state_ffn_conv, ln1_g, w_in, q_norm_g, k_norm_g, w_attn_out,
           rwkv_mu, rwkv_w0, rwkv_w2, rwkv_a0, rwkv_a2, rwkv_g2, rwkv_k_k, rwkv_k_a, rwkv_r_k, rwkv_gn_g, rwkv_gn_b,
           w_rwkv_out, w_o, ln2_g, w_up, conv_w, conv_b, w_down):
    batch, seq, _ = x_prompt.shape
    nsamp = x_sample.shape[0]
    assert seq % SEG == 0 and x_sample.shape[1] == 1 and batch * RWKV_HEADS * WKV_Q == LANES
    assert ln1_g.shape[0] == 1, "single layer"
    lyr = 0

    w_in_bf = w_in[lyr].astype(BF16)
    bd768 = bd512 = _block_diag_ones(SEG_TILE)
    gq = jnp.tile(q_norm_g[lyr], ATT_WIDTH // HEAD_DIM)[None]
    gk = jnp.tile(k_norm_g[lyr], ATT_WIDTH // HEAD_DIM)[None]
    row1 = lambda a: a.reshape(1, -1)
    prep_w = (row1(rwkv_mu[lyr]), row1(rwkv_w0[lyr]), row1(rwkv_a0[lyr]),
              *_split_bf16(_pad_rows(rwkv_w2[lyr], 0, DECAY_LORA + ICLR_LORA)),
              *_split_bf16(_pad_rows(rwkv_a2[lyr], DECAY_LORA, DECAY_LORA + ICLR_LORA)),
              *_split_bf16(rwkv_g2[lyr]),
              row1(rwkv_k_k[lyr]), row1(rwkv_k_a[lyr]), row1(rwkv_r_k[lyr]), bd512)
    merge_w = (row1(rwkv_gn_g[lyr]), row1(rwkv_gn_b[lyr]), bd512, w_attn_out[lyr].astype(BF16),
               w_rwkv_out[lyr].astype(BF16), w_o[lyr].astype(BF16))
    ffn_w = (row1(ln2_g[lyr]), w_up[lyr].astype(BF16), conv_w[lyr], row1(conv_b[lyr]), w_down[lyr].astype(BF16))

    xs = x_sample.reshape(nsamp, D_MODEL)
    qs, ks, vs, prs, gates_s = _proj(xs, row1(ln1_g[lyr]), w_in_bf, bd768, gq, gk, tm=nsamp)
    caches = [c[lyr].transpose(0, 2, 3, 1).reshape(nsamp, GROUP_WIDTH, c.shape[2])
              for c in (cache_k_w128, cache_v_w128, cache_k_w512, cache_v_w512, cache_k_w2048, cache_v_w2048)]
    for g, (win, dil) in enumerate(ATT_GROUPS):
        assert caches[2 * g].shape[2] == BLOCK * dil, "window caches must be full"
    ks_t, vs_t = ks.T, vs.T
    cache_args = lambda g: (qs, ks, vs, ks_t, vs_t, caches[2 * g], caches[2 * g + 1])
    big, mid = N_GROUPS - 1, N_GROUPS - 2
    cache_done = {}

    xp = x_prompt.reshape(batch * seq, D_MODEL)
    first_half = _proj(xp, row1(ln1_g[lyr]), w_in_bf, bd768, gq, gk, tm=256, prep=(prep_w, batch, seq),
                       cache_rider=(mid, cache_args(mid)))
    qn, kn, v, pr_last, gates = first_half[:5]
    na, dec, nb, kp, rp, rv, vkr, bonus, gg = first_half[5:5 + N_PREP_OUT]
    cache_done[mid] = first_half[5 + N_PREP_OUT:]
    att = [_attn_prompt(qn, kn, v, g, batch, seq) for g in range(N_GROUPS)]
    yw, s_fin = _wkv_seq(dec, nb, kp, na, rp, rv)
    merge_args = (xp, *[a[0] for a in att], *[a[1] for a in att], yw, vkr, bonus, gg, gates, *merge_w)
    yp, ust, *cache_done[big] = _merge_ffn_seq(merge_args, ffn_w, batch, seq, tm=seq * batch // nsamp,
                                               cache_rider=(big, cache_args(big)))

    y_prompt = yp.reshape(batch, seq, D_MODEL)
    keep_max = min(max(w for w, _ in ATT_GROUPS), seq)
    k_tail, v_tail = _tail_channels_first(kn, v, batch, seq, keep_max, tm=256)
    kv_prompt = []
    for g, (win, _) in enumerate(ATT_GROUPS):
        keep = min(win, seq)
        for tail in (k_tail, v_tail):
            c = tail[:, g * GROUP_WIDTH:(g + 1) * GROUP_WIDTH, keep_max - keep:]
            kv_prompt.append(c.reshape(batch, HEADS_PER_GROUP, HEAD_DIM, keep).transpose(0, 3, 1, 2)[None])
    shift_prompt = pr_last[None, :, -1]
    wkv_prompt = (s_fin.reshape(RWKV_HEAD, WKV_SUB, WKV_Q, batch, RWKV_HEADS)
                  .transpose(3, 4, 2, 1, 0).reshape(1, batch, RWKV_HEADS, RWKV_HEAD, RWKV_HEAD))
    conv_prompt = ust[None, :, 6:8]

    kv_sample, o_s, lse_s = [], [], []
    for g, (win, dil) in enumerate(ATT_GROUPS):
        k_out, v_out, o_g, lse_g = cache_done[g] if g in cache_done else _cache_attn(cache_args(g), g, bb=8)
        kv_sample += [c.reshape(nsamp, HEADS_PER_GROUP, HEAD_DIM, win).transpose(0, 3, 1, 2)[None]
                      for c in (k_out, v_out)]
        o_s.append(o_g)
        lse_s.append(lse_g)
    na, dec, nb, kp, rp, rv, vkr, bonus, gg = _prep_step(prs, state_rwkv_shift[lyr], prep_w, tm=nsamp)
    state_t = state_rwkv_wkv[lyr].transpose(1, 2, 3, 0)
    yw_s, state_new = _wkv_step(state_t, dec, nb, kp, na, rp, rv)
    x2s = _merge(xs, o_s, lse_s, yw_s[None], vkr, bonus, gg, gates_s, *merge_w, tm=nsamp)
    conv_prev = state_ffn_conv[lyr]
    ys, u_s = _ffn_step(x2s, (conv_prev[:, 0], conv_prev[:, 1]), *ffn_w, tm=nsamp)

    y_sample = ys.reshape(nsamp, 1, D_MODEL)
    shift_sample = prs[None]
    wkv_sample = state_new.transpose(3, 0, 1, 2)[None]
    conv_sample = jnp.stack([conv_prev[:, 1], u_s], axis=1)[None]

    return (y_prompt, y_sample,
            kv_prompt[0], kv_sample[0], kv_prompt[1], kv_sample[1],
            kv_prompt[2], kv_sample[2], kv_prompt[3], kv_sample[3],
            kv_prompt[4], kv_sample[4], kv_prompt[5], kv_sample[5],
            shift_prompt, shift_sample, wkv_prompt, wkv_sample, conv_prompt, conv_sample)
```

```python
import functools

import jax
import jax.numpy as jnp
from jax import lax
from jax.experimental import pallas as pl
from jax.experimental.pallas import tpu as pltpu

F32 = jnp.float32
BF16 = jnp.bfloat16

D_MODEL = 1024
HEAD_DIM = 64
ATT_GROUPS = ((128, 1), (512, 4), (2048, 16))
N_GROUPS = len(ATT_GROUPS)
HEADS_PER_GROUP = 4
GROUP_WIDTH = HEADS_PER_GROUP * HEAD_DIM
ATT_WIDTH = N_GROUPS * GROUP_WIDTH
BLOCK = 128
SEG = 2048
RWKV_HEADS = 8
RWKV_HEAD = 64
RWKV_WIDTH = RWKV_HEADS * RWKV_HEAD
DECAY_LORA = 64
ICLR_LORA = 64
GATE_LORA = 128
RWKV_PROJ = 3 * RWKV_WIDTH + DECAY_LORA + ICLR_LORA + GATE_LORA
LORA_OFF = 3 * RWKV_WIDTH
GATE_OFF = LORA_OFF + DECAY_LORA + ICLR_LORA
PR_OFF = 3 * ATT_WIDTH
GATES_OFF = PR_OFF + RWKV_PROJ
P_TOTAL = GATES_OFF + 2 * D_MODEL
D_FF = 2816
FF_CHUNK = 256
RMS_EPS = 1e-6
GN_EPS = 64e-5
L2_EPS = 1e-12
NEG_BIG = -1e30

LANES = 128
SEG_TILE = 256
WKV_Q = 4
WKV_SUB = RWKV_HEAD // WKV_Q
WKV_TB = 128
WKV_GROUP = 8
VMEM_LIMIT = 56 * 1024 * 1024


def _cparams(*sem):
    return pltpu.CompilerParams(dimension_semantics=sem, vmem_limit_bytes=VMEM_LIMIT)


def _resident(shape):
    n = len(shape)
    return pl.BlockSpec(shape, lambda *_: (0,) * n, pipeline_mode=pl.Buffered(1))


def _split_bf16(x):
    hi = x.astype(BF16)
    lo = (x - hi.astype(F32)).astype(BF16)
    return hi, lo


def _dot(a, b):
    return jnp.dot(a, b, preferred_element_type=F32)


def _dot3(a, b_hi, b_lo):
    a_hi, a_lo = _split_bf16(a)
    return _dot(a_hi, b_hi) + _dot(a_lo, b_hi) + _dot(a_hi, b_lo)


def _seg_sum_bf16(x_bf, ones_bd):
    return jnp.concatenate([_dot(x_bf[:, c:c + SEG_TILE], ones_bd) for c in range(0, x_bf.shape[1], SEG_TILE)],
                           axis=1)


def _seg_sum(x, ones_bd):
    hi, lo = _split_bf16(x)
    return _seg_sum_bf16(hi, ones_bd) + _seg_sum_bf16(lo, ones_bd)


def _sigmoid(x):
    return 1.0 / (1.0 + jnp.exp(-x))


def _block_diag_ones(width):
    idx = jnp.arange(width) // HEAD_DIM
    return (idx[:, None] == idx[None, :]).astype(BF16)


N_PREP_W, N_PREP_OUT = 13, 9


N_SEQ_OUT = N_PREP_OUT + 2


def _proj_kernel(*refs, cache_group, cache_bb, tiles_per_seq):
    with_prep = tiles_per_seq is not None
    x_ref, g1_ref, w_ref, bd_ref, gq_ref, gk_ref = refs[:6]
    n_w = 6 + (N_PREP_W if with_prep else 0)
    n_in = n_w + (N_CACHE_IN if cache_group is not None else 0)
    q_ref, k_ref, v_ref, pr_ref, gate_ref = refs[n_in:n_in + 5]
    n_out = n_in + 5 + (N_SEQ_OUT if with_prep else 0)
    if with_prep:
        carry_ref = refs[-1]

        @pl.when(pl.program_id(0) % tiles_per_seq == 0)
        def _():
            carry_ref[...] = jnp.zeros_like(carry_ref)

    if cache_group is not None:
        cache_refs = refs[n_w:n_in] + refs[n_out:n_out + N_CACHE_OUT]
        _cache_attn_body(cache_refs, pl.program_id(0) * cache_bb, cache_group, cache_bb)
    x = x_ref[...]
    ms = jnp.mean(x * x, axis=-1, keepdims=True)
    xn = (x * lax.rsqrt(ms + RMS_EPS) * g1_ref[...]).astype(BF16)

    def proj(c0, c1):
        return _dot(xn, w_ref[:, c0:c1])

    def head_norm(z, g):
        ss = _seg_sum_bf16((z * z).astype(BF16), bd_ref[...])
        return z * lax.rsqrt(ss * (1.0 / HEAD_DIM) + RMS_EPS) * g

    q_ref[...] = head_norm(proj(0, ATT_WIDTH), gq_ref[...])
    k = head_norm(proj(ATT_WIDTH, 2 * ATT_WIDTH), gk_ref[...])
    v = proj(2 * ATT_WIDTH, PR_OFF)
    k_ref[...] = k
    v_ref[...] = v
    gate_ref[...] = proj(GATES_OFF, P_TOTAL).astype(gate_ref.dtype)
    pr = proj(PR_OFF, GATES_OFF)
    if not with_prep:
        pr_ref[...] = pr
        return
    kt_ref, vt_ref = refs[n_out - 2:n_out]
    kt_ref[...] = k.T
    vt_ref[...] = v.T
    tm = pr.shape[0]
    row = lax.broadcasted_iota(jnp.int32, pr.shape, 0)
    prev = jnp.where(row == 0, carry_ref[7:8, :], pltpu.roll(pr, 1, 0))
    carry_ref[...] = pr[tm - 8:tm, :]
    pr_ref[...] = pr[tm - 8:tm, :]
    _prep_math(pr, prev, *refs[6:n_w], refs[n_in + 5:n_in + 5 + N_PREP_OUT])


def _proj(x2d, ln1_g, w_in_bf, bd768, gq, gk, tm, prep=None, cache_rider=None):
    m = x2d.shape[0]
    row = lambda w: pl.BlockSpec((tm, w), lambda i: (i, 0))
    widths = (ATT_WIDTH, ATT_WIDTH, ATT_WIDTH, RWKV_PROJ, 2 * D_MODEL)
    in_specs = [row(D_MODEL), _resident((1, D_MODEL)), _resident((D_MODEL, P_TOTAL)),
                _resident((SEG_TILE, SEG_TILE)), _resident((1, ATT_WIDTH)), _resident((1, ATT_WIDTH))]
    out_specs = [row(w) for w in widths]
    out_shape = [jax.ShapeDtypeStruct((m, w), BF16 if i == len(widths) - 1 else F32) for i, w in enumerate(widths)]
    prep_w, tiles_per_seq, scratch = (), None, []
    if prep is not None:
        prep_w, batch, seq, keep = prep
        assert keep % tm == 0 and seq % tm == 0
        tiles_per_seq = seq // tm
        in_specs += [_resident(w.shape) for w in prep_w]
        out_specs[3] = pl.BlockSpec((None, 8, RWKV_PROJ), lambda i: (i // tiles_per_seq, 0, 0))
        out_shape[3] = jax.ShapeDtypeStruct((batch, 8, RWKV_PROJ), F32)
        chan = pl.BlockSpec((None, RWKV_WIDTH, tm), lambda i: (i // tiles_per_seq, 0, i % tiles_per_seq))
        out_specs += [chan] * 6 + [row(RWKV_WIDTH)] * 3
        out_shape += [jax.ShapeDtypeStruct((batch, RWKV_WIDTH, seq), F32)] * 6
        out_shape += [jax.ShapeDtypeStruct((m, RWKV_WIDTH), F32)] * 3
        first = (seq - keep) // tm
        tail = pl.BlockSpec((None, ATT_WIDTH, tm),
                            lambda i: (i // tiles_per_seq, 0, jnp.maximum(i % tiles_per_seq - first, 0)))
        out_specs += [tail] * 2
        out_shape += [jax.ShapeDtypeStruct((batch, ATT_WIDTH, keep), F32)] * 2
        scratch = [pltpu.VMEM((8, RWKV_PROJ), F32)]
    cache_group, cache_args, cache_bb = None, (), 0
    if cache_rider is not None:
        cache_group, cache_args = cache_rider
        cache_bb = cache_args[5].shape[0] * tm // m
        assert cache_bb * (m // tm) == cache_args[5].shape[0]
        c_in, c_out, c_shape = _cache_attn_specs(cache_args, cache_bb, lambda i: i)
        in_specs, out_specs, out_shape = in_specs + c_in, out_specs + c_out, out_shape + c_shape
    return pl.pallas_call(
        functools.partial(_proj_kernel, cache_group=cache_group, cache_bb=cache_bb, tiles_per_seq=tiles_per_seq),
        out_shape=tuple(out_shape),
        grid=(m // tm,),
        in_specs=in_specs,
        out_specs=tuple(out_specs),
        scratch_shapes=scratch,
        compiler_params=_cparams("arbitrary"),
        name="proj",
    )(x2d, ln1_g, w_in_bf, bd768, gq, gk, *prep_w, *cache_args)


def _attn_prompt_kernel(*refs, dil):
    q_ref, kp_ref, kc_ref, vp_ref, vc_ref = (refs[2 * i:2 * i + 2] for i in range(5))
    o_ref, lse_ref = refs[10], refs[11]
    not_first_seg = pl.program_id(1) > 0
    nblk = SEG // (BLOCK * dil)
    stacked = (HEADS_PER_GROUP * BLOCK, 2 * BLOCK)
    row = lax.broadcasted_iota(jnp.int32, stacked, 0) % BLOCK
    col = lax.broadcasted_iota(jnp.int32, stacked, 1)
    in_prev = col < BLOCK
    band_prev = in_prev & (col >= row)
    band_cur = jnp.logical_not(in_prev) & (col - BLOCK <= row)
    bias_cur = jnp.where(band_cur, 0.0, NEG_BIG)
    bias_both = jnp.where(band_cur | band_prev, 0.0, NEG_BIG)
    lane = lax.broadcasted_iota(jnp.int32, (1, GROUP_WIDTH), 1)
    head_masks = [(lane // HEAD_DIM) == h for h in range(HEADS_PER_GROUP)]

    def row_slice(start):
        return pl.ds(start, BLOCK) if dil == 1 else pl.ds(start, BLOCK, stride=dil)

    def rows(halves, start):
        return jnp.concatenate([h[row_slice(start), :] for h in halves], axis=1)

    def block(q_start, k1_ref, k1_start, v1_ref, prev_valid):
        q = rows(q_ref, q_start) * (HEAD_DIM ** -0.5)
        q4 = jnp.concatenate([jnp.where(hm, q, 0.0) for hm in head_masks], axis=0).astype(BF16)
        kt = jnp.concatenate([rows(k1_ref, k1_start), rows(kc_ref, q_start)], axis=0).astype(BF16)
        vt = jnp.concatenate([rows(v1_ref, k1_start), rows(vc_ref, q_start)], axis=0).astype(BF16)
        s = lax.dot_general(q4, kt, (((1,), (1,)), ((), ())), preferred_element_type=F32)
        s = s + (bias_both if prev_valid is True else jnp.where(prev_valid, bias_both, bias_cur))
        m = jnp.max(s, axis=-1, keepdims=True)
        p = jnp.exp(s - m)
        l = jnp.sum(p, axis=-1, keepdims=True)
        ov = _dot(p.astype(BF16), vt)
        lse = m + jnp.log(l)
        o = jnp.zeros((BLOCK, GROUP_WIDTH), F32)
        l_b = jnp.ones((BLOCK, GROUP_WIDTH), F32)
        lse_b = jnp.zeros((BLOCK, GROUP_WIDTH), F32)
        for h, hm in enumerate(head_masks):
            sl = slice(h * BLOCK, (h + 1) * BLOCK)
            o = jnp.where(hm, ov[sl], o)
            l_b = jnp.where(hm, l[sl], l_b)
            lse_b = jnp.where(hm, lse[sl], lse_b)
        o = o / l_b
        for half in range(2):
            lanes = slice(half * LANES, (half + 1) * LANES)
            o_ref[half, row_slice(q_start), :] = o[:, lanes]
            lse_ref[half, row_slice(q_start), :] = lse_b[:, lanes]

    def first_blocks(r, carry):
        block(r, kp_ref, SEG - BLOCK * dil + r, vp_ref, not_first_seg)
        return carry

    def later_blocks(i, carry):
        r = i % dil
        nb = 1 + i // dil
        q_start = nb * BLOCK * dil + r
        block(q_start, kc_ref, q_start - BLOCK * dil, vc_ref, True)
        return carry

    lax.fori_loop(0, dil, first_blocks, 0, unroll=min(2, dil))
    if nblk > 1:
        lax.fori_loop(0, dil * (nblk - 1), later_blocks, 0, unroll=2)


def _attn_prompt(qn, kn, v, group, batch, seq):
    dil = ATT_GROUPS[group][1]
    nseg = seq // SEG
    halves = GROUP_WIDTH // LANES
    cur = [pl.BlockSpec((SEG, LANES), lambda b, s, c=halves * group + h: (b * nseg + s, c)) for h in range(halves)]
    prev = [pl.BlockSpec((SEG, LANES), lambda b, s, c=halves * group + h: (b * nseg + jnp.maximum(s - 1, 0), c))
            for h in range(halves)]
    out = pl.BlockSpec((halves, SEG, LANES), lambda b, s: (0, b * nseg + s, 0))
    shape = jax.ShapeDtypeStruct((halves, batch * seq, LANES), F32)
    return pl.pallas_call(
        functools.partial(_attn_prompt_kernel, dil=dil),
        out_shape=(shape, shape),
        grid=(batch, nseg),
        in_specs=cur + prev + cur + prev + cur,
        out_specs=(out, out),
        compiler_params=_cparams("parallel", "arbitrary"),
        name=f"attn_prompt_g{group}",
    )(qn, qn, kn, kn, kn, kn, v, v, v, v)


def _cache_attn_body(refs, first, group, bb):
    q_ref, kn_ref, vn_ref, knt_ref, vnt_ref, kc_ref, vc_ref, ko_ref, vo_ref, o_ref, lse_ref = refs
    dil = ATT_GROUPS[group][1]
    win = kc_ref.shape[2]
    nseq = q_ref.shape[0]
    cols = slice(group * GROUP_WIDTH, (group + 1) * GROUP_WIDTH)
    pos = lax.broadcasted_iota(jnp.int32, (1, win), 1)
    in_window = (pos % dil) == 0
    newest = lax.broadcasted_iota(jnp.int32, (GROUP_WIDTH, LANES), 1) == LANES - 1
    head_rows = (lax.broadcasted_iota(jnp.int32, (8, GROUP_WIDTH), 0)
                 == lax.broadcasted_iota(jnp.int32, (8, GROUP_WIDTH), 1) // HEAD_DIM)
    seq_lane = lax.broadcasted_iota(jnp.int32, (GROUP_WIDTH, nseq), 1)

    def hi_lo_rows(x):
        hi, lo = _split_bf16(x)
        return jnp.concatenate([hi, lo], axis=0)

    def fold(x):
        return x[0:8] + x[8:16]

    def shifted(cache, new_t_ref, n):
        new_col = jnp.sum(jnp.where(seq_lane == n, new_t_ref[cols, :], 0.0), axis=1, keepdims=True)
        rolled = pltpu.roll(cache, win - 1, 1)
        last_tile = jnp.where(newest, new_col, rolled[:, win - LANES:])
        return last_tile if win == LANES else jnp.concatenate([rolled[:, :win - LANES], last_tile], axis=1)

    for b in range(bb):
        n = first + b
        q = q_ref[pl.ds(n, 1), cols] * (HEAD_DIM ** -0.5)
        k_new = kn_ref[pl.ds(n, 1), cols]
        v_new = vn_ref[pl.ds(n, 1), cols]
        q8 = jnp.where(head_rows, q, 0.0)
        kc = kc_ref[b]
        vc = vc_ref[b]
        s = fold(_dot(hi_lo_rows(q8), kc.astype(BF16)))
        s = jnp.where(in_window, s, NEG_BIG)
        s_new = jnp.sum(q8 * k_new, axis=1, keepdims=True)
        m = jnp.maximum(jnp.max(s, axis=1, keepdims=True), s_new)
        p = jnp.exp(s - m)
        p_new = jnp.exp(s_new - m)
        l = jnp.sum(p, axis=1, keepdims=True) + p_new
        pv = fold(lax.dot_general(hi_lo_rows(p), vc.astype(BF16), (((1,), (1,)), ((), ())),
                                  preferred_element_type=F32))
        ov = (pv + p_new * v_new) / l
        o = jnp.sum(jnp.where(head_rows, ov, 0.0), axis=0, keepdims=True)
        lse = jnp.sum(jnp.where(head_rows, m + jnp.log(l), 0.0), axis=0, keepdims=True)
        for half in range(GROUP_WIDTH // LANES):
            lanes = slice(half * LANES, (half + 1) * LANES)
            o_ref[half, pl.ds(n, 1), :] = o[:, lanes]
            lse_ref[half, pl.ds(n, 1), :] = lse[:, lanes]
        ko_ref[b] = shifted(kc, knt_ref, n)
        vo_ref[b] = shifted(vc, vnt_ref, n)


N_CACHE_IN, N_CACHE_OUT = 7, 4


def _cache_attn_specs(args, bb, block_of):
    n, _, win = args[5].shape
    halves = GROUP_WIDTH // LANES
    cache = pl.BlockSpec((bb, GROUP_WIDTH, win), lambda *g: (block_of(*g), 0, 0))
    rows = _resident((n, ATT_WIDTH))
    rows_t = _resident((ATT_WIDTH, n))
    out = pl.BlockSpec((halves, n, LANES), lambda *g: (0, 0, 0))
    cache_shape = jax.ShapeDtypeStruct(args[5].shape, F32)
    o_shape = jax.ShapeDtypeStruct((halves, n, LANES), F32)
    return [rows] * 3 + [rows_t] * 2 + [cache] * 2, [cache, cache, out, out], [cache_shape] * 2 + [o_shape] * 2


def _cache_attn_kernel(*refs, group, bb):
    _cache_attn_body(refs, pl.program_id(0) * bb, group, bb)


def _cache_attn(args, group, bb):
    in_specs, out_specs, out_shape = _cache_attn_specs(args, bb, lambda i: i)
    return pl.pallas_call(
        functools.partial(_cache_attn_kernel, group=group, bb=bb),
        out_shape=tuple(out_shape),
        grid=(args[5].shape[0] // bb,),
        in_specs=in_specs,
        out_specs=tuple(out_specs),
        compiler_params=_cparams("arbitrary"),
        name=f"cache_attn_g{group}",
    )(*args)


def _prep_math(p, prev, mu_ref, w0_ref, a0_ref, w2h_ref, w2l_ref, a2h_ref, a2l_ref, g2h_ref, g2l_ref,
               kk_ref, ka_ref, rk_ref, bd_ref, outs):
    na_ref, dec_ref, nb_ref, kp_ref, rp_ref, v_ref, vkr_ref, bonus_ref, g_ref = outs
    bd = bd_ref[...]

    def put(ref, val):
        ref[...] = val.T

    xs = p + mu_ref[...] * (prev - p)
    r = xs[:, 0:RWKV_WIDTH]
    k = xs[:, RWKV_WIDTH:2 * RWKV_WIDTH]
    v = xs[:, 2 * RWKV_WIDTH:3 * RWKV_WIDTH]
    lora = xs[:, LORA_OFF:GATE_OFF]
    gl = xs[:, GATE_OFF:RWKV_PROJ]
    z = w0_ref[...] + _dot3(jnp.tanh(lora), w2h_ref[...], w2l_ref[...])
    softplus_neg = jnp.maximum(-z, 0.0) + jnp.log(1.0 + jnp.exp(-jnp.abs(z)))
    w = -softplus_neg - 0.5
    dec = jnp.exp(-jnp.exp(w))
    a = _sigmoid(a0_ref[...] + _dot3(lora, a2h_ref[...], a2l_ref[...]))
    g = _dot3(_sigmoid(gl), g2h_ref[...], g2l_ref[...])
    kk = k * kk_ref[...]
    kk = kk / jnp.maximum(jnp.sqrt(_seg_sum(kk * kk, bd)), L2_EPS)
    kp = k * (1.0 + (a - 1.0) * ka_ref[...])
    na = -kk
    nb = kk * a
    rp = dec * r + na * _seg_sum(nb * r, bd)
    put(na_ref, na)
    put(dec_ref, dec)
    put(nb_ref, nb)
    put(kp_ref, kp)
    put(rp_ref, rp)
    put(v_ref, v)
    vkr_ref[...] = v * _seg_sum(kp * r, bd)
    bonus_ref[...] = _seg_sum(r * kp * rk_ref[...], bd) * v
    g_ref[...] = g


def _prep_step_kernel(p_ref, prev_ref, *rest):
    weights, outs = rest[:N_PREP_W], rest[N_PREP_W:N_PREP_W + N_PREP_OUT]
    _prep_math(p_ref[...], prev_ref[...], *weights, outs)


def _prep_step(pr, shift_prev, weights, tm):
    m = pr.shape[0]
    w_specs = [_resident(w.shape) for w in weights]
    out_shape = tuple(jax.ShapeDtypeStruct((m, RWKV_WIDTH), F32) for _ in range(9))
    row = lambda w: pl.BlockSpec((tm, w), lambda i: (i, 0))
    chan = pl.BlockSpec((RWKV_WIDTH, tm), lambda i: (0, i))
    chan_shape = jax.ShapeDtypeStruct((RWKV_WIDTH, m), F32)
    return pl.pallas_call(
        _prep_step_kernel, out_shape=(chan_shape,) * 6 + out_shape[6:], grid=(m // tm,),
        in_specs=[row(RWKV_PROJ), row(RWKV_PROJ)] + w_specs,
        out_specs=(chan,) * 6 + tuple(row(RWKV_WIDTH) for _ in range(3)),
        compiler_params=_cparams("parallel"), name="rwkv_prep_step",
    )(pr, shift_prev, *weights)


def _wkv_seq_kernel(w_ref, nb_ref, k_ref, na_ref, rp_ref, v_ref, y_ref, sout_ref,
                    s_ref, e_ref, first_ref, stage_ref, vt_ref, yt_ref):
    @pl.when(pl.program_id(0) == 0)
    def _():
        s_ref[...] = jnp.zeros_like(s_ref)

    nbatch = w_ref.shape[0]
    tb = w_ref.shape[2]
    zero = jnp.zeros((WKV_SUB, LANES), F32)

    def head_rows(ref, b, c):
        return ref[b, pl.ds(c, RWKV_HEADS, stride=RWKV_HEAD), :]

    def time_major(ref, j):
        rows = jnp.concatenate([head_rows(ref, b, j) for b in range(nbatch)], axis=0)
        return jnp.concatenate([rows] * WKV_Q, axis=0).T

    unroll = 4

    def build_e(jg, carry):
        for j in [jg * unroll + u for u in range(unroll)]:
            for o, ref in enumerate((w_ref, nb_ref, k_ref)):
                e_ref[o, j] = time_major(ref, j)
            for o, ref in ((3, na_ref), (4, rp_ref)):
                rows = time_major(ref, j)
                first_ref[o - 3, j] = rows[0:WKV_GROUP]
                e_ref[o, j] = pltpu.roll(rows, tb - 1, 0)
        return carry

    def build_v(ig, carry):
        for ih in [ig * unroll + u for u in range(unroll)]:
            rows = jnp.concatenate([head_rows(v_ref, b, il * WKV_SUB + ih)
                                    for il in range(WKV_Q) for b in range(nbatch)], axis=0)
            vt_ref[pl.ds(ih, tb, stride=WKV_SUB), :] = rows.T
        return carry

    lax.fori_loop(0, RWKV_HEAD // unroll, build_e, 0)
    lax.fori_loop(0, WKV_SUB // unroll, build_v, 0)

    n_ops = 5
    pairs = [(o, j) for o in range(n_ops) for j in range(RWKV_HEAD)]
    per_step = len(pairs) // WKV_GROUP

    def stage_tiles(slot, group, which):
        rows = pl.ds(pl.multiple_of(jnp.minimum(group, tb // WKV_GROUP - 1) * WKV_GROUP, WKV_GROUP), WKV_GROUP)
        for o, j in which:
            stage_ref[slot, o, j] = e_ref[o, j, rows, :]

    def step(group, u, slot, carry):
        sa, y = carry
        t = group * WKV_GROUP + u
        tile = pl.ds(pl.multiple_of(t * WKV_SUB, WKV_SUB), WKV_SUB)
        yt_ref[tile, :] = y
        v = vt_ref[tile, :]
        stage_tiles(1 - slot, group + 1, pairs[u * per_step:(u + 1) * per_step])
        n_acc = 4
        sa_acc = [zero] * n_acc
        y_acc = [zero] * n_acc
        for j in range(RWKV_HEAD):
            row = lambda o: stage_ref[slot, o, j, u:u + 1, :]
            sn = s_ref[j] * row(0) + sa * row(1) + v * row(2)
            s_ref[j] = sn
            sa_acc[j % n_acc] = sa_acc[j % n_acc] + sn * row(3)
            y_acc[j % n_acc] = y_acc[j % n_acc] + sn * row(4)
        return (sa_acc[0] + sa_acc[1]) + (sa_acc[2] + sa_acc[3]), (y_acc[0] + y_acc[1]) + (y_acc[2] + y_acc[3])

    region = 1

    def two_groups(gg, carry):
        for slot in range(2):
            for u0 in range(0, WKV_GROUP, region):
                def do(c, slot=slot, u0=u0):
                    for u in range(u0, u0 + region):
                        c = step(2 * gg + slot, u, slot, c)
                    return c
                carry = do(carry) if (slot, u0) == (0, 0) else lax.cond(gg >= 0, do, lambda c: c, carry)
        return carry

    def first_matvecs(j, carry):
        sa, y = carry
        sj = s_ref[j]
        return sa + sj * first_ref[0, j, 0:1, :], y + sj * first_ref[1, j, 0:1, :]

    stage_tiles(0, 0, pairs)
    lax.fori_loop(0, tb // (2 * WKV_GROUP), two_groups, lax.fori_loop(0, RWKV_HEAD, first_matvecs, (zero, zero)))

    def emit_y(ig, carry):
        for ih in [ig * unroll + u for u in range(unroll)]:
            cols = yt_ref[pl.ds(ih, tb, stride=WKV_SUB), :].T
            for il in range(WKV_Q):
                for b in range(nbatch):
                    r0 = il * (LANES // WKV_Q) + b * RWKV_HEADS
                    y_ref[b, pl.ds(il * WKV_SUB + ih, RWKV_HEADS, stride=RWKV_HEAD), :] = cols[r0:r0 + RWKV_HEADS]
        return carry

    lax.fori_loop(0, WKV_SUB // unroll, emit_y, 0)

    @pl.when(pl.program_id(0) == pl.num_programs(0) - 1)
    def _():
        sout_ref[...] = s_ref[...]


def _wkv_seq(w, nb, k, na, rp, v):
    batch, _, t = w.shape
    blk = pl.BlockSpec((batch, RWKV_WIDTH, WKV_TB), lambda i: (0, 0, i))
    state = pl.BlockSpec((RWKV_HEAD, WKV_SUB, LANES), lambda i: (0, 0, 0))
    return pl.pallas_call(
        _wkv_seq_kernel,
        out_shape=(jax.ShapeDtypeStruct((batch, RWKV_WIDTH, t), F32),
                   jax.ShapeDtypeStruct((RWKV_HEAD, WKV_SUB, LANES), F32)),
        grid=(t // WKV_TB,),
        in_specs=[blk] * 6,
        out_specs=(blk, state),
        scratch_shapes=[pltpu.VMEM((RWKV_HEAD, WKV_SUB, LANES), F32),
                        pltpu.VMEM((5, RWKV_HEAD, WKV_TB, LANES), F32),
                        pltpu.VMEM((2, RWKV_HEAD, WKV_GROUP, LANES), F32),
                        pltpu.VMEM((2, 5, RWKV_HEAD, WKV_GROUP, LANES), F32),
                        pltpu.VMEM((WKV_TB * WKV_SUB, LANES), F32),
                        pltpu.VMEM((WKV_TB * WKV_SUB, LANES), F32)],
        compiler_params=_cparams("arbitrary"),
        name="wkv_seq",
    )(w, nb, k, na, rp, v)


def _wkv_step_kernel(s_ref, w_ref, nb_ref, k_ref, na_ref, rp_ref, v_ref, y_ref, sout_ref):
    w, nb, k, na, rp = w_ref[...], nb_ref[...], k_ref[...], na_ref[...], rp_ref[...]
    for i in range(RWKV_HEAD):
        si = s_ref[i]
        sa = jnp.sum(si * na, axis=0, keepdims=True)
        y_ref[i:i + 1, :] = jnp.sum(si * rp, axis=0, keepdims=True)
        sout_ref[i] = si * w + sa * nb + v_ref[i:i + 1, :] * k


def _wkv_step(state, w, nb, k, na, rp, v):
    n = state.shape[-1]
    st = pl.BlockSpec((None, RWKV_HEAD, RWKV_HEAD, n), lambda h: (h, 0, 0, 0))
    vec = pl.BlockSpec((RWKV_HEAD, n), lambda h: (h, 0))
    return pl.pallas_call(
        _wkv_step_kernel,
        out_shape=(jax.ShapeDtypeStruct((RWKV_WIDTH, n), F32), jax.ShapeDtypeStruct(state.shape, F32)),
        grid=(RWKV_HEADS,),
        in_specs=[st] + [vec] * 6,
        out_specs=(vec, st),
        compiler_params=_cparams("parallel"),
        name="wkv_step",
    )(state, w, nb, k, na, rp, v)


N_MERGE_IN = 18


def _merge_tile(refs, yw_channels_first):
    (x_ref, o0_ref, o1_ref, o2_ref, l0_ref, l1_ref, l2_ref, yw_ref, vkr_ref, bonus_ref, g_ref,
     gate_ref, gng_ref, gnb_ref, bd_ref, wa_ref, wr_ref, wo_ref) = refs
    wide = lambda ref: jnp.concatenate([ref[0], ref[1]], axis=1)
    l0, l1, l2 = wide(l0_ref), wide(l1_ref), wide(l2_ref)
    lm = jnp.maximum(jnp.maximum(l0, l1), l2)
    e0, e1, e2 = jnp.exp(l0 - lm), jnp.exp(l1 - lm), jnp.exp(l2 - lm)
    o_att = (e0 * wide(o0_ref) + e1 * wide(o1_ref) + e2 * wide(o2_ref)) / (e0 + e1 + e2)
    y_att = _dot(o_att.astype(BF16), wa_ref[...])

    bd = bd_ref[...]
    y = (yw_ref[...].T if yw_channels_first else yw_ref[...]) + vkr_ref[...]
    mean = _seg_sum(y, bd) * (1.0 / RWKV_HEAD)
    yc = y - mean
    var = _seg_sum(yc * yc, bd) * (1.0 / RWKV_HEAD)
    yn = yc * lax.rsqrt(var + GN_EPS) * gng_ref[...] + gnb_ref[...]
    o_rwkv = (yn + bonus_ref[...]) * g_ref[...]
    y_rwkv = _dot(o_rwkv.astype(BF16), wr_ref[...])

    gate_att = gate_ref[:, 0:D_MODEL].astype(F32)
    gate_rwkv = gate_ref[:, D_MODEL:2 * D_MODEL].astype(F32)
    merged = _sigmoid(gate_att) * y_att + _sigmoid(gate_rwkv) * y_rwkv
    return x_ref[...] + _dot(merged.astype(BF16), wo_ref[...])


def _merge_kernel(*refs, yw_channels_first):
    refs[N_MERGE_IN][...] = _merge_tile(refs[:N_MERGE_IN], yw_channels_first)


def _merge_specs(yw, tm, tile_of):
    row = lambda w: pl.BlockSpec((tm, w), lambda *g: (tile_of(*g), 0))
    if yw.ndim == 3:
        nt = yw.shape[2] // tm
        yw_spec = pl.BlockSpec((None, RWKV_WIDTH, tm), lambda *g: (tile_of(*g) // nt, 0, tile_of(*g) % nt))
    else:
        yw_spec = row(RWKV_WIDTH)
    halves = pl.BlockSpec((GROUP_WIDTH // LANES, tm, LANES), lambda *g: (0, tile_of(*g), 0))
    return ([row(D_MODEL)] + [halves] * 6 + [yw_spec] + [row(RWKV_WIDTH)] * 3 + [row(2 * D_MODEL)]
            + [_resident((1, RWKV_WIDTH)), _resident((1, RWKV_WIDTH)), _resident((SEG_TILE, SEG_TILE)),
               _resident((GROUP_WIDTH, D_MODEL)), _resident((RWKV_WIDTH, D_MODEL)), _resident((D_MODEL, D_MODEL))])


def _merge(x2d, o_groups, lse_groups, yw, vkr, bonus, g, gates, gn_g, gn_b, bd512, wa, wr, wo, tm):
    m = x2d.shape[0]
    return pl.pallas_call(
        functools.partial(_merge_kernel, yw_channels_first=yw.ndim == 3),
        out_shape=jax.ShapeDtypeStruct((m, D_MODEL), F32),
        grid=(m // tm,),
        in_specs=_merge_specs(yw, tm, lambda i: i),
        out_specs=pl.BlockSpec((tm, D_MODEL), lambda i: (i, 0)),
        compiler_params=_cparams("parallel"),
        name="merge",
    )(x2d, *o_groups, *lse_groups, yw, vkr, bonus, g, gates, gn_g, gn_b, bd512, wa, wr, wo)


def _ffn_body(x, g2_ref, wup_ref, cw_ref, cb_ref, wdn_ref, out_ref, u_prev, u_sink):
    ms = jnp.mean(x * x, axis=-1, keepdims=True)
    xn = (x * lax.rsqrt(ms + RMS_EPS) * g2_ref[...]).astype(BF16)

    def up(c):
        return tuple(_dot(xn, wup_ref[:, c0:c0 + FF_CHUNK]) for c0 in (c, D_FF + c))

    def conv(u, c0):
        c1 = c0 + FF_CHUNK
        u_m2, u_m1 = u_prev(u, c0, c1)
        u_sink(u, c0)
        return cb_ref[:, c0:c1] + u_m2 * cw_ref[0:1, c0:c1] + u_m1 * cw_ref[1:2, c0:c1] + u * cw_ref[2:3, c0:c1]

    acc = x
    nxt = up(0)
    for c in range(0, D_FF, FF_CHUNK):
        ug, uv = nxt
        if c + FF_CHUNK < D_FF:
            nxt = up(c + FF_CHUNK)
        gate = conv(ug, c)
        val = conv(uv, D_FF + c)
        h = gate * _sigmoid(gate) * val
        acc = acc + _dot(h.astype(BF16), wdn_ref[c:c + FF_CHUNK, :])
    out_ref[...] = acc


def _ffn_seq_kernel(*refs, cache_group, yw_channels_first):
    g2_ref, wup_ref, cw_ref, cb_ref, wdn_ref = refs[N_MERGE_IN:N_MERGE_IN + 5]
    n_in = N_MERGE_IN + 5 + (N_CACHE_IN if cache_group is not None else 0)
    out_ref, ust_ref = refs[n_in:n_in + 2]
    carry_ref = refs[-1]

    @pl.when(pl.program_id(1) == 0)
    def _():
        carry_ref[...] = jnp.zeros_like(carry_ref)

    if cache_group is not None:
        step = pl.program_id(0) * pl.num_programs(1) + pl.program_id(1)
        _cache_attn_body(refs[N_MERGE_IN + 5:n_in] + refs[n_in + 2:-1], step, cache_group, bb=1)

    x = _merge_tile(refs[:N_MERGE_IN], yw_channels_first)
    tm = x.shape[0]
    row = lax.broadcasted_iota(jnp.int32, (tm, FF_CHUNK), 0)

    def u_prev(u, c0, c1):
        c6 = carry_ref[6:7, c0:c1]
        c7 = carry_ref[7:8, c0:c1]
        u_m1 = jnp.where(row == 0, c7, pltpu.roll(u, 1, 0))
        u_m2 = jnp.where(row == 0, c6, jnp.where(row == 1, c7, pltpu.roll(u, 2, 0)))
        return u_m2, u_m1

    def u_sink(u, c0):
        carry_ref[:, c0:c0 + FF_CHUNK] = u[tm - 8:tm, :]

    _ffn_body(x, g2_ref, wup_ref, cw_ref, cb_ref, wdn_ref, out_ref, u_prev, u_sink)
    ust_ref[...] = carry_ref[...]


def _ffn_step_kernel(x_ref, p0_ref, p1_ref, g2_ref, wup_ref, cw_ref, cb_ref, wdn_ref, out_ref, u_ref):
    def u_prev(u, c0, c1):
        return p0_ref[:, c0:c1], p1_ref[:, c0:c1]

    def u_sink(u, c0):
        u_ref[:, c0:c0 + FF_CHUNK] = u

    _ffn_body(x_ref[...], g2_ref, wup_ref, cw_ref, cb_ref, wdn_ref, out_ref, u_prev, u_sink)


_FFN_W_SPECS = lambda: [_resident((1, D_MODEL)), _resident((D_MODEL, 2 * D_FF)), _resident((3, 2 * D_FF)),
                        _resident((1, 2 * D_FF)), _resident((D_FF, D_MODEL))]


def _merge_ffn_seq(merge_args, ffn_w, batch, seq, tm, cache_rider=None):
    m = merge_args[0].shape[0]
    nt = seq // tm
    tile_of = lambda b, t: b * nt + t
    row = pl.BlockSpec((tm, D_MODEL), lambda b, t: (tile_of(b, t), 0))
    yw = merge_args[7]
    in_specs = _merge_specs(yw, tm, tile_of) + _FFN_W_SPECS()
    out_specs = [row, pl.BlockSpec((None, 8, 2 * D_FF), lambda b, t: (b, 0, 0))]
    out_shape = [jax.ShapeDtypeStruct((m, D_MODEL), F32), jax.ShapeDtypeStruct((batch, 8, 2 * D_FF), F32)]
    cache_group, cache_args = cache_rider if cache_rider is not None else (None, ())
    if cache_rider is not None:
        assert cache_args[5].shape[0] == batch * nt, "one sample sequence per grid step"
        c_in, c_out, c_shape = _cache_attn_specs(cache_args, 1, tile_of)
        in_specs, out_specs, out_shape = in_specs + c_in, out_specs + c_out, out_shape + c_shape
    return pl.pallas_call(
        functools.partial(_ffn_seq_kernel, cache_group=cache_group, yw_channels_first=yw.ndim == 3),
        out_shape=tuple(out_shape),
        grid=(batch, nt),
        in_specs=in_specs,
        out_specs=tuple(out_specs),
        scratch_shapes=[pltpu.VMEM((8, 2 * D_FF), F32)],
        compiler_params=_cparams("arbitrary", "arbitrary"), name="merge_ffn_seq",
    )(*merge_args, *ffn_w, *cache_args)


def _ffn_step(x2d, conv_prev, ln2_g, wup, cw, cb, wdn, tm):
    m = x2d.shape[0]
    w_specs = _FFN_W_SPECS()
    row = lambda w: pl.BlockSpec((tm, w), lambda i: (i, 0))
    return pl.pallas_call(
        _ffn_step_kernel,
        out_shape=(jax.ShapeDtypeStruct((m, D_MODEL), F32), jax.ShapeDtypeStruct((m, 2 * D_FF), F32)),
        grid=(m // tm,),
        in_specs=[row(D_MODEL), row(2 * D_FF), row(2 * D_FF)] + w_specs,
        out_specs=(row(D_MODEL), row(2 * D_FF)),
        compiler_params=_cparams("parallel"), name="ffn_step",
    )(x2d, conv_prev[0], conv_prev[1], ln2_g, wup, cw, cb, wdn)


def _pad_rows(w, before, total):
    return jnp.zeros((total, w.shape[1]), w.dtype).at[before:before + w.shape[0]].set(w)


def kernel(x_prompt, x_sample, cache_k_w128, cache_v_w128, cache_k_w512, cache_v_w512, cache_k_w2048, cache_v_w2048,
           state_rwkv_shift, state_rwkv_wkv, state_ffn_conv, ln1_g, w_in, q_norm_g, k_norm_g, w_attn_out,
           rwkv_mu, rwkv_w0, rwkv_w2, rwkv_a0, rwkv_a2, rwkv_g2, rwkv_k_k, rwkv_k_a, rwkv_r_k, rwkv_gn_g, rwkv_gn_b,
           w_rwkv_out, w_o, ln2_g, w_up, conv_w, conv_b, w_down):
    batch, seq, _ = x_prompt.shape
    nsamp = x_sample.shape[0]
    assert seq % SEG == 0 and x_sample.shape[1] == 1 and batch * RWKV_HEADS * WKV_Q == LANES
    assert ln1_g.shape[0] == 1, "single layer"
    lyr = 0

    w_in_bf = w_in[lyr].astype(BF16)
    bd768 = bd512 = _block_diag_ones(SEG_TILE)
    gq = jnp.tile(q_norm_g[lyr], ATT_WIDTH // HEAD_DIM)[None]
    gk = jnp.tile(k_norm_g[lyr], ATT_WIDTH // HEAD_DIM)[None]
    row1 = lambda a: a.reshape(1, -1)
    prep_w = (row1(rwkv_mu[lyr]), row1(rwkv_w0[lyr]), row1(rwkv_a0[lyr]),
              *_split_bf16(_pad_rows(rwkv_w2[lyr], 0, DECAY_LORA + ICLR_LORA)),
              *_split_bf16(_pad_rows(rwkv_a2[lyr], DECAY_LORA, DECAY_LORA + ICLR_LORA)),
              *_split_bf16(rwkv_g2[lyr]),
              row1(rwkv_k_k[lyr]), row1(rwkv_k_a[lyr]), row1(rwkv_r_k[lyr]), bd512)
    merge_w = (row1(rwkv_gn_g[lyr]), row1(rwkv_gn_b[lyr]), bd512, w_attn_out[lyr].astype(BF16),
               w_rwkv_out[lyr].astype(BF16), w_o[lyr].astype(BF16))
    ffn_w = (row1(ln2_g[lyr]), w_up[lyr].astype(BF16), conv_w[lyr], row1(conv_b[lyr]), w_down[lyr].astype(BF16))

    xs = x_sample.reshape(nsamp, D_MODEL)
    qs, ks, vs, prs, gates_s = _proj(xs, row1(ln1_g[lyr]), w_in_bf, bd768, gq, gk, tm=nsamp)
    caches = [c[lyr].transpose(0, 2, 3, 1).reshape(nsamp, GROUP_WIDTH, c.shape[2])
              for c in (cache_k_w128, cache_v_w128, cache_k_w512, cache_v_w512, cache_k_w2048, cache_v_w2048)]
    for g, (win, dil) in enumerate(ATT_GROUPS):
        assert caches[2 * g].shape[2] == BLOCK * dil, "window caches must be full"
    ks_t, vs_t = ks.T, vs.T
    cache_args = lambda g: (qs, ks, vs, ks_t, vs_t, caches[2 * g], caches[2 * g + 1])
    big, mid = N_GROUPS - 1, N_GROUPS - 2
    cache_done = {}

    xp = x_prompt.reshape(batch * seq, D_MODEL)
    keep_max = min(max(w for w, _ in ATT_GROUPS), seq)
    first_half = _proj(xp, row1(ln1_g[lyr]), w_in_bf, bd768, gq, gk, tm=256, prep=(prep_w, batch, seq, keep_max),
                       cache_rider=(mid, cache_args(mid)))
    qn, kn, v, pr_last, gates = first_half[:5]
    na, dec, nb, kp, rp, rv, vkr, bonus, gg = first_half[5:5 + N_PREP_OUT]
    k_tail, v_tail = first_half[5 + N_PREP_OUT:5 + N_SEQ_OUT]
    cache_done[mid] = first_half[5 + N_SEQ_OUT:]
    att = [_attn_prompt(qn, kn, v, g, batch, seq) for g in range(N_GROUPS)]
    yw, s_fin = _wkv_seq(dec, nb, kp, na, rp, rv)
    merge_args = (xp, *[a[0] for a in att], *[a[1] for a in att], yw, vkr, bonus, gg, gates, *merge_w)
    yp, ust, *cache_done[big] = _merge_ffn_seq(merge_args, ffn_w, batch, seq, tm=seq * batch // nsamp,
                                               cache_rider=(big, cache_args(big)))

    y_prompt = yp.reshape(batch, seq, D_MODEL)
    kv_prompt = []
    for g, (win, _) in enumerate(ATT_GROUPS):
        keep = min(win, seq)
        for tail in (k_tail, v_tail):
            c = tail[:, g * GROUP_WIDTH:(g + 1) * GROUP_WIDTH, keep_max - keep:]
            kv_prompt.append(c.reshape(batch, HEADS_PER_GROUP, HEAD_DIM, keep).transpose(0, 3, 1, 2)[None])
    shift_prompt = pr_last[None, :, -1]
    wkv_prompt = (s_fin.reshape(RWKV_HEAD, WKV_SUB, WKV_Q, batch, RWKV_HEADS)
                  .transpose(3, 4, 2, 1, 0).reshape(1, batch, RWKV_HEADS, RWKV_HEAD, RWKV_HEAD))
    conv_prompt = ust[None, :, 6:8]

    kv_sample, o_s, lse_s = [], [], []
    for g, (win, dil) in enumerate(ATT_GROUPS):
        k_out, v_out, o_g, lse_g = cache_done[g] if g in cache_done else _cache_attn(cache_args(g), g, bb=8)
        kv_sample += [c.reshape(nsamp, HEADS_PER_GROUP, HEAD_DIM, win).transpose(0, 3, 1, 2)[None]
                      for c in (k_out, v_out)]
        o_s.append(o_g)
        lse_s.append(lse_g)
    na, dec, nb, kp, rp, rv, vkr, bonus, gg = _prep_step(prs, state_rwkv_shift[lyr], prep_w, tm=nsamp)
    state_t = state_rwkv_wkv[lyr].transpose(1, 2, 3, 0)
    yw_s, state_new = _wkv_step(state_t, dec, nb, kp, na, rp, rv)
    x2s = _merge(xs, o_s, lse_s, yw_s[None], vkr, bonus, gg, gates_s, *merge_w, tm=nsamp)
    conv_prev = state_ffn_conv[lyr]
    ys, u_s = _ffn_step(x2s, (conv_prev[:, 0], conv_prev[:, 1]), *ffn_w, tm=nsamp)

    y_sample = ys.reshape(nsamp, 1, D_MODEL)
    shift_sample = prs[None]
    wkv_sample = state_new.transpose(3, 0, 1, 2)[None]
    conv_sample = jnp.stack([conv_prev[:, 1], u_s], axis=1)[None]

    return (y_prompt, y_sample,
            kv_prompt[0], kv_sample[0], kv_prompt[1], kv_sample[1],
            kv_prompt[2], kv_sample[2], kv_prompt[3], kv_sample[3],
            kv_prompt[4], kv_sample[4], kv_prompt[5], kv_sample[5],
            shift_prompt, shift_sample, wkv_prompt, wkv_sample, conv_prompt, conv_sample)
```

```python
import functools

import jax
import jax.numpy as jnp
from jax import lax
from jax.experimental import pallas as pl
from jax.experimental.pallas import tpu as pltpu

F32 = jnp.float32
BF16 = jnp.bfloat16

D_MODEL = 1024
HEAD_DIM = 64
ATT_GROUPS = ((128, 1), (512, 4), (2048, 16))
N_GROUPS = len(ATT_GROUPS)
HEADS_PER_GROUP = 4
GROUP_WIDTH = HEADS_PER_GROUP * HEAD_DIM
ATT_WIDTH = N_GROUPS * GROUP_WIDTH
BLOCK = 128
SEG = 2048
RWKV_HEADS = 8
RWKV_HEAD = 64
RWKV_WIDTH = RWKV_HEADS * RWKV_HEAD
DECAY_LORA = 64
ICLR_LORA = 64
GATE_LORA = 128
RWKV_PROJ = 3 * RWKV_WIDTH + DECAY_LORA + ICLR_LORA + GATE_LORA
LORA_OFF = 3 * RWKV_WIDTH
GATE_OFF = LORA_OFF + DECAY_LORA + ICLR_LORA
PR_OFF = 3 * ATT_WIDTH
GATES_OFF = PR_OFF + RWKV_PROJ
P_TOTAL = GATES_OFF + 2 * D_MODEL
D_FF = 2816
FF_CHUNK = 256
RMS_EPS = 1e-6
GN_EPS = 64e-5
L2_EPS = 1e-12
NEG_BIG = -1e30

LANES = 128
SEG_TILE = 256
WKV_Q = 4
WKV_SUB = RWKV_HEAD // WKV_Q
WKV_TB = 128
WKV_GROUP = 8
VMEM_LIMIT = 56 * 1024 * 1024


def _cparams(*sem):
    return pltpu.CompilerParams(dimension_semantics=sem, vmem_limit_bytes=VMEM_LIMIT)


def _resident(shape):
    n = len(shape)
    return pl.BlockSpec(shape, lambda *_: (0,) * n, pipeline_mode=pl.Buffered(1))


def _split_bf16(x):
    hi = x.astype(BF16)
    lo = (x - hi.astype(F32)).astype(BF16)
    return hi, lo


def _dot(a, b):
    return jnp.dot(a, b, preferred_element_type=F32)


def _dot3(a, b_hi, b_lo):
    a_hi, a_lo = _split_bf16(a)
    return _dot(a_hi, b_hi) + _dot(a_lo, b_hi) + _dot(a_hi, b_lo)


def _seg_sum_bf16(x_bf, ones_bd):
    return jnp.concatenate([_dot(x_bf[:, c:c + SEG_TILE], ones_bd) for c in range(0, x_bf.shape[1], SEG_TILE)],
                           axis=1)


def _seg_sum(x, ones_bd):
    hi, lo = _split_bf16(x)
    return _seg_sum_bf16(hi, ones_bd) + _seg_sum_bf16(lo, ones_bd)


def _sigmoid(x):
    return 1.0 / (1.0 + jnp.exp(-x))


def _block_diag_ones(width):
    idx = jnp.arange(width) // HEAD_DIM
    return (idx[:, None] == idx[None, :]).astype(BF16)


N_PREP_W, N_PREP_OUT = 13, 9


N_SEQ_OUT = N_PREP_OUT + 2


def _proj_kernel(*refs, cache_group, cache_bb, tiles_per_seq):
    with_prep = tiles_per_seq is not None
    x_ref, g1_ref, w_ref, bd_ref, gq_ref, gk_ref = refs[:6]
    n_w = 6 + (N_PREP_W if with_prep else 0)
    n_in = n_w + (N_CACHE_IN if cache_group is not None else 0)
    q_ref, k_ref, v_ref, pr_ref, gate_ref = refs[n_in:n_in + 5]
    n_out = n_in + 5 + (N_SEQ_OUT if with_prep else 0)
    if with_prep:
        carry_ref = refs[-1]

        @pl.when(pl.program_id(0) % tiles_per_seq == 0)
        def _():
            carry_ref[...] = jnp.zeros_like(carry_ref)

    if cache_group is not None:
        cache_refs = refs[n_w:n_in] + refs[n_out:n_out + N_CACHE_OUT]
        _cache_attn_body(cache_refs, pl.program_id(0) * cache_bb, cache_group, cache_bb)
    x = x_ref[...]
    ms = jnp.mean(x * x, axis=-1, keepdims=True)
    xn = (x * lax.rsqrt(ms + RMS_EPS) * g1_ref[...]).astype(BF16)

    def proj(c0, c1):
        return _dot(xn, w_ref[:, c0:c1])

    def head_norm(z, g):
        ss = _seg_sum_bf16((z * z).astype(BF16), bd_ref[...])
        return z * lax.rsqrt(ss * (1.0 / HEAD_DIM) + RMS_EPS) * g

    q_ref[...] = head_norm(proj(0, ATT_WIDTH), gq_ref[...])
    k = head_norm(proj(ATT_WIDTH, 2 * ATT_WIDTH), gk_ref[...])
    v = proj(2 * ATT_WIDTH, PR_OFF)
    k_ref[...] = k
    v_ref[...] = v
    gate_ref[...] = proj(GATES_OFF, P_TOTAL).astype(gate_ref.dtype)
    pr = proj(PR_OFF, GATES_OFF)
    if not with_prep:
        pr_ref[...] = pr
        return
    kt_ref, vt_ref = refs[n_out - 2:n_out]
    kt_ref[...] = k.T
    vt_ref[...] = v.T
    tm = pr.shape[0]
    row = lax.broadcasted_iota(jnp.int32, pr.shape, 0)
    prev = jnp.where(row == 0, carry_ref[7:8, :], pltpu.roll(pr, 1, 0))
    carry_ref[...] = pr[tm - 8:tm, :]
    pr_ref[...] = pr[tm - 8:tm, :]
    _prep_math(pr, prev, *refs[6:n_w], refs[n_in + 5:n_in + 5 + N_PREP_OUT])


def _proj(x2d, ln1_g, w_in_bf, bd768, gq, gk, tm, prep=None, cache_rider=None):
    m = x2d.shape[0]
    row = lambda w: pl.BlockSpec((tm, w), lambda i: (i, 0))
    widths = (ATT_WIDTH, ATT_WIDTH, ATT_WIDTH, RWKV_PROJ, 2 * D_MODEL)
    in_specs = [row(D_MODEL), _resident((1, D_MODEL)), _resident((D_MODEL, P_TOTAL)),
                _resident((SEG_TILE, SEG_TILE)), _resident((1, ATT_WIDTH)), _resident((1, ATT_WIDTH))]
    out_specs = [row(w) for w in widths]
    out_shape = [jax.ShapeDtypeStruct((m, w), BF16 if i == len(widths) - 1 else F32) for i, w in enumerate(widths)]
    prep_w, tiles_per_seq, scratch = (), None, []
    if prep is not None:
        prep_w, batch, seq, keep = prep
        assert keep % tm == 0 and seq % tm == 0
        tiles_per_seq = seq // tm
        in_specs += [_resident(w.shape) for w in prep_w]
        out_specs[3] = pl.BlockSpec((None, 8, RWKV_PROJ), lambda i: (i // tiles_per_seq, 0, 0))
        out_shape[3] = jax.ShapeDtypeStruct((batch, 8, RWKV_PROJ), F32)
        chan = pl.BlockSpec((None, RWKV_WIDTH, tm), lambda i: (i // tiles_per_seq, 0, i % tiles_per_seq))
        out_specs += [chan] * 6 + [row(RWKV_WIDTH)] * 3
        out_shape += [jax.ShapeDtypeStruct((batch, RWKV_WIDTH, seq), F32)] * 6
        out_shape += [jax.ShapeDtypeStruct((m, RWKV_WIDTH), F32)] * 3
        first = (seq - keep) // tm
        tail = pl.BlockSpec((None, ATT_WIDTH, tm),
                            lambda i: (i // tiles_per_seq, 0, jnp.maximum(i % tiles_per_seq - first, 0)))
        out_specs += [tail] * 2
        out_shape += [jax.ShapeDtypeStruct((batch, ATT_WIDTH, keep), F32)] * 2
        scratch = [pltpu.VMEM((8, RWKV_PROJ), F32)]
    cache_group, cache_args, cache_bb = None, (), 0
    if cache_rider is not None:
        cache_group, cache_args = cache_rider
        cache_bb = cache_args[5].shape[0] * tm // m
        assert cache_bb * (m // tm) == cache_args[5].shape[0]
        c_in, c_out, c_shape = _cache_attn_specs(cache_args, cache_bb, lambda i: i)
        in_specs, out_specs, out_shape = in_specs + c_in, out_specs + c_out, out_shape + c_shape
    return pl.pallas_call(
        functools.partial(_proj_kernel, cache_group=cache_group, cache_bb=cache_bb, tiles_per_seq=tiles_per_seq),
        out_shape=tuple(out_shape),
        grid=(m // tm,),
        in_specs=in_specs,
        out_specs=tuple(out_specs),
        scratch_shapes=scratch,
        compiler_params=_cparams("arbitrary"),
        name="proj",
    )(x2d, ln1_g, w_in_bf, bd768, gq, gk, *prep_w, *cache_args)


def _attn_prompt_kernel(*refs, dil):
    q_ref, kp_ref, kc_ref, vp_ref, vc_ref = (refs[2 * i:2 * i + 2] for i in range(5))
    o_ref, lse_ref = refs[10], refs[11]
    not_first_seg = pl.program_id(1) > 0
    nblk = SEG // (BLOCK * dil)
    stacked = (HEADS_PER_GROUP * BLOCK, 2 * BLOCK)
    row = lax.broadcasted_iota(jnp.int32, stacked, 0) % BLOCK
    col = lax.broadcasted_iota(jnp.int32, stacked, 1)
    in_prev = col < BLOCK
    band_prev = in_prev & (col >= row)
    band_cur = jnp.logical_not(in_prev) & (col - BLOCK <= row)
    bias_cur = jnp.where(band_cur, 0.0, NEG_BIG)
    bias_both = jnp.where(band_cur | band_prev, 0.0, NEG_BIG)
    lane = lax.broadcasted_iota(jnp.int32, (1, GROUP_WIDTH), 1)
    head_masks = [(lane // HEAD_DIM) == h for h in range(HEADS_PER_GROUP)]

    def row_slice(start):
        return pl.ds(start, BLOCK) if dil == 1 else pl.ds(start, BLOCK, stride=dil)

    def rows(halves, start):
        return jnp.concatenate([h[row_slice(start), :] for h in halves], axis=1)

    def block(q_start, k1_ref, k1_start, v1_ref, prev_valid):
        q = rows(q_ref, q_start) * (HEAD_DIM ** -0.5)
        q4 = jnp.concatenate([jnp.where(hm, q, 0.0) for hm in head_masks], axis=0).astype(BF16)
        kt = jnp.concatenate([rows(k1_ref, k1_start), rows(kc_ref, q_start)], axis=0).astype(BF16)
        vt = jnp.concatenate([rows(v1_ref, k1_start), rows(vc_ref, q_start)], axis=0).astype(BF16)
        s = lax.dot_general(q4, kt, (((1,), (1,)), ((), ())), preferred_element_type=F32)
        s = s + (bias_both if prev_valid is True else jnp.where(prev_valid, bias_both, bias_cur))
        m = jnp.max(s, axis=-1, keepdims=True)
        p = jnp.exp(s - m)
        l = jnp.sum(p, axis=-1, keepdims=True)
        ov = _dot(p.astype(BF16), vt)
        lse = m + jnp.log(l)
        o = jnp.zeros((BLOCK, GROUP_WIDTH), F32)
        l_b = jnp.ones((BLOCK, GROUP_WIDTH), F32)
        lse_b = jnp.zeros((BLOCK, GROUP_WIDTH), F32)
        for h, hm in enumerate(head_masks):
            sl = slice(h * BLOCK, (h + 1) * BLOCK)
            o = jnp.where(hm, ov[sl], o)
            l_b = jnp.where(hm, l[sl], l_b)
            lse_b = jnp.where(hm, lse[sl], lse_b)
        o = o / l_b
        for half in range(2):
            lanes = slice(half * LANES, (half + 1) * LANES)
            o_ref[half, row_slice(q_start), :] = o[:, lanes]
            lse_ref[half, row_slice(q_start), :] = lse_b[:, lanes]

    def first_blocks(r, carry):
        block(r, kp_ref, SEG - BLOCK * dil + r, vp_ref, not_first_seg)
        return carry

    def later_blocks(i, carry):
        r = i % dil
        nb = 1 + i // dil
        q_start = nb * BLOCK * dil + r
        block(q_start, kc_ref, q_start - BLOCK * dil, vc_ref, True)
        return carry

    lax.fori_loop(0, dil, first_blocks, 0, unroll=min(2, dil))
    if nblk > 1:
        lax.fori_loop(0, dil * (nblk - 1), later_blocks, 0, unroll=2)


def _attn_prompt(qn, kn, v, group, batch, seq):
    dil = ATT_GROUPS[group][1]
    nseg = seq // SEG
    halves = GROUP_WIDTH // LANES
    cur = [pl.BlockSpec((SEG, LANES), lambda b, s, c=halves * group + h: (b * nseg + s, c)) for h in range(halves)]
    prev = [pl.BlockSpec((SEG, LANES), lambda b, s, c=halves * group + h: (b * nseg + jnp.maximum(s - 1, 0), c))
            for h in range(halves)]
    out = pl.BlockSpec((halves, SEG, LANES), lambda b, s: (0, b * nseg + s, 0))
    shape = jax.ShapeDtypeStruct((halves, batch * seq, LANES), F32)
    return pl.pallas_call(
        functools.partial(_attn_prompt_kernel, dil=dil),
        out_shape=(shape, shape),
        grid=(batch, nseg),
        in_specs=cur + prev + cur + prev + cur,
        out_specs=(out, out),
        compiler_params=_cparams("parallel", "arbitrary"),
        name=f"attn_prompt_g{group}",
    )(qn, qn, kn, kn, kn, kn, v, v, v, v)


def _cache_attn_body(refs, first, group, bb):
    q_ref, kn_ref, vn_ref, knt_ref, vnt_ref, kc_ref, vc_ref, ko_ref, vo_ref, o_ref, lse_ref = refs
    dil = ATT_GROUPS[group][1]
    win = kc_ref.shape[2]
    nseq = q_ref.shape[0]
    cols = slice(group * GROUP_WIDTH, (group + 1) * GROUP_WIDTH)
    pos = lax.broadcasted_iota(jnp.int32, (1, win), 1)
    in_window = (pos % dil) == 0
    newest = lax.broadcasted_iota(jnp.int32, (GROUP_WIDTH, LANES), 1) == LANES - 1
    head_rows = (lax.broadcasted_iota(jnp.int32, (8, GROUP_WIDTH), 0)
                 == lax.broadcasted_iota(jnp.int32, (8, GROUP_WIDTH), 1) // HEAD_DIM)
    seq_lane = lax.broadcasted_iota(jnp.int32, (GROUP_WIDTH, nseq), 1)

    def hi_lo_rows(x):
        hi, lo = _split_bf16(x)
        return jnp.concatenate([hi, lo], axis=0)

    def fold(x):
        return x[0:8] + x[8:16]

    def shifted(cache, new_t_ref, n):
        new_col = jnp.sum(jnp.where(seq_lane == n, new_t_ref[cols, :], 0.0), axis=1, keepdims=True)
        rolled = pltpu.roll(cache, win - 1, 1)
        last_tile = jnp.where(newest, new_col, rolled[:, win - LANES:])
        return last_tile if win == LANES else jnp.concatenate([rolled[:, :win - LANES], last_tile], axis=1)

    for b in range(bb):
        n = first + b
        q = q_ref[pl.ds(n, 1), cols] * (HEAD_DIM ** -0.5)
        k_new = kn_ref[pl.ds(n, 1), cols]
        v_new = vn_ref[pl.ds(n, 1), cols]
        q8 = jnp.where(head_rows, q, 0.0)
        kc = kc_ref[b]
        vc = vc_ref[b]
        s = fold(_dot(hi_lo_rows(q8), kc.astype(BF16)))
        s = jnp.where(in_window, s, NEG_BIG)
        s_new = jnp.sum(q8 * k_new, axis=1, keepdims=True)
        m = jnp.maximum(jnp.max(s, axis=1, keepdims=True), s_new)
        p = jnp.exp(s - m)
        p_new = jnp.exp(s_new - m)
        l = jnp.sum(p, axis=1, keepdims=True) + p_new
        pv = fold(lax.dot_general(hi_lo_rows(p), vc.astype(BF16), (((1,), (1,)), ((), ())),
                                  preferred_element_type=F32))
        ov = (pv + p_new * v_new) / l
        o = jnp.sum(jnp.where(head_rows, ov, 0.0), axis=0, keepdims=True)
        lse = jnp.sum(jnp.where(head_rows, m + jnp.log(l), 0.0), axis=0, keepdims=True)
        for half in range(GROUP_WIDTH // LANES):
            lanes = slice(half * LANES, (half + 1) * LANES)
            o_ref[half, pl.ds(n, 1), :] = o[:, lanes]
            lse_ref[half, pl.ds(n, 1), :] = lse[:, lanes]
        ko_ref[b] = shifted(kc, knt_ref, n)
        vo_ref[b] = shifted(vc, vnt_ref, n)


N_CACHE_IN, N_CACHE_OUT = 7, 4


def _cache_attn_specs(args, bb, block_of):
    n, _, win = args[5].shape
    halves = GROUP_WIDTH // LANES
    cache = pl.BlockSpec((bb, GROUP_WIDTH, win), lambda *g: (block_of(*g), 0, 0))
    rows = _resident((n, ATT_WIDTH))
    rows_t = _resident((ATT_WIDTH, n))
    out = pl.BlockSpec((halves, n, LANES), lambda *g: (0, 0, 0))
    cache_shape = jax.ShapeDtypeStruct(args[5].shape, F32)
    o_shape = jax.ShapeDtypeStruct((halves, n, LANES), F32)
    return [rows] * 3 + [rows_t] * 2 + [cache] * 2, [cache, cache, out, out], [cache_shape] * 2 + [o_shape] * 2


def _cache_attn_kernel(*refs, group, bb):
    _cache_attn_body(refs, pl.program_id(0) * bb, group, bb)


def _cache_attn(args, group, bb):
    in_specs, out_specs, out_shape = _cache_attn_specs(args, bb, lambda i: i)
    return pl.pallas_call(
        functools.partial(_cache_attn_kernel, group=group, bb=bb),
        out_shape=tuple(out_shape),
        grid=(args[5].shape[0] // bb,),
        in_specs=in_specs,
        out_specs=tuple(out_specs),
        compiler_params=_cparams("arbitrary"),
        name=f"cache_attn_g{group}",
    )(*args)


def _prep_math(p, prev, mu_ref, w0_ref, a0_ref, w2h_ref, w2l_ref, a2h_ref, a2l_ref, g2h_ref, g2l_ref,
               kk_ref, ka_ref, rk_ref, bd_ref, outs):
    na_ref, dec_ref, nb_ref, kp_ref, rp_ref, v_ref, vkr_ref, bonus_ref, g_ref = outs
    bd = bd_ref[...]

    def put(ref, val):
        ref[...] = val.T

    xs = p + mu_ref[...] * (prev - p)
    r = xs[:, 0:RWKV_WIDTH]
    k = xs[:, RWKV_WIDTH:2 * RWKV_WIDTH]
    v = xs[:, 2 * RWKV_WIDTH:3 * RWKV_WIDTH]
    lora = xs[:, LORA_OFF:GATE_OFF]
    gl = xs[:, GATE_OFF:RWKV_PROJ]
    z = w0_ref[...] + _dot3(jnp.tanh(lora), w2h_ref[...], w2l_ref[...])
    softplus_neg = jnp.maximum(-z, 0.0) + jnp.log(1.0 + jnp.exp(-jnp.abs(z)))
    w = -softplus_neg - 0.5
    dec = jnp.exp(-jnp.exp(w))
    a = _sigmoid(a0_ref[...] + _dot3(lora, a2h_ref[...], a2l_ref[...]))
    g = _dot3(_sigmoid(gl), g2h_ref[...], g2l_ref[...])
    kk = k * kk_ref[...]
    kk = kk / jnp.maximum(jnp.sqrt(_seg_sum(kk * kk, bd)), L2_EPS)
    kp = k * (1.0 + (a - 1.0) * ka_ref[...])
    na = -kk
    nb = kk * a
    rp = dec * r + na * _seg_sum(nb * r, bd)
    put(na_ref, na)
    put(dec_ref, dec)
    put(nb_ref, nb)
    put(kp_ref, kp)
    put(rp_ref, rp)
    put(v_ref, v)
    vkr_ref[...] = v * _seg_sum(kp * r, bd)
    bonus_ref[...] = _seg_sum(r * kp * rk_ref[...], bd) * v
    g_ref[...] = g


def _prep_step_kernel(p_ref, prev_ref, *rest):
    weights, outs = rest[:N_PREP_W], rest[N_PREP_W:N_PREP_W + N_PREP_OUT]
    _prep_math(p_ref[...], prev_ref[...], *weights, outs)


def _prep_step(pr, shift_prev, weights, tm):
    m = pr.shape[0]
    w_specs = [_resident(w.shape) for w in weights]
    out_shape = tuple(jax.ShapeDtypeStruct((m, RWKV_WIDTH), F32) for _ in range(9))
    row = lambda w: pl.BlockSpec((tm, w), lambda i: (i, 0))
    chan = pl.BlockSpec((RWKV_WIDTH, tm), lambda i: (0, i))
    chan_shape = jax.ShapeDtypeStruct((RWKV_WIDTH, m), F32)
    return pl.pallas_call(
        _prep_step_kernel, out_shape=(chan_shape,) * 6 + out_shape[6:], grid=(m // tm,),
        in_specs=[row(RWKV_PROJ), row(RWKV_PROJ)] + w_specs,
        out_specs=(chan,) * 6 + tuple(row(RWKV_WIDTH) for _ in range(3)),
        compiler_params=_cparams("parallel"), name="rwkv_prep_step",
    )(pr, shift_prev, *weights)


def _wkv_seq_kernel(w_ref, nb_ref, k_ref, na_ref, rp_ref, v_ref, y_ref, sout_ref,
                    s_ref, e_ref, first_ref, stage_ref, vt_ref, yt_ref):
    @pl.when(pl.program_id(0) == 0)
    def _():
        s_ref[...] = jnp.zeros_like(s_ref)

    nbatch = w_ref.shape[0]
    tb = w_ref.shape[2]
    zero = jnp.zeros((WKV_SUB, LANES), F32)

    def head_rows(ref, b, c):
        return ref[b, pl.ds(c, RWKV_HEADS, stride=RWKV_HEAD), :]

    def time_major(ref, j):
        rows = jnp.concatenate([head_rows(ref, b, j) for b in range(nbatch)], axis=0)
        return jnp.concatenate([rows] * WKV_Q, axis=0).T

    unroll = 8

    def build_e(jg, carry):
        for j in [jg * unroll + u for u in range(unroll)]:
            for o, ref in enumerate((w_ref, nb_ref, k_ref)):
                e_ref[o, j] = time_major(ref, j)
            for o, ref in ((3, na_ref), (4, rp_ref)):
                rows = time_major(ref, j)
                first_ref[o - 3, j] = rows[0:WKV_GROUP]
                e_ref[o, j] = pltpu.roll(rows, tb - 1, 0)
        return carry

    def build_v(ig, carry):
        for ih in [ig * unroll + u for u in range(unroll)]:
            rows = jnp.concatenate([head_rows(v_ref, b, il * WKV_SUB + ih)
                                    for il in range(WKV_Q) for b in range(nbatch)], axis=0)
            vt_ref[pl.ds(ih, tb, stride=WKV_SUB), :] = rows.T
        return carry

    lax.fori_loop(0, RWKV_HEAD // unroll, build_e, 0)
    lax.fori_loop(0, WKV_SUB // unroll, build_v, 0)

    n_ops = 5
    pairs = [(o, j) for o in range(n_ops) for j in range(RWKV_HEAD)]
    per_step = len(pairs) // WKV_GROUP

    def stage_tiles(slot, group, which):
        rows = pl.ds(pl.multiple_of(jnp.minimum(group, tb // WKV_GROUP - 1) * WKV_GROUP, WKV_GROUP), WKV_GROUP)
        for o, j in which:
            stage_ref[slot, o, j] = e_ref[o, j, rows, :]

    def step(group, u, slot, carry):
        sa, y = carry
        t = group * WKV_GROUP + u
        tile = pl.ds(pl.multiple_of(t * WKV_SUB, WKV_SUB), WKV_SUB)
        yt_ref[tile, :] = y
        v = vt_ref[tile, :]
        stage_tiles(1 - slot, group + 1, pairs[u * per_step:(u + 1) * per_step])
        n_acc = 4
        sa_acc = [zero] * n_acc
        y_acc = [zero] * n_acc
        for j in range(RWKV_HEAD):
            row = lambda o: stage_ref[slot, o, j, u:u + 1, :]
            sn = s_ref[j] * row(0) + sa * row(1) + v * row(2)
            s_ref[j] = sn
            sa_acc[j % n_acc] = sa_acc[j % n_acc] + sn * row(3)
            y_acc[j % n_acc] = y_acc[j % n_acc] + sn * row(4)
        return (sa_acc[0] + sa_acc[1]) + (sa_acc[2] + sa_acc[3]), (y_acc[0] + y_acc[1]) + (y_acc[2] + y_acc[3])

    region = 1

    def two_groups(gg, carry):
        for slot in range(2):
            for u0 in range(0, WKV_GROUP, region):
                def do(c, slot=slot, u0=u0):
                    for u in range(u0, u0 + region):
                        c = step(2 * gg + slot, u, slot, c)
                    return c
                carry = do(carry) if (slot, u0) == (0, 0) else lax.cond(gg >= 0, do, lambda c: c, carry)
        return carry

    def first_matvecs(j, carry):
        sa, y = carry
        sj = s_ref[j]
        return sa + sj * first_ref[0, j, 0:1, :], y + sj * first_ref[1, j, 0:1, :]

    stage_tiles(0, 0, pairs)
    lax.fori_loop(0, tb // (2 * WKV_GROUP), two_groups, lax.fori_loop(0, RWKV_HEAD, first_matvecs, (zero, zero)))

    def emit_y(ig, carry):
        for ih in [ig * unroll + u for u in range(unroll)]:
            cols = yt_ref[pl.ds(ih, tb, stride=WKV_SUB), :].T
            for il in range(WKV_Q):
                for b in range(nbatch):
                    r0 = il * (LANES // WKV_Q) + b * RWKV_HEADS
                    y_ref[b, pl.ds(il * WKV_SUB + ih, RWKV_HEADS, stride=RWKV_HEAD), :] = cols[r0:r0 + RWKV_HEADS]
        return carry

    lax.fori_loop(0, WKV_SUB // unroll, emit_y, 0)

    @pl.when(pl.program_id(0) == pl.num_programs(0) - 1)
    def _():
        sout_ref[...] = s_ref[...]


def _wkv_seq(w, nb, k, na, rp, v):
    batch, _, t = w.shape
    blk = pl.BlockSpec((batch, RWKV_WIDTH, WKV_TB), lambda i: (0, 0, i))
    state = pl.BlockSpec((RWKV_HEAD, WKV_SUB, LANES), lambda i: (0, 0, 0))
    return pl.pallas_call(
        _wkv_seq_kernel,
        out_shape=(jax.ShapeDtypeStruct((batch, RWKV_WIDTH, t), F32),
                   jax.ShapeDtypeStruct((RWKV_HEAD, WKV_SUB, LANES), F32)),
        grid=(t // WKV_TB,),
        in_specs=[blk] * 6,
        out_specs=(blk, state),
        scratch_shapes=[pltpu.VMEM((RWKV_HEAD, WKV_SUB, LANES), F32),
                        pltpu.VMEM((5, RWKV_HEAD, WKV_TB, LANES), F32),
                        pltpu.VMEM((2, RWKV_HEAD, WKV_GROUP, LANES), F32),
                        pltpu.VMEM((2, 5, RWKV_HEAD, WKV_GROUP, LANES), F32),
                        pltpu.VMEM((WKV_TB * WKV_SUB, LANES), F32),
                        pltpu.VMEM((WKV_TB * WKV_SUB, LANES), F32)],
        compiler_params=_cparams("arbitrary"),
        name="wkv_seq",
    )(w, nb, k, na, rp, v)


def _wkv_step_kernel(s_ref, w_ref, nb_ref, k_ref, na_ref, rp_ref, v_ref, y_ref, sout_ref):
    w, nb, k, na, rp = w_ref[...], nb_ref[...], k_ref[...], na_ref[...], rp_ref[...]
    for i in range(RWKV_HEAD):
        si = s_ref[i]
        sa = jnp.sum(si * na, axis=0, keepdims=True)
        y_ref[i:i + 1, :] = jnp.sum(si * rp, axis=0, keepdims=True)
        sout_ref[i] = si * w + sa * nb + v_ref[i:i + 1, :] * k


def _wkv_step(state, w, nb, k, na, rp, v):
    n = state.shape[-1]
    st = pl.BlockSpec((None, RWKV_HEAD, RWKV_HEAD, n), lambda h: (h, 0, 0, 0))
    vec = pl.BlockSpec((RWKV_HEAD, n), lambda h: (h, 0))
    return pl.pallas_call(
        _wkv_step_kernel,
        out_shape=(jax.ShapeDtypeStruct((RWKV_WIDTH, n), F32), jax.ShapeDtypeStruct(state.shape, F32)),
        grid=(RWKV_HEADS,),
        in_specs=[st] + [vec] * 6,
        out_specs=(vec, st),
        compiler_params=_cparams("parallel"),
        name="wkv_step",
    )(state, w, nb, k, na, rp, v)


N_MERGE_IN = 18


def _merge_tile(refs, yw_channels_first):
    (x_ref, o0_ref, o1_ref, o2_ref, l0_ref, l1_ref, l2_ref, yw_ref, vkr_ref, bonus_ref, g_ref,
     gate_ref, gng_ref, gnb_ref, bd_ref, wa_ref, wr_ref, wo_ref) = refs
    wide = lambda ref: jnp.concatenate([ref[0], ref[1]], axis=1)
    l0, l1, l2 = wide(l0_ref), wide(l1_ref), wide(l2_ref)
    lm = jnp.maximum(jnp.maximum(l0, l1), l2)
    e0, e1, e2 = jnp.exp(l0 - lm), jnp.exp(l1 - lm), jnp.exp(l2 - lm)
    o_att = (e0 * wide(o0_ref) + e1 * wide(o1_ref) + e2 * wide(o2_ref)) / (e0 + e1 + e2)
    y_att = _dot(o_att.astype(BF16), wa_ref[...])

    bd = bd_ref[...]
    y = (yw_ref[...].T if yw_channels_first else yw_ref[...]) + vkr_ref[...]
    mean = _seg_sum(y, bd) * (1.0 / RWKV_HEAD)
    yc = y - mean
    var = _seg_sum(yc * yc, bd) * (1.0 / RWKV_HEAD)
    yn = yc * lax.rsqrt(var + GN_EPS) * gng_ref[...] + gnb_ref[...]
    o_rwkv = (yn + bonus_ref[...]) * g_ref[...]
    y_rwkv = _dot(o_rwkv.astype(BF16), wr_ref[...])

    gate_att = gate_ref[:, 0:D_MODEL].astype(F32)
    gate_rwkv = gate_ref[:, D_MODEL:2 * D_MODEL].astype(F32)
    merged = _sigmoid(gate_att) * y_att + _sigmoid(gate_rwkv) * y_rwkv
    return x_ref[...] + _dot(merged.astype(BF16), wo_ref[...])


def _merge_kernel(*refs, yw_channels_first):
    refs[N_MERGE_IN][...] = _merge_tile(refs[:N_MERGE_IN], yw_channels_first)


def _merge_specs(yw, tm, tile_of):
    row = lambda w: pl.BlockSpec((tm, w), lambda *g: (tile_of(*g), 0))
    if yw.ndim == 3:
        nt = yw.shape[2] // tm
        yw_spec = pl.BlockSpec((None, RWKV_WIDTH, tm), lambda *g: (tile_of(*g) // nt, 0, tile_of(*g) % nt))
    else:
        yw_spec = row(RWKV_WIDTH)
    halves = pl.BlockSpec((GROUP_WIDTH // LANES, tm, LANES), lambda *g: (0, tile_of(*g), 0))
    return ([row(D_MODEL)] + [halves] * 6 + [yw_spec] + [row(RWKV_WIDTH)] * 3 + [row(2 * D_MODEL)]
            + [_resident((1, RWKV_WIDTH)), _resident((1, RWKV_WIDTH)), _resident((SEG_TILE, SEG_TILE)),
               _resident((GROUP_WIDTH, D_MODEL)), _resident((RWKV_WIDTH, D_MODEL)), _resident((D_MODEL, D_MODEL))])


def _merge(x2d, o_groups, lse_groups, yw, vkr, bonus, g, gates, gn_g, gn_b, bd512, wa, wr, wo, tm):
    m = x2d.shape[0]
    return pl.pallas_call(
        functools.partial(_merge_kernel, yw_channels_first=yw.ndim == 3),
        out_shape=jax.ShapeDtypeStruct((m, D_MODEL), F32),
        grid=(m // tm,),
        in_specs=_merge_specs(yw, tm, lambda i: i),
        out_specs=pl.BlockSpec((tm, D_MODEL), lambda i: (i, 0)),
        compiler_params=_cparams("parallel"),
        name="merge",
    )(x2d, *o_groups, *lse_groups, yw, vkr, bonus, g, gates, gn_g, gn_b, bd512, wa, wr, wo)


def _ffn_body(x, g2_ref, wup_ref, cw_ref, cb_ref, wdn_ref, out_ref, u_prev, u_sink):
    ms = jnp.mean(x * x, axis=-1, keepdims=True)
    xn = (x * lax.rsqrt(ms + RMS_EPS) * g2_ref[...]).astype(BF16)

    def up(c):
        return tuple(_dot(xn, wup_ref[:, c0:c0 + FF_CHUNK]) for c0 in (c, D_FF + c))

    def conv(u, c0):
        c1 = c0 + FF_CHUNK
        u_m2, u_m1 = u_prev(u, c0, c1)
        u_sink(u, c0)
        return cb_ref[:, c0:c1] + u_m2 * cw_ref[0:1, c0:c1] + u_m1 * cw_ref[1:2, c0:c1] + u * cw_ref[2:3, c0:c1]

    acc = x
    nxt = up(0)
    for c in range(0, D_FF, FF_CHUNK):
        ug, uv = nxt
        if c + FF_CHUNK < D_FF:
            nxt = up(c + FF_CHUNK)
        gate = conv(ug, c)
        val = conv(uv, D_FF + c)
        h = gate * _sigmoid(gate) * val
        acc = acc + _dot(h.astype(BF16), wdn_ref[c:c + FF_CHUNK, :])
    out_ref[...] = acc


def _ffn_seq_kernel(*refs, cache_group, yw_channels_first, tiles_per_seq, n_tiles):
    g2_ref, wup_ref, cw_ref, cb_ref, wdn_ref = refs[N_MERGE_IN:N_MERGE_IN + 5]
    n_in = N_MERGE_IN + 5 + (N_CACHE_IN if cache_group is not None else 0)
    out_ref, ust_ref = refs[n_in:n_in + 2]
    carry_ref, x_ref = refs[-2:]
    step = pl.program_id(0)
    ffn_tile = jnp.maximum(step - 1, 0)

    @pl.when(step == 0)
    def _():
        x_ref[...] = jnp.zeros_like(x_ref)

    @pl.when(ffn_tile % tiles_per_seq == 0)
    def _():
        carry_ref[...] = jnp.zeros_like(carry_ref)

    if cache_group is not None:
        _cache_attn_body(refs[N_MERGE_IN + 5:n_in] + refs[n_in + 2:-2], jnp.minimum(step, n_tiles - 1),
                         cache_group, bb=1)

    x = x_ref[...]
    x_ref[...] = _merge_tile(refs[:N_MERGE_IN], yw_channels_first)
    tm = x.shape[0]
    row = lax.broadcasted_iota(jnp.int32, (tm, FF_CHUNK), 0)

    def u_prev(u, c0, c1):
        c6 = carry_ref[6:7, c0:c1]
        c7 = carry_ref[7:8, c0:c1]
        u_m1 = jnp.where(row == 0, c7, pltpu.roll(u, 1, 0))
        u_m2 = jnp.where(row == 0, c6, jnp.where(row == 1, c7, pltpu.roll(u, 2, 0)))
        return u_m2, u_m1

    def u_sink(u, c0):
        carry_ref[:, c0:c0 + FF_CHUNK] = u[tm - 8:tm, :]

    _ffn_body(x, g2_ref, wup_ref, cw_ref, cb_ref, wdn_ref, out_ref, u_prev, u_sink)
    ust_ref[...] = carry_ref[...]


def _ffn_step_kernel(x_ref, p0_ref, p1_ref, g2_ref, wup_ref, cw_ref, cb_ref, wdn_ref, out_ref, u_ref):
    def u_prev(u, c0, c1):
        return p0_ref[:, c0:c1], p1_ref[:, c0:c1]

    def u_sink(u, c0):
        u_ref[:, c0:c0 + FF_CHUNK] = u

    _ffn_body(x_ref[...], g2_ref, wup_ref, cw_ref, cb_ref, wdn_ref, out_ref, u_prev, u_sink)


_FFN_W_SPECS = lambda: [_resident((1, D_MODEL)), _resident((D_MODEL, 2 * D_FF)), _resident((3, 2 * D_FF)),
                        _resident((1, 2 * D_FF)), _resident((D_FF, D_MODEL))]


def _merge_ffn_seq(merge_args, ffn_w, batch, seq, tm, cache_rider=None):
    m = merge_args[0].shape[0]
    nt = seq // tm
    n_tiles = batch * nt
    merge_tile = lambda s: jnp.minimum(s, n_tiles - 1)
    ffn_tile = lambda s: jnp.maximum(s - 1, 0)
    yw = merge_args[7]
    in_specs = _merge_specs(yw, tm, merge_tile) + _FFN_W_SPECS()
    out_specs = [pl.BlockSpec((tm, D_MODEL), lambda s: (ffn_tile(s), 0)),
                 pl.BlockSpec((None, 8, 2 * D_FF), lambda s: (ffn_tile(s) // nt, 0, 0))]
    out_shape = [jax.ShapeDtypeStruct((m, D_MODEL), F32), jax.ShapeDtypeStruct((batch, 8, 2 * D_FF), F32)]
    cache_group, cache_args = cache_rider if cache_rider is not None else (None, ())
    if cache_rider is not None:
        assert cache_args[5].shape[0] == n_tiles, "one sample sequence per row tile"
        c_in, c_out, c_shape = _cache_attn_specs(cache_args, 1, merge_tile)
        in_specs, out_specs, out_shape = in_specs + c_in, out_specs + c_out, out_shape + c_shape
    return pl.pallas_call(
        functools.partial(_ffn_seq_kernel, cache_group=cache_group, yw_channels_first=yw.ndim == 3,
                          tiles_per_seq=nt, n_tiles=n_tiles),
        out_shape=tuple(out_shape),
        grid=(n_tiles + 1,),
        in_specs=in_specs,
        out_specs=tuple(out_specs),
        scratch_shapes=[pltpu.VMEM((8, 2 * D_FF), F32), pltpu.VMEM((tm, D_MODEL), F32)],
        compiler_params=_cparams("arbitrary"), name="merge_ffn_seq",
    )(*merge_args, *ffn_w, *cache_args)


def _ffn_step(x2d, conv_prev, ln2_g, wup, cw, cb, wdn, tm):
    m = x2d.shape[0]
    w_specs = _FFN_W_SPECS()
    row = lambda w: pl.BlockSpec((tm, w), lambda i: (i, 0))
    return pl.pallas_call(
        _ffn_step_kernel,
        out_shape=(jax.ShapeDtypeStruct((m, D_MODEL), F32), jax.ShapeDtypeStruct((m, 2 * D_FF), F32)),
        grid=(m // tm,),
        in_specs=[row(D_MODEL), row(2 * D_FF), row(2 * D_FF)] + w_specs,
        out_specs=(row(D_MODEL), row(2 * D_FF)),
        compiler_params=_cparams("parallel"), name="ffn_step",
    )(x2d, conv_prev[0], conv_prev[1], ln2_g, wup, cw, cb, wdn)


def _pad_rows(w, before, total):
    return jnp.zeros((total, w.shape[1]), w.dtype).at[before:before + w.shape[0]].set(w)


def kernel(x_prompt, x_sample, cache_k_w128, cache_v_w128, cache_k_w512, cache_v_w512, cache_k_w2048, cache_v_w2048,
           state_rwkv_shift, state_rwkv_wkv, state_ffn_conv, ln1_g, w_in, q_norm_g, k_norm_g, w_attn_out,
           rwkv_mu, rwkv_w0, rwkv_w2, rwkv_a0, rwkv_a2, rwkv_g2, rwkv_k_k, rwkv_k_a, rwkv_r_k, rwkv_gn_g, rwkv_gn_b,
           w_rwkv_out, w_o, ln2_g, w_up, conv_w, conv_b, w_down):
    batch, seq, _ = x_prompt.shape
    nsamp = x_sample.shape[0]
    assert seq % SEG == 0 and x_sample.shape[1] == 1 and batch * RWKV_HEADS * WKV_Q == LANES
    assert ln1_g.shape[0] == 1, "single layer"
    lyr = 0

    w_in_bf = w_in[lyr].astype(BF16)
    bd768 = bd512 = _block_diag_ones(SEG_TILE)
    gq = jnp.tile(q_norm_g[lyr], ATT_WIDTH // HEAD_DIM)[None]
    gk = jnp.tile(k_norm_g[lyr], ATT_WIDTH // HEAD_DIM)[None]
    row1 = lambda a: a.reshape(1, -1)
    prep_w = (row1(rwkv_mu[lyr]), row1(rwkv_w0[lyr]), row1(rwkv_a0[lyr]),
              *_split_bf16(_pad_rows(rwkv_w2[lyr], 0, DECAY_LORA + ICLR_LORA)),
              *_split_bf16(_pad_rows(rwkv_a2[lyr], DECAY_LORA, DECAY_LORA + ICLR_LORA)),
              *_split_bf16(rwkv_g2[lyr]),
              row1(rwkv_k_k[lyr]), row1(rwkv_k_a[lyr]), row1(rwkv_r_k[lyr]), bd512)
    merge_w = (row1(rwkv_gn_g[lyr]), row1(rwkv_gn_b[lyr]), bd512, w_attn_out[lyr].astype(BF16),
               w_rwkv_out[lyr].astype(BF16), w_o[lyr].astype(BF16))
    ffn_w = (row1(ln2_g[lyr]), w_up[lyr].astype(BF16), conv_w[lyr], row1(conv_b[lyr]), w_down[lyr].astype(BF16))

    xs = x_sample.reshape(nsamp, D_MODEL)
    qs, ks, vs, prs, gates_s = _proj(xs, row1(ln1_g[lyr]), w_in_bf, bd768, gq, gk, tm=nsamp)
    caches = [c[lyr].transpose(0, 2, 3, 1).reshape(nsamp, GROUP_WIDTH, c.shape[2])
              for c in (cache_k_w128, cache_v_w128, cache_k_w512, cache_v_w512, cache_k_w2048, cache_v_w2048)]
    for g, (win, dil) in enumerate(ATT_GROUPS):
        assert caches[2 * g].shape[2] == BLOCK * dil, "window caches must be full"
    ks_t, vs_t = ks.T, vs.T
    cache_args = lambda g: (qs, ks, vs, ks_t, vs_t, caches[2 * g], caches[2 * g + 1])
    big, mid = N_GROUPS - 1, N_GROUPS - 2
    cache_done = {}

    xp = x_prompt.reshape(batch * seq, D_MODEL)
    keep_max = min(max(w for w, _ in ATT_GROUPS), seq)
    first_half = _proj(xp, row1(ln1_g[lyr]), w_in_bf, bd768, gq, gk, tm=256, prep=(prep_w, batch, seq, keep_max),
                       cache_rider=(mid, cache_args(mid)))
    qn, kn, v, pr_last, gates = first_half[:5]
    na, dec, nb, kp, rp, rv, vkr, bonus, gg = first_half[5:5 + N_PREP_OUT]
    k_tail, v_tail = first_half[5 + N_PREP_OUT:5 + N_SEQ_OUT]
    cache_done[mid] = first_half[5 + N_SEQ_OUT:]
    att = [_attn_prompt(qn, kn, v, g, batch, seq) for g in range(N_GROUPS)]
    yw, s_fin = _wkv_seq(dec, nb, kp, na, rp, rv)
    merge_args = (xp, *[a[0] for a in att], *[a[1] for a in att], yw, vkr, bonus, gg, gates, *merge_w)
    yp, ust, *cache_done[big] = _merge_ffn_seq(merge_args, ffn_w, batch, seq, tm=seq * batch // nsamp,
                                               cache_rider=(big, cache_args(big)))

    y_prompt = yp.reshape(batch, seq, D_MODEL)
    kv_prompt = []
    for g, (win, _) in enumerate(ATT_GROUPS):
        keep = min(win, seq)
        for tail in (k_tail, v_tail):
            c = tail[:, g * GROUP_WIDTH:(g + 1) * GROUP_WIDTH, keep_max - keep:]
            kv_prompt.append(c.reshape(batch, HEADS_PER_GROUP, HEAD_DIM, keep).transpose(0, 3, 1, 2)[None])
    shift_prompt = pr_last[None, :, -1]
    wkv_prompt = (s_fin.reshape(RWKV_HEAD, WKV_SUB, WKV_Q, batch, RWKV_HEADS)
                  .transpose(3, 4, 2, 1, 0).reshape(1, batch, RWKV_HEADS, RWKV_HEAD, RWKV_HEAD))
    conv_prompt = ust[None, :, 6:8]

    kv_sample, o_s, lse_s = [], [], []
    for g, (win, dil) in enumerate(ATT_GROUPS):
        k_out, v_out, o_g, lse_g = cache_done[g] if g in cache_done else _cache_attn(cache_args(g), g, bb=8)
        kv_sample += [c.reshape(nsamp, HEADS_PER_GROUP, HEAD_DIM, win).transpose(0, 3, 1, 2)[None]
                      for c in (k_out, v_out)]
        o_s.append(o_g)
        lse_s.append(lse_g)
    na, dec, nb, kp, rp, rv, vkr, bonus, gg = _prep_step(prs, state_rwkv_shift[lyr], prep_w, tm=nsamp)
    state_t = state_rwkv_wkv[lyr].transpose(1, 2, 3, 0)
    yw_s, state_new = _wkv_step(state_t, dec, nb, kp, na, rp, rv)
    x2s = _merge(xs, o_s, lse_s, yw_s[None], vkr, bonus, gg, gates_s, *merge_w, tm=nsamp)
    conv_prev = state_ffn_conv[lyr]
    ys, u_s = _ffn_step(x2s, (conv_prev[:, 0], conv_prev[:, 1]), *ffn_w, tm=nsamp)

    y_sample = ys.reshape(nsamp, 1, D_MODEL)
    shift_sample = prs[None]
    wkv_sample = state_new.transpose(3, 0, 1, 2)[None]
    conv_sample = jnp.stack([conv_prev[:, 1], u_s], axis=1)[None]

    return (y_prompt, y_sample,
            kv_prompt[0], kv_sample[0], kv_prompt[1], kv_sample[1],
            kv_prompt[2], kv_sample[2], kv_prompt[3], kv_sample[3],
            kv_prompt[4], kv_sample[4], kv_prompt[5], kv_sample[5],
            shift_prompt, shift_sample, wkv_prompt, wkv_sample, conv_prompt, conv_sample)
```

```python
import functools

import jax
import jax.numpy as jnp
from jax import lax
from jax.experimental import pallas as pl
from jax.experimental.pallas import tpu as pltpu

F32 = jnp.float32
BF16 = jnp.bfloat16

D_MODEL = 1024
HEAD_DIM = 64
ATT_GROUPS = ((128, 1), (512, 4), (2048, 16))
N_GROUPS = len(ATT_GROUPS)
HEADS_PER_GROUP = 4
GROUP_WIDTH = HEADS_PER_GROUP * HEAD_DIM
ATT_WIDTH = N_GROUPS * GROUP_WIDTH
BLOCK = 128
SEG = 2048
RWKV_HEADS = 8
RWKV_HEAD = 64
RWKV_WIDTH = RWKV_HEADS * RWKV_HEAD
DECAY_LORA = 64
ICLR_LORA = 64
GATE_LORA = 128
RWKV_PROJ = 3 * RWKV_WIDTH + DECAY_LORA + ICLR_LORA + GATE_LORA
LORA_OFF = 3 * RWKV_WIDTH
GATE_OFF = LORA_OFF + DECAY_LORA + ICLR_LORA
PR_OFF = 3 * ATT_WIDTH
GATES_OFF = PR_OFF + RWKV_PROJ
P_TOTAL = GATES_OFF + 2 * D_MODEL
D_FF = 2816
FF_CHUNK = 256
RMS_EPS = 1e-6
GN_EPS = 64e-5
L2_EPS = 1e-12
NEG_BIG = -1e30

LANES = 128
SEG_TILE = 256
WKV_Q = 4
WKV_SUB = RWKV_HEAD // WKV_Q
WKV_TB = 128
WKV_GROUP = 8
VMEM_LIMIT = 56 * 1024 * 1024
PROJ_ROWS = 256
SMALL_CACHE_BLOCK = 8


def _cparams(*sem):
    return pltpu.CompilerParams(dimension_semantics=sem, vmem_limit_bytes=VMEM_LIMIT)


def _resident(shape):
    n = len(shape)
    return pl.BlockSpec(shape, lambda *_: (0,) * n, pipeline_mode=pl.Buffered(1))


def _split_bf16(x):
    hi = x.astype(BF16)
    lo = (x - hi.astype(F32)).astype(BF16)
    return hi, lo


def _dot(a, b):
    return jnp.dot(a, b, preferred_element_type=F32)


def _dot3(a, b_hi, b_lo):
    a_hi, a_lo = _split_bf16(a)
    return _dot(a_hi, b_hi) + _dot(a_lo, b_hi) + _dot(a_hi, b_lo)


def _seg_sum_bf16(x_bf, ones_bd):
    return jnp.concatenate([_dot(x_bf[:, c:c + SEG_TILE], ones_bd) for c in range(0, x_bf.shape[1], SEG_TILE)],
                           axis=1)


def _seg_sum(x, ones_bd):
    hi, lo = _split_bf16(x)
    return _seg_sum_bf16(hi, ones_bd) + _seg_sum_bf16(lo, ones_bd)


def _sigmoid(x):
    return 1.0 / (1.0 + jnp.exp(-x))


def _block_diag_ones(width):
    idx = jnp.arange(width) // HEAD_DIM
    return (idx[:, None] == idx[None, :]).astype(BF16)


N_PREP_W, N_PREP_OUT = 13, 9


N_SEQ_OUT = N_PREP_OUT + 2


def _proj_kernel(*refs, cache_group, cache_bb, tiles_per_seq):
    with_prep = tiles_per_seq is not None
    x_ref, g1_ref, w_ref, bd_ref, gq_ref, gk_ref = refs[:6]
    n_w = 6 + (N_PREP_W if with_prep else 0)
    n_in = n_w + (N_CACHE_IN if cache_group is not None else 0)
    q_ref, k_ref, v_ref, pr_ref, gate_ref = refs[n_in:n_in + 5]
    n_out = n_in + 5 + (N_SEQ_OUT if with_prep else 0)
    if with_prep:
        carry_ref = refs[-1]

        @pl.when(pl.program_id(0) % tiles_per_seq == 0)
        def _():
            carry_ref[...] = jnp.zeros_like(carry_ref)

    if cache_group is not None:
        cache_refs = refs[n_w:n_in] + refs[n_out:n_out + N_CACHE_OUT]
        _cache_attn_body(cache_refs, pl.program_id(0) * cache_bb, cache_group, cache_bb)
    x = x_ref[...]
    ms = jnp.mean(x * x, axis=-1, keepdims=True)
    xn = (x * lax.rsqrt(ms + RMS_EPS) * g1_ref[...]).astype(BF16)

    def proj(c0, c1):
        return _dot(xn, w_ref[:, c0:c1])

    def head_norm(z, g):
        ss = _seg_sum_bf16((z * z).astype(BF16), bd_ref[...])
        return z * lax.rsqrt(ss * (1.0 / HEAD_DIM) + RMS_EPS) * g

    q_ref[...] = head_norm(proj(0, ATT_WIDTH), gq_ref[...])
    k = head_norm(proj(ATT_WIDTH, 2 * ATT_WIDTH), gk_ref[...])
    v = proj(2 * ATT_WIDTH, PR_OFF)
    k_ref[...] = k
    v_ref[...] = v
    gate_ref[...] = proj(GATES_OFF, P_TOTAL).astype(gate_ref.dtype)
    pr = proj(PR_OFF, GATES_OFF)
    if not with_prep:
        pr_ref[...] = pr
        return
    kt_ref, vt_ref = refs[n_out - 2:n_out]
    kt_ref[...] = k.T
    vt_ref[...] = v.T
    tm = pr.shape[0]
    row = lax.broadcasted_iota(jnp.int32, pr.shape, 0)
    prev = jnp.where(row == 0, carry_ref[7:8, :], pltpu.roll(pr, 1, 0))
    carry_ref[...] = pr[tm - 8:tm, :]
    pr_ref[...] = pr[tm - 8:tm, :]
    _prep_math(pr, prev, *refs[6:n_w], refs[n_in + 5:n_in + 5 + N_PREP_OUT])


def _proj(x2d, ln1_g, w_in_bf, bd768, gq, gk, tm, prep=None, cache_rider=None):
    m = x2d.shape[0]
    row = lambda w: pl.BlockSpec((tm, w), lambda i: (i, 0))
    widths = (ATT_WIDTH, ATT_WIDTH, ATT_WIDTH, RWKV_PROJ, 2 * D_MODEL)
    in_specs = [row(D_MODEL), _resident((1, D_MODEL)), _resident((D_MODEL, P_TOTAL)),
                _resident((SEG_TILE, SEG_TILE)), _resident((1, ATT_WIDTH)), _resident((1, ATT_WIDTH))]
    out_specs = [row(w) for w in widths]
    out_shape = [jax.ShapeDtypeStruct((m, w), BF16 if i == len(widths) - 1 else F32) for i, w in enumerate(widths)]
    prep_w, tiles_per_seq, scratch = (), None, []
    if prep is not None:
        prep_w, batch, seq, keep = prep
        assert keep % tm == 0 and seq % tm == 0
        tiles_per_seq = seq // tm
        in_specs += [_resident(w.shape) for w in prep_w]
        out_specs[3] = pl.BlockSpec((None, 8, RWKV_PROJ), lambda i: (i // tiles_per_seq, 0, 0))
        out_shape[3] = jax.ShapeDtypeStruct((batch, 8, RWKV_PROJ), F32)
        chan = pl.BlockSpec((None, RWKV_WIDTH, tm), lambda i: (i // tiles_per_seq, 0, i % tiles_per_seq))
        out_specs += [chan] * 6 + [row(RWKV_WIDTH)] * 3
        out_shape += [jax.ShapeDtypeStruct((batch, RWKV_WIDTH, seq), F32)] * 6
        out_shape += [jax.ShapeDtypeStruct((m, RWKV_WIDTH), F32)] * 3
        first = (seq - keep) // tm
        tail = pl.BlockSpec((None, ATT_WIDTH, tm),
                            lambda i: (i // tiles_per_seq, 0, jnp.maximum(i % tiles_per_seq - first, 0)))
        out_specs += [tail] * 2
        out_shape += [jax.ShapeDtypeStruct((batch, ATT_WIDTH, keep), F32)] * 2
        scratch = [pltpu.VMEM((8, RWKV_PROJ), F32)]
    cache_group, cache_args, cache_bb = None, (), 0
    if cache_rider is not None:
        cache_group, cache_args = cache_rider
        cache_bb = cache_args[5].shape[0] * tm // m
        assert cache_bb * (m // tm) == cache_args[5].shape[0]
        c_in, c_out, c_shape = _cache_attn_specs(cache_args, cache_bb, lambda i: i)
        in_specs, out_specs, out_shape = in_specs + c_in, out_specs + c_out, out_shape + c_shape
    return pl.pallas_call(
        functools.partial(_proj_kernel, cache_group=cache_group, cache_bb=cache_bb, tiles_per_seq=tiles_per_seq),
        out_shape=tuple(out_shape),
        grid=(m // tm,),
        in_specs=in_specs,
        out_specs=tuple(out_specs),
        scratch_shapes=scratch,
        compiler_params=_cparams("arbitrary"),
        name="proj",
    )(x2d, ln1_g, w_in_bf, bd768, gq, gk, *prep_w, *cache_args)


def _attn_prompt_kernel(*refs, dil):
    q_ref, kp_ref, kc_ref, vp_ref, vc_ref = (refs[2 * i:2 * i + 2] for i in range(5))
    o_ref, lse_ref = refs[10], refs[11]
    not_first_seg = pl.program_id(1) > 0
    nblk = SEG // (BLOCK * dil)
    stacked = (HEADS_PER_GROUP * BLOCK, 2 * BLOCK)
    row = lax.broadcasted_iota(jnp.int32, stacked, 0) % BLOCK
    col = lax.broadcasted_iota(jnp.int32, stacked, 1)
    in_prev = col < BLOCK
    band_prev = in_prev & (col >= row)
    band_cur = jnp.logical_not(in_prev) & (col - BLOCK <= row)
    bias_cur = jnp.where(band_cur, 0.0, NEG_BIG)
    bias_both = jnp.where(band_cur | band_prev, 0.0, NEG_BIG)
    lane = lax.broadcasted_iota(jnp.int32, (1, GROUP_WIDTH), 1)
    head_masks = [(lane // HEAD_DIM) == h for h in range(HEADS_PER_GROUP)]

    def row_slice(start):
        return pl.ds(start, BLOCK) if dil == 1 else pl.ds(start, BLOCK, stride=dil)

    def rows(halves, start):
        return jnp.concatenate([h[row_slice(start), :] for h in halves], axis=1)

    def block(q_start, k1_ref, k1_start, v1_ref, prev_valid):
        q = rows(q_ref, q_start) * (HEAD_DIM ** -0.5)
        q4 = jnp.concatenate([jnp.where(hm, q, 0.0) for hm in head_masks], axis=0).astype(BF16)
        kt = jnp.concatenate([rows(k1_ref, k1_start), rows(kc_ref, q_start)], axis=0).astype(BF16)
        vt = jnp.concatenate([rows(v1_ref, k1_start), rows(vc_ref, q_start)], axis=0).astype(BF16)
        s = lax.dot_general(q4, kt, (((1,), (1,)), ((), ())), preferred_element_type=F32)
        s = s + (bias_both if prev_valid is True else jnp.where(prev_valid, bias_both, bias_cur))
        m = jnp.max(s, axis=-1, keepdims=True)
        p = jnp.exp(s - m)
        l = jnp.sum(p, axis=-1, keepdims=True)
        ov = _dot(p.astype(BF16), vt)
        lse = m + jnp.log(l)
        o = jnp.zeros((BLOCK, GROUP_WIDTH), F32)
        l_b = jnp.ones((BLOCK, GROUP_WIDTH), F32)
        lse_b = jnp.zeros((BLOCK, GROUP_WIDTH), F32)
        for h, hm in enumerate(head_masks):
            sl = slice(h * BLOCK, (h + 1) * BLOCK)
            o = jnp.where(hm, ov[sl], o)
            l_b = jnp.where(hm, l[sl], l_b)
            lse_b = jnp.where(hm, lse[sl], lse_b)
        o = o / l_b
        for half in range(2):
            lanes = slice(half * LANES, (half + 1) * LANES)
            o_ref[half, row_slice(q_start), :] = o[:, lanes]
            lse_ref[half, row_slice(q_start), :] = lse_b[:, lanes]

    def first_blocks(r, carry):
        block(r, kp_ref, SEG - BLOCK * dil + r, vp_ref, not_first_seg)
        return carry

    def later_blocks(i, carry):
        r = i % dil
        nb = 1 + i // dil
        q_start = nb * BLOCK * dil + r
        block(q_start, kc_ref, q_start - BLOCK * dil, vc_ref, True)
        return carry

    lax.fori_loop(0, dil, first_blocks, 0, unroll=min(8, dil))
    if nblk > 1:
        n_later = dil * (nblk - 1)
        lax.fori_loop(0, n_later, later_blocks, 0, unroll=max(u for u in (3, 4, 5) if n_later % u == 0))


def _attn_prompt(qn, kn, v, group, batch, seq):
    dil = ATT_GROUPS[group][1]
    nseg = seq // SEG
    halves = GROUP_WIDTH // LANES
    cur = [pl.BlockSpec((SEG, LANES), lambda b, s, c=halves * group + h: (b * nseg + s, c)) for h in range(halves)]
    prev = [pl.BlockSpec((SEG, LANES), lambda b, s, c=halves * group + h: (b * nseg + jnp.maximum(s - 1, 0), c))
            for h in range(halves)]
    out = pl.BlockSpec((halves, SEG, LANES), lambda b, s: (0, b * nseg + s, 0))
    shape = jax.ShapeDtypeStruct((halves, batch * seq, LANES), F32)
    return pl.pallas_call(
        functools.partial(_attn_prompt_kernel, dil=dil),
        out_shape=(shape, shape),
        grid=(batch, nseg),
        in_specs=cur + prev + cur + prev + cur,
        out_specs=(out, out),
        compiler_params=_cparams("parallel", "arbitrary"),
        name=f"attn_prompt_g{group}",
    )(qn, qn, kn, kn, kn, kn, v, v, v, v)


def _cache_attn_body(refs, first, group, bb):
    q_ref, kn_ref, vn_ref, knt_ref, vnt_ref, kc_ref, vc_ref, ko_ref, vo_ref, o_ref, lse_ref = refs
    dil = ATT_GROUPS[group][1]
    win = kc_ref.shape[2]
    nseq = q_ref.shape[0]
    cols = slice(group * GROUP_WIDTH, (group + 1) * GROUP_WIDTH)
    pos = lax.broadcasted_iota(jnp.int32, (1, win), 1)
    in_window = (pos % dil) == 0
    newest = lax.broadcasted_iota(jnp.int32, (GROUP_WIDTH, LANES), 1) == LANES - 1
    head_rows = (lax.broadcasted_iota(jnp.int32, (8, GROUP_WIDTH), 0)
                 == lax.broadcasted_iota(jnp.int32, (8, GROUP_WIDTH), 1) // HEAD_DIM)
    seq_lane = lax.broadcasted_iota(jnp.int32, (GROUP_WIDTH, nseq), 1)

    def hi_lo_rows(x):
        hi, lo = _split_bf16(x)
        return jnp.concatenate([hi, lo], axis=0)

    def fold(x):
        return x[0:8] + x[8:16]

    def shifted(cache, new_t_ref, n):
        new_col = jnp.sum(jnp.where(seq_lane == n, new_t_ref[cols, :], 0.0), axis=1, keepdims=True)
        rolled = pltpu.roll(cache, win - 1, 1)
        last_tile = jnp.where(newest, new_col, rolled[:, win - LANES:])
        return last_tile if win == LANES else jnp.concatenate([rolled[:, :win - LANES], last_tile], axis=1)

    for b in range(bb):
        n = first + b
        q = q_ref[pl.ds(n, 1), cols] * (HEAD_DIM ** -0.5)
        k_new = kn_ref[pl.ds(n, 1), cols]
        v_new = vn_ref[pl.ds(n, 1), cols]
        q8 = jnp.where(head_rows, q, 0.0)
        kc = kc_ref[b]
        vc = vc_ref[b]
        s = fold(_dot(hi_lo_rows(q8), kc.astype(BF16)))
        s = jnp.where(in_window, s, NEG_BIG)
        s_new = jnp.sum(q8 * k_new, axis=1, keepdims=True)
        m = jnp.maximum(jnp.max(s, axis=1, keepdims=True), s_new)
        p = jnp.exp(s - m)
        p_new = jnp.exp(s_new - m)
        l = jnp.sum(p, axis=1, keepdims=True) + p_new
        pv = fold(lax.dot_general(hi_lo_rows(p), vc.astype(BF16), (((1,), (1,)), ((), ())),
                                  preferred_element_type=F32))
        ov = (pv + p_new * v_new) / l
        o = jnp.sum(jnp.where(head_rows, ov, 0.0), axis=0, keepdims=True)
        lse = jnp.sum(jnp.where(head_rows, m + jnp.log(l), 0.0), axis=0, keepdims=True)
        for half in range(GROUP_WIDTH // LANES):
            lanes = slice(half * LANES, (half + 1) * LANES)
            o_ref[half, pl.ds(n, 1), :] = o[:, lanes]
            lse_ref[half, pl.ds(n, 1), :] = lse[:, lanes]
        ko_ref[b] = shifted(kc, knt_ref, n)
        vo_ref[b] = shifted(vc, vnt_ref, n)


N_CACHE_IN, N_CACHE_OUT = 7, 4


def _cache_attn_specs(args, bb, block_of):
    n, _, win = args[5].shape
    halves = GROUP_WIDTH // LANES
    cache = pl.BlockSpec((bb, GROUP_WIDTH, win), lambda *g: (block_of(*g), 0, 0))
    rows = _resident((n, ATT_WIDTH))
    rows_t = _resident((ATT_WIDTH, n))
    out = pl.BlockSpec((halves, n, LANES), lambda *g: (0, 0, 0))
    cache_shape = jax.ShapeDtypeStruct(args[5].shape, F32)
    o_shape = jax.ShapeDtypeStruct((halves, n, LANES), F32)
    return [rows] * 3 + [rows_t] * 2 + [cache] * 2, [cache, cache, out, out], [cache_shape] * 2 + [o_shape] * 2


def _cache_attn_kernel(*refs, group, bb):
    _cache_attn_body(refs, pl.program_id(0) * bb, group, bb)


def _cache_attn(args, group, bb):
    in_specs, out_specs, out_shape = _cache_attn_specs(args, bb, lambda i: i)
    return pl.pallas_call(
        functools.partial(_cache_attn_kernel, group=group, bb=bb),
        out_shape=tuple(out_shape),
        grid=(args[5].shape[0] // bb,),
        in_specs=in_specs,
        out_specs=tuple(out_specs),
        compiler_params=_cparams("arbitrary"),
        name=f"cache_attn_g{group}",
    )(*args)


def _prep_math(p, prev, mu_ref, w0_ref, a0_ref, w2h_ref, w2l_ref, a2h_ref, a2l_ref, g2h_ref, g2l_ref,
               kk_ref, ka_ref, rk_ref, bd_ref, outs):
    na_ref, dec_ref, nb_ref, kp_ref, rp_ref, v_ref, vkr_ref, bonus_ref, g_ref = outs
    bd = bd_ref[...]

    def put(ref, val):
        ref[...] = val.T

    xs = p + mu_ref[...] * (prev - p)
    r = xs[:, 0:RWKV_WIDTH]
    k = xs[:, RWKV_WIDTH:2 * RWKV_WIDTH]
    v = xs[:, 2 * RWKV_WIDTH:3 * RWKV_WIDTH]
    lora = xs[:, LORA_OFF:GATE_OFF]
    gl = xs[:, GATE_OFF:RWKV_PROJ]
    z = w0_ref[...] + _dot3(jnp.tanh(lora), w2h_ref[...], w2l_ref[...])
    softplus_neg = jnp.maximum(-z, 0.0) + jnp.log(1.0 + jnp.exp(-jnp.abs(z)))
    w = -softplus_neg - 0.5
    dec = jnp.exp(-jnp.exp(w))
    a = _sigmoid(a0_ref[...] + _dot3(lora, a2h_ref[...], a2l_ref[...]))
    g = _dot3(_sigmoid(gl), g2h_ref[...], g2l_ref[...])
    kk = k * kk_ref[...]
    kk = kk / jnp.maximum(jnp.sqrt(_seg_sum(kk * kk, bd)), L2_EPS)
    kp = k * (1.0 + (a - 1.0) * ka_ref[...])
    na = -kk
    nb = kk * a
    rp = dec * r + na * _seg_sum(nb * r, bd)
    put(na_ref, na)
    put(dec_ref, dec)
    put(nb_ref, nb)
    put(kp_ref, kp)
    put(rp_ref, rp)
    put(v_ref, v)
    vkr_ref[...] = v * _seg_sum(kp * r, bd)
    bonus_ref[...] = _seg_sum(r * kp * rk_ref[...], bd) * v
    g_ref[...] = g


def _prep_step_kernel(p_ref, prev_ref, *rest):
    weights, outs = rest[:N_PREP_W], rest[N_PREP_W:N_PREP_W + N_PREP_OUT]
    _prep_math(p_ref[...], prev_ref[...], *weights, outs)


def _prep_step(pr, shift_prev, weights, tm):
    m = pr.shape[0]
    w_specs = [_resident(w.shape) for w in weights]
    out_shape = tuple(jax.ShapeDtypeStruct((m, RWKV_WIDTH), F32) for _ in range(9))
    row = lambda w: pl.BlockSpec((tm, w), lambda i: (i, 0))
    chan = pl.BlockSpec((RWKV_WIDTH, tm), lambda i: (0, i))
    chan_shape = jax.ShapeDtypeStruct((RWKV_WIDTH, m), F32)
    return pl.pallas_call(
        _prep_step_kernel, out_shape=(chan_shape,) * 6 + out_shape[6:], grid=(m // tm,),
        in_specs=[row(RWKV_PROJ), row(RWKV_PROJ)] + w_specs,
        out_specs=(chan,) * 6 + tuple(row(RWKV_WIDTH) for _ in range(3)),
        compiler_params=_cparams("parallel"), name="rwkv_prep_step",
    )(pr, shift_prev, *weights)


def _wkv_seq_kernel(w_ref, nb_ref, k_ref, na_ref, rp_ref, v_ref, y_ref, sout_ref,
                    s_ref, e_ref, first_ref, stage_ref, vt_ref, yt_ref):
    @pl.when(pl.program_id(0) == 0)
    def _():
        s_ref[...] = jnp.zeros_like(s_ref)

    nbatch = w_ref.shape[0]
    tb = w_ref.shape[2]
    zero = jnp.zeros((WKV_SUB, LANES), F32)

    def head_rows(ref, b, c):
        return ref[b, pl.ds(c, RWKV_HEADS, stride=RWKV_HEAD), :]

    def time_major(ref, j):
        rows = jnp.concatenate([head_rows(ref, b, j) for b in range(nbatch)], axis=0)
        return jnp.concatenate([rows] * WKV_Q, axis=0).T

    unroll = 8

    def build_e(jg, carry):
        for j in [jg * unroll + u for u in range(unroll)]:
            for o, ref in enumerate((w_ref, nb_ref, k_ref)):
                e_ref[o, j] = time_major(ref, j)
            for o, ref in ((3, na_ref), (4, rp_ref)):
                rows = time_major(ref, j)
                first_ref[o - 3, j] = rows[0:WKV_GROUP]
                e_ref[o, j] = pltpu.roll(rows, tb - 1, 0)
        return carry

    def build_v(ig, carry):
        for ih in [ig * unroll + u for u in range(unroll)]:
            rows = jnp.concatenate([head_rows(v_ref, b, il * WKV_SUB + ih)
                                    for il in range(WKV_Q) for b in range(nbatch)], axis=0)
            vt_ref[pl.ds(ih, tb, stride=WKV_SUB), :] = rows.T
        return carry

    lax.fori_loop(0, RWKV_HEAD // unroll, build_e, 0)
    lax.fori_loop(0, WKV_SUB // unroll, build_v, 0)

    n_ops = 5
    pairs = [(o, j) for o in range(n_ops) for j in range(RWKV_HEAD)]
    per_step = len(pairs) // WKV_GROUP

    def stage_tiles(slot, group, which):
        rows = pl.ds(pl.multiple_of(jnp.minimum(group, tb // WKV_GROUP - 1) * WKV_GROUP, WKV_GROUP), WKV_GROUP)
        for o, j in which:
            stage_ref[slot, o, j] = e_ref[o, j, rows, :]

    def step(group, u, slot, carry):
        sa, y = carry
        t = group * WKV_GROUP + u
        tile = pl.ds(pl.multiple_of(t * WKV_SUB, WKV_SUB), WKV_SUB)
        yt_ref[tile, :] = y
        v = vt_ref[tile, :]
        stage_tiles(1 - slot, group + 1, pairs[u * per_step:(u + 1) * per_step])
        n_acc = 4
        sa_acc = [zero] * n_acc
        y_acc = [zero] * n_acc
        for j in range(RWKV_HEAD):
            row = lambda o: stage_ref[slot, o, j, u:u + 1, :]
            sn = s_ref[j] * row(0) + sa * row(1) + v * row(2)
            s_ref[j] = sn
            sa_acc[j % n_acc] = sa_acc[j % n_acc] + sn * row(3)
            y_acc[j % n_acc] = y_acc[j % n_acc] + sn * row(4)
        return (sa_acc[0] + sa_acc[1]) + (sa_acc[2] + sa_acc[3]), (y_acc[0] + y_acc[1]) + (y_acc[2] + y_acc[3])

    region = 1

    def two_groups(gg, carry):
        for slot in range(2):
            for u0 in range(0, WKV_GROUP, region):
                def do(c, slot=slot, u0=u0):
                    for u in range(u0, u0 + region):
                        c = step(2 * gg + slot, u, slot, c)
                    return c
                carry = do(carry) if (slot, u0) == (0, 0) else lax.cond(gg >= 0, do, lambda c: c, carry)
        return carry

    def first_matvecs(j, carry):
        sa, y = carry
        sj = s_ref[j]
        return sa + sj * first_ref[0, j, 0:1, :], y + sj * first_ref[1, j, 0:1, :]

    stage_tiles(0, 0, pairs)
    lax.fori_loop(0, tb // (2 * WKV_GROUP), two_groups, lax.fori_loop(0, RWKV_HEAD, first_matvecs, (zero, zero)))

    def emit_y(ig, carry):
        for ih in [ig * unroll + u for u in range(unroll)]:
            cols = yt_ref[pl.ds(ih, tb, stride=WKV_SUB), :].T
            for il in range(WKV_Q):
                for b in range(nbatch):
                    r0 = il * (LANES // WKV_Q) + b * RWKV_HEADS
                    y_ref[b, pl.ds(il * WKV_SUB + ih, RWKV_HEADS, stride=RWKV_HEAD), :] = cols[r0:r0 + RWKV_HEADS]
        return carry

    lax.fori_loop(0, WKV_SUB // unroll, emit_y, 0)

    @pl.when(pl.program_id(0) == pl.num_programs(0) - 1)
    def _():
        sout_ref[...] = s_ref[...]


def _wkv_seq(w, nb, k, na, rp, v):
    batch, _, t = w.shape
    blk = pl.BlockSpec((batch, RWKV_WIDTH, WKV_TB), lambda i: (0, 0, i))
    state = pl.BlockSpec((RWKV_HEAD, WKV_SUB, LANES), lambda i: (0, 0, 0))
    return pl.pallas_call(
        _wkv_seq_kernel,
        out_shape=(jax.ShapeDtypeStruct((batch, RWKV_WIDTH, t), F32),
                   jax.ShapeDtypeStruct((RWKV_HEAD, WKV_SUB, LANES), F32)),
        grid=(t // WKV_TB,),
        in_specs=[blk] * 6,
        out_specs=(blk, state),
        scratch_shapes=[pltpu.VMEM((RWKV_HEAD, WKV_SUB, LANES), F32),
                        pltpu.VMEM((5, RWKV_HEAD, WKV_TB, LANES), F32),
                        pltpu.VMEM((2, RWKV_HEAD, WKV_GROUP, LANES), F32),
                        pltpu.VMEM((2, 5, RWKV_HEAD, WKV_GROUP, LANES), F32),
                        pltpu.VMEM((WKV_TB * WKV_SUB, LANES), F32),
                        pltpu.VMEM((WKV_TB * WKV_SUB, LANES), F32)],
        compiler_params=_cparams("arbitrary"),
        name="wkv_seq",
    )(w, nb, k, na, rp, v)


def _wkv_step_kernel(s_ref, w_ref, nb_ref, k_ref, na_ref, rp_ref, v_ref, y_ref, sout_ref):
    w, nb, k, na, rp = w_ref[...], nb_ref[...], k_ref[...], na_ref[...], rp_ref[...]
    for i in range(RWKV_HEAD):
        si = s_ref[i]
        sa = jnp.sum(si * na, axis=0, keepdims=True)
        y_ref[i:i + 1, :] = jnp.sum(si * rp, axis=0, keepdims=True)
        sout_ref[i] = si * w + sa * nb + v_ref[i:i + 1, :] * k


def _wkv_step(state, w, nb, k, na, rp, v):
    n = state.shape[-1]
    st = pl.BlockSpec((None, RWKV_HEAD, RWKV_HEAD, n), lambda h: (h, 0, 0, 0))
    vec = pl.BlockSpec((RWKV_HEAD, n), lambda h: (h, 0))
    return pl.pallas_call(
        _wkv_step_kernel,
        out_shape=(jax.ShapeDtypeStruct((RWKV_WIDTH, n), F32), jax.ShapeDtypeStruct(state.shape, F32)),
        grid=(RWKV_HEADS,),
        in_specs=[st] + [vec] * 6,
        out_specs=(vec, st),
        compiler_params=_cparams("parallel"),
        name="wkv_step",
    )(state, w, nb, k, na, rp, v)


N_MERGE_IN = 18


def _merge_tile(refs, yw_channels_first):
    (x_ref, o0_ref, o1_ref, o2_ref, l0_ref, l1_ref, l2_ref, yw_ref, vkr_ref, bonus_ref, g_ref,
     gate_ref, gng_ref, gnb_ref, bd_ref, wa_ref, wr_ref, wo_ref) = refs
    wide = lambda ref: jnp.concatenate([ref[0], ref[1]], axis=1)
    l0, l1, l2 = wide(l0_ref), wide(l1_ref), wide(l2_ref)
    lm = jnp.maximum(jnp.maximum(l0, l1), l2)
    e0, e1, e2 = jnp.exp(l0 - lm), jnp.exp(l1 - lm), jnp.exp(l2 - lm)
    o_att = (e0 * wide(o0_ref) + e1 * wide(o1_ref) + e2 * wide(o2_ref)) / (e0 + e1 + e2)
    y_att = _dot(o_att.astype(BF16), wa_ref[...])

    bd = bd_ref[...]
    y = (yw_ref[...].T if yw_channels_first else yw_ref[...]) + vkr_ref[...]
    mean = _seg_sum(y, bd) * (1.0 / RWKV_HEAD)
    yc = y - mean
    var = _seg_sum(yc * yc, bd) * (1.0 / RWKV_HEAD)
    yn = yc * lax.rsqrt(var + GN_EPS) * gng_ref[...] + gnb_ref[...]
    o_rwkv = (yn + bonus_ref[...]) * g_ref[...]
    y_rwkv = _dot(o_rwkv.astype(BF16), wr_ref[...])

    gate_att = gate_ref[:, 0:D_MODEL].astype(F32)
    gate_rwkv = gate_ref[:, D_MODEL:2 * D_MODEL].astype(F32)
    merged = _sigmoid(gate_att) * y_att + _sigmoid(gate_rwkv) * y_rwkv
    return x_ref[...] + _dot(merged.astype(BF16), wo_ref[...])


def _merge_kernel(*refs, yw_channels_first):
    refs[N_MERGE_IN][...] = _merge_tile(refs[:N_MERGE_IN], yw_channels_first)


def _merge_specs(yw, tm, tile_of):
    row = lambda w: pl.BlockSpec((tm, w), lambda *g: (tile_of(*g), 0))
    if yw.ndim == 3:
        nt = yw.shape[2] // tm
        yw_spec = pl.BlockSpec((None, RWKV_WIDTH, tm), lambda *g: (tile_of(*g) // nt, 0, tile_of(*g) % nt))
    else:
        yw_spec = row(RWKV_WIDTH)
    halves = pl.BlockSpec((GROUP_WIDTH // LANES, tm, LANES), lambda *g: (0, tile_of(*g), 0))
    return ([row(D_MODEL)] + [halves] * 6 + [yw_spec] + [row(RWKV_WIDTH)] * 3 + [row(2 * D_MODEL)]
            + [_resident((1, RWKV_WIDTH)), _resident((1, RWKV_WIDTH)), _resident((SEG_TILE, SEG_TILE)),
               _resident((GROUP_WIDTH, D_MODEL)), _resident((RWKV_WIDTH, D_MODEL)), _resident((D_MODEL, D_MODEL))])


def _merge(x2d, o_groups, lse_groups, yw, vkr, bonus, g, gates, gn_g, gn_b, bd512, wa, wr, wo, tm):
    m = x2d.shape[0]
    return pl.pallas_call(
        functools.partial(_merge_kernel, yw_channels_first=yw.ndim == 3),
        out_shape=jax.ShapeDtypeStruct((m, D_MODEL), F32),
        grid=(m // tm,),
        in_specs=_merge_specs(yw, tm, lambda i: i),
        out_specs=pl.BlockSpec((tm, D_MODEL), lambda i: (i, 0)),
        compiler_params=_cparams("parallel"),
        name="merge",
    )(x2d, *o_groups, *lse_groups, yw, vkr, bonus, g, gates, gn_g, gn_b, bd512, wa, wr, wo)


def _ffn_body(x, g2_ref, wup_ref, cw_ref, cb_ref, wdn_ref, out_ref, u_prev, u_sink):
    ms = jnp.mean(x * x, axis=-1, keepdims=True)
    xn = (x * lax.rsqrt(ms + RMS_EPS) * g2_ref[...]).astype(BF16)

    def up(c):
        return tuple(_dot(xn, wup_ref[:, c0:c0 + FF_CHUNK]) for c0 in (c, D_FF + c))

    def conv(u, c0):
        c1 = c0 + FF_CHUNK
        u_m2, u_m1 = u_prev(u, c0, c1)
        u_sink(u, c0)
        return cb_ref[:, c0:c1] + u_m2 * cw_ref[0:1, c0:c1] + u_m1 * cw_ref[1:2, c0:c1] + u * cw_ref[2:3, c0:c1]

    acc = x
    nxt = up(0)
    for c in range(0, D_FF, FF_CHUNK):
        ug, uv = nxt
        if c + FF_CHUNK < D_FF:
            nxt = up(c + FF_CHUNK)
        gate = conv(ug, c)
        val = conv(uv, D_FF + c)
        h = gate * _sigmoid(gate) * val
        acc = acc + _dot(h.astype(BF16), wdn_ref[c:c + FF_CHUNK, :])
    out_ref[...] = acc


def _ffn_seq_kernel(*refs, cache_group, yw_channels_first, tiles_per_seq, n_tiles):
    g2_ref, wup_ref, cw_ref, cb_ref, wdn_ref = refs[N_MERGE_IN:N_MERGE_IN + 5]
    n_in = N_MERGE_IN + 5 + (N_CACHE_IN if cache_group is not None else 0)
    out_ref, ust_ref = refs[n_in:n_in + 2]
    carry_ref, x_ref = refs[-2:]
    step = pl.program_id(0)
    ffn_tile = jnp.maximum(step - 1, 0)

    @pl.when(step == 0)
    def _():
        x_ref[...] = jnp.zeros_like(x_ref)

    @pl.when(ffn_tile % tiles_per_seq == 0)
    def _():
        carry_ref[...] = jnp.zeros_like(carry_ref)

    if cache_group is not None:
        _cache_attn_body(refs[N_MERGE_IN + 5:n_in] + refs[n_in + 2:-2], jnp.minimum(step, n_tiles - 1),
                         cache_group, bb=1)

    x = x_ref[...]
    x_ref[...] = _merge_tile(refs[:N_MERGE_IN], yw_channels_first)
    tm = x.shape[0]
    row = lax.broadcasted_iota(jnp.int32, (tm, FF_CHUNK), 0)

    def u_prev(u, c0, c1):
        c6 = carry_ref[6:7, c0:c1]
        c7 = carry_ref[7:8, c0:c1]
        u_m1 = jnp.where(row == 0, c7, pltpu.roll(u, 1, 0))
        u_m2 = jnp.where(row == 0, c6, jnp.where(row == 1, c7, pltpu.roll(u, 2, 0)))
        return u_m2, u_m1

    def u_sink(u, c0):
        carry_ref[:, c0:c0 + FF_CHUNK] = u[tm - 8:tm, :]

    _ffn_body(x, g2_ref, wup_ref, cw_ref, cb_ref, wdn_ref, out_ref, u_prev, u_sink)
    ust_ref[...] = carry_ref[...]


def _ffn_step_kernel(x_ref, p0_ref, p1_ref, g2_ref, wup_ref, cw_ref, cb_ref, wdn_ref, out_ref, u_ref):
    def u_prev(u, c0, c1):
        return p0_ref[:, c0:c1], p1_ref[:, c0:c1]

    def u_sink(u, c0):
        u_ref[:, c0:c0 + FF_CHUNK] = u

    _ffn_body(x_ref[...], g2_ref, wup_ref, cw_ref, cb_ref, wdn_ref, out_ref, u_prev, u_sink)


_FFN_W_SPECS = lambda: [_resident((1, D_MODEL)), _resident((D_MODEL, 2 * D_FF)), _resident((3, 2 * D_FF)),
                        _resident((1, 2 * D_FF)), _resident((D_FF, D_MODEL))]


def _merge_ffn_seq(merge_args, ffn_w, batch, seq, tm, cache_rider=None):
    m = merge_args[0].shape[0]
    nt = seq // tm
    n_tiles = batch * nt
    merge_tile = lambda s: jnp.minimum(s, n_tiles - 1)
    ffn_tile = lambda s: jnp.maximum(s - 1, 0)
    yw = merge_args[7]
    in_specs = _merge_specs(yw, tm, merge_tile) + _FFN_W_SPECS()
    out_specs = [pl.BlockSpec((tm, D_MODEL), lambda s: (ffn_tile(s), 0)),
                 pl.BlockSpec((None, 8, 2 * D_FF), lambda s: (ffn_tile(s) // nt, 0, 0))]
    out_shape = [jax.ShapeDtypeStruct((m, D_MODEL), F32), jax.ShapeDtypeStruct((batch, 8, 2 * D_FF), F32)]
    cache_group, cache_args = cache_rider if cache_rider is not None else (None, ())
    if cache_rider is not None:
        assert cache_args[5].shape[0] == n_tiles, "one sample sequence per row tile"
        c_in, c_out, c_shape = _cache_attn_specs(cache_args, 1, merge_tile)
        in_specs, out_specs, out_shape = in_specs + c_in, out_specs + c_out, out_shape + c_shape
    return pl.pallas_call(
        functools.partial(_ffn_seq_kernel, cache_group=cache_group, yw_channels_first=yw.ndim == 3,
                          tiles_per_seq=nt, n_tiles=n_tiles),
        out_shape=tuple(out_shape),
        grid=(n_tiles + 1,),
        in_specs=in_specs,
        out_specs=tuple(out_specs),
        scratch_shapes=[pltpu.VMEM((8, 2 * D_FF), F32), pltpu.VMEM((tm, D_MODEL), F32)],
        compiler_params=_cparams("arbitrary"), name="merge_ffn_seq",
    )(*merge_args, *ffn_w, *cache_args)


def _ffn_step(x2d, conv_prev, ln2_g, wup, cw, cb, wdn, tm):
    m = x2d.shape[0]
    w_specs = _FFN_W_SPECS()
    row = lambda w: pl.BlockSpec((tm, w), lambda i: (i, 0))
    return pl.pallas_call(
        _ffn_step_kernel,
        out_shape=(jax.ShapeDtypeStruct((m, D_MODEL), F32), jax.ShapeDtypeStruct((m, 2 * D_FF), F32)),
        grid=(m // tm,),
        in_specs=[row(D_MODEL), row(2 * D_FF), row(2 * D_FF)] + w_specs,
        out_specs=(row(D_MODEL), row(2 * D_FF)),
        compiler_params=_cparams("parallel"), name="ffn_step",
    )(x2d, conv_prev[0], conv_prev[1], ln2_g, wup, cw, cb, wdn)


def _pad_rows(w, before, total):
    return jnp.zeros((total, w.shape[1]), w.dtype).at[before:before + w.shape[0]].set(w)


def kernel(x_prompt, x_sample, cache_k_w128, cache_v_w128, cache_k_w512, cache_v_w512, cache_k_w2048, cache_v_w2048,
           state_rwkv_shift, state_rwkv_wkv, state_ffn_conv, ln1_g, w_in, q_norm_g, k_norm_g, w_attn_out,
           rwkv_mu, rwkv_w0, rwkv_w2, rwkv_a0, rwkv_a2, rwkv_g2, rwkv_k_k, rwkv_k_a, rwkv_r_k, rwkv_gn_g, rwkv_gn_b,
           w_rwkv_out, w_o, ln2_g, w_up, conv_w, conv_b, w_down):
    batch, seq, _ = x_prompt.shape
    nsamp = x_sample.shape[0]
    assert seq % SEG == 0 and x_sample.shape[1] == 1 and batch * RWKV_HEADS * WKV_Q == LANES
    assert ln1_g.shape[0] == 1, "single layer"
    lyr = 0

    w_in_bf = w_in[lyr].astype(BF16)
    bd768 = bd512 = _block_diag_ones(SEG_TILE)
    gq = jnp.tile(q_norm_g[lyr], ATT_WIDTH // HEAD_DIM)[None]
    gk = jnp.tile(k_norm_g[lyr], ATT_WIDTH // HEAD_DIM)[None]
    row1 = lambda a: a.reshape(1, -1)
    prep_w = (row1(rwkv_mu[lyr]), row1(rwkv_w0[lyr]), row1(rwkv_a0[lyr]),
              *_split_bf16(_pad_rows(rwkv_w2[lyr], 0, DECAY_LORA + ICLR_LORA)),
              *_split_bf16(_pad_rows(rwkv_a2[lyr], DECAY_LORA, DECAY_LORA + ICLR_LORA)),
              *_split_bf16(rwkv_g2[lyr]),
              row1(rwkv_k_k[lyr]), row1(rwkv_k_a[lyr]), row1(rwkv_r_k[lyr]), bd512)
    merge_w = (row1(rwkv_gn_g[lyr]), row1(rwkv_gn_b[lyr]), bd512, w_attn_out[lyr].astype(BF16),
               w_rwkv_out[lyr].astype(BF16), w_o[lyr].astype(BF16))
    ffn_w = (row1(ln2_g[lyr]), w_up[lyr].astype(BF16), conv_w[lyr], row1(conv_b[lyr]), w_down[lyr].astype(BF16))

    xs = x_sample.reshape(nsamp, D_MODEL)
    qs, ks, vs, prs, gates_s = _proj(xs, row1(ln1_g[lyr]), w_in_bf, bd768, gq, gk, tm=nsamp)
    caches = [c[lyr].transpose(0, 2, 3, 1).reshape(nsamp, GROUP_WIDTH, c.shape[2])
              for c in (cache_k_w128, cache_v_w128, cache_k_w512, cache_v_w512, cache_k_w2048, cache_v_w2048)]
    for g, (win, dil) in enumerate(ATT_GROUPS):
        assert caches[2 * g].shape[2] == BLOCK * dil, "window caches must be full"
    ks_t, vs_t = ks.T, vs.T
    cache_args = lambda g: (qs, ks, vs, ks_t, vs_t, caches[2 * g], caches[2 * g + 1])
    big, mid = N_GROUPS - 1, N_GROUPS - 2
    cache_done = {}

    xp = x_prompt.reshape(batch * seq, D_MODEL)
    keep_max = min(max(w for w, _ in ATT_GROUPS), seq)
    first_half = _proj(xp, row1(ln1_g[lyr]), w_in_bf, bd768, gq, gk, tm=PROJ_ROWS,
                       prep=(prep_w, batch, seq, keep_max),
                       cache_rider=(mid, cache_args(mid)))
    qn, kn, v, pr_last, gates = first_half[:5]
    na, dec, nb, kp, rp, rv, vkr, bonus, gg = first_half[5:5 + N_PREP_OUT]
    k_tail, v_tail = first_half[5 + N_PREP_OUT:5 + N_SEQ_OUT]
    cache_done[mid] = first_half[5 + N_SEQ_OUT:]
    att = [_attn_prompt(qn, kn, v, g, batch, seq) for g in range(N_GROUPS)]
    yw, s_fin = _wkv_seq(dec, nb, kp, na, rp, rv)
    merge_args = (xp, *[a[0] for a in att], *[a[1] for a in att], yw, vkr, bonus, gg, gates, *merge_w)
    ffn_rows = seq * batch // nsamp
    yp, ust, *cache_done[big] = _merge_ffn_seq(merge_args, ffn_w, batch, seq, tm=ffn_rows,
                                               cache_rider=(big, cache_args(big)))

    y_prompt = yp.reshape(batch, seq, D_MODEL)
    kv_prompt = []
    for g, (win, _) in enumerate(ATT_GROUPS):
        keep = min(win, seq)
        for tail in (k_tail, v_tail):
            c = tail[:, g * GROUP_WIDTH:(g + 1) * GROUP_WIDTH, keep_max - keep:]
            kv_prompt.append(c.reshape(batch, HEADS_PER_GROUP, HEAD_DIM, keep).transpose(0, 3, 1, 2)[None])
    shift_prompt = pr_last[None, :, -1]
    wkv_prompt = (s_fin.reshape(RWKV_HEAD, WKV_SUB, WKV_Q, batch, RWKV_HEADS)
                  .transpose(3, 4, 2, 1, 0).reshape(1, batch, RWKV_HEADS, RWKV_HEAD, RWKV_HEAD))
    conv_prompt = ust[None, :, 6:8]

    kv_sample, o_s, lse_s = [], [], []
    for g, (win, dil) in enumerate(ATT_GROUPS):
        k_out, v_out, o_g, lse_g = (cache_done[g] if g in cache_done
                                    else _cache_attn(cache_args(g), g, bb=SMALL_CACHE_BLOCK))
        kv_sample += [c.reshape(nsamp, HEADS_PER_GROUP, HEAD_DIM, win).transpose(0, 3, 1, 2)[None]
                      for c in (k_out, v_out)]
        o_s.append(o_g)
        lse_s.append(lse_g)
    na, dec, nb, kp, rp, rv, vkr, bonus, gg = _prep_step(prs, state_rwkv_shift[lyr], prep_w, tm=nsamp)
    state_t = state_rwkv_wkv[lyr].transpose(1, 2, 3, 0)
    yw_s, state_new = _wkv_step(state_t, dec, nb, kp, na, rp, rv)
    x2s = _merge(xs, o_s, lse_s, yw_s[None], vkr, bonus, gg, gates_s, *merge_w, tm=nsamp)
    conv_prev = state_ffn_conv[lyr]
    ys, u_s = _ffn_step(x2s, (conv_prev[:, 0], conv_prev[:, 1]), *ffn_w, tm=nsamp)

    y_sample = ys.reshape(nsamp, 1, D_MODEL)
    shift_sample = prs[None]
    wkv_sample = state_new.transpose(3, 0, 1, 2)[None]
    conv_sample = jnp.stack([conv_prev[:, 1], u_s], axis=1)[None]

    return (y_prompt, y_sample,
            kv_prompt[0], kv_sample[0], kv_prompt[1], kv_sample[1],
            kv_prompt[2], kv_sample[2], kv_prompt[3], kv_sample[3],
            kv_prompt[4], kv_sample[4], kv_prompt[5], kv_sample[5],
            shift_prompt, shift_sample, wkv_prompt, wkv_sample, conv_prompt, conv_sample)
```

```python
import functools

import jax
import jax.numpy as jnp
from jax import lax
from jax.experimental import pallas as pl
from jax.experimental.pallas import tpu as pltpu

F32 = jnp.float32
BF16 = jnp.bfloat16

D_MODEL = 1024
HEAD_DIM = 64
ATT_GROUPS = ((128, 1), (512, 4), (2048, 16))
N_GROUPS = len(ATT_GROUPS)
HEADS_PER_GROUP = 4
GROUP_WIDTH = HEADS_PER_GROUP * HEAD_DIM
ATT_WIDTH = N_GROUPS * GROUP_WIDTH
BLOCK = 128
SEG = 2048
RWKV_HEADS = 8
RWKV_HEAD = 64
RWKV_WIDTH = RWKV_HEADS * RWKV_HEAD
DECAY_LORA = 64
ICLR_LORA = 64
GATE_LORA = 128
RWKV_PROJ = 3 * RWKV_WIDTH + DECAY_LORA + ICLR_LORA + GATE_LORA
LORA_OFF = 3 * RWKV_WIDTH
GATE_OFF = LORA_OFF + DECAY_LORA + ICLR_LORA
PR_OFF = 3 * ATT_WIDTH
GATES_OFF = PR_OFF + RWKV_PROJ
P_TOTAL = GATES_OFF + 2 * D_MODEL
D_FF = 2816
FF_CHUNK = 256
RMS_EPS = 1e-6
GN_EPS = 64e-5
L2_EPS = 1e-12
NEG_BIG = -1e30

LANES = 128
SEG_TILE = 256
WKV_Q = 4
WKV_SUB = RWKV_HEAD // WKV_Q
WKV_TB = 128
WKV_GROUP = 8
VMEM_LIMIT = 56 * 1024 * 1024
PROJ_ROWS = 256
SMALL_CACHE_BLOCK = 8


def _cparams(*sem):
    return pltpu.CompilerParams(dimension_semantics=sem, vmem_limit_bytes=VMEM_LIMIT)


def _resident(shape):
    n = len(shape)
    return pl.BlockSpec(shape, lambda *_: (0,) * n, pipeline_mode=pl.Buffered(1))


def _split_bf16(x):
    hi = x.astype(BF16)
    lo = (x - hi.astype(F32)).astype(BF16)
    return hi, lo


def _dot(a, b):
    return jnp.dot(a, b, preferred_element_type=F32)


def _dot3(a, b_hi, b_lo):
    a_hi, a_lo = _split_bf16(a)
    return _dot(a_hi, b_hi) + _dot(a_lo, b_hi) + _dot(a_hi, b_lo)


def _seg_sum_bf16(x_bf, ones_bd):
    return jnp.concatenate([_dot(x_bf[:, c:c + SEG_TILE], ones_bd) for c in range(0, x_bf.shape[1], SEG_TILE)],
                           axis=1)


def _seg_sum(x, ones_bd):
    hi, lo = _split_bf16(x)
    return _seg_sum_bf16(hi, ones_bd) + _seg_sum_bf16(lo, ones_bd)


def _sigmoid(x):
    return 1.0 / (1.0 + jnp.exp(-x))


def _block_diag_ones(width):
    idx = jnp.arange(width) // HEAD_DIM
    return (idx[:, None] == idx[None, :]).astype(BF16)


N_PREP_W, N_PREP_OUT = 13, 9


N_SEQ_OUT = N_PREP_OUT + 2


def _proj_kernel(*refs, cache_group, cache_bb, tiles_per_seq):
    with_prep = tiles_per_seq is not None
    x_ref, g1_ref, w_ref, bd_ref, gq_ref, gk_ref = refs[:6]
    n_w = 6 + (N_PREP_W if with_prep else 0)
    n_in = n_w + (N_CACHE_IN if cache_group is not None else 0)
    q_ref, k_ref, v_ref, pr_ref, gate_ref = refs[n_in:n_in + 5]
    n_out = n_in + 5 + (N_SEQ_OUT if with_prep else 0)
    if with_prep:
        carry_ref = refs[-1]

        @pl.when(pl.program_id(0) % tiles_per_seq == 0)
        def _():
            carry_ref[...] = jnp.zeros_like(carry_ref)

    if cache_group is not None:
        cache_refs = refs[n_w:n_in] + refs[n_out:n_out + N_CACHE_OUT]
        _cache_attn_body(cache_refs, pl.program_id(0) * cache_bb, cache_group, cache_bb)
    x = x_ref[...]
    ms = jnp.mean(x * x, axis=-1, keepdims=True)
    xn = (x * lax.rsqrt(ms + RMS_EPS) * g1_ref[...]).astype(BF16)

    def proj(c0, c1):
        return _dot(xn, w_ref[:, c0:c1])

    def head_norm(z, g):
        ss = _seg_sum_bf16((z * z).astype(BF16), bd_ref[...])
        return z * lax.rsqrt(ss * (1.0 / HEAD_DIM) + RMS_EPS) * g

    q_ref[...] = head_norm(proj(0, ATT_WIDTH), gq_ref[...])
    k = head_norm(proj(ATT_WIDTH, 2 * ATT_WIDTH), gk_ref[...])
    v = proj(2 * ATT_WIDTH, PR_OFF)
    k_ref[...] = k
    v_ref[...] = v
    gate_ref[...] = proj(GATES_OFF, P_TOTAL).astype(gate_ref.dtype)
    pr = proj(PR_OFF, GATES_OFF)
    if not with_prep:
        pr_ref[...] = pr
        return
    kt_ref, vt_ref = refs[n_out - 2:n_out]
    kt_ref[...] = k.T
    vt_ref[...] = v.T
    tm = pr.shape[0]
    row = lax.broadcasted_iota(jnp.int32, pr.shape, 0)
    prev = jnp.where(row == 0, carry_ref[7:8, :], pltpu.roll(pr, 1, 0))
    carry_ref[...] = pr[tm - 8:tm, :]
    pr_ref[...] = pr[tm - 8:tm, :]
    _prep_math(pr, prev, *refs[6:n_w], refs[n_in + 5:n_in + 5 + N_PREP_OUT])


def _proj(x2d, ln1_g, w_in_bf, bd768, gq, gk, tm, prep=None, cache_rider=None):
    m = x2d.shape[0]
    row = lambda w: pl.BlockSpec((tm, w), lambda i: (i, 0))
    widths = (ATT_WIDTH, ATT_WIDTH, ATT_WIDTH, RWKV_PROJ, 2 * D_MODEL)
    in_specs = [row(D_MODEL), _resident((1, D_MODEL)), _resident((D_MODEL, P_TOTAL)),
                _resident((SEG_TILE, SEG_TILE)), _resident((1, ATT_WIDTH)), _resident((1, ATT_WIDTH))]
    out_specs = [row(w) for w in widths]
    out_shape = [jax.ShapeDtypeStruct((m, w), BF16 if i == len(widths) - 1 else F32) for i, w in enumerate(widths)]
    prep_w, tiles_per_seq, scratch = (), None, []
    if prep is not None:
        prep_w, batch, seq, keep = prep
        assert keep % tm == 0 and seq % tm == 0
        tiles_per_seq = seq // tm
        in_specs += [_resident(w.shape) for w in prep_w]
        out_specs[3] = pl.BlockSpec((None, 8, RWKV_PROJ), lambda i: (i // tiles_per_seq, 0, 0))
        out_shape[3] = jax.ShapeDtypeStruct((batch, 8, RWKV_PROJ), F32)
        chan = pl.BlockSpec((None, RWKV_WIDTH, tm), lambda i: (i // tiles_per_seq, 0, i % tiles_per_seq))
        out_specs += [chan] * 6 + [row(RWKV_WIDTH)] * 3
        out_shape += [jax.ShapeDtypeStruct((batch, RWKV_WIDTH, seq), F32)] * 6
        out_shape += [jax.ShapeDtypeStruct((m, RWKV_WIDTH), F32)] * 3
        first = (seq - keep) // tm
        tail = pl.BlockSpec((None, ATT_WIDTH, tm),
                            lambda i: (i // tiles_per_seq, 0, jnp.maximum(i % tiles_per_seq - first, 0)))
        out_specs += [tail] * 2
        out_shape += [jax.ShapeDtypeStruct((batch, ATT_WIDTH, keep), F32)] * 2
        scratch = [pltpu.VMEM((8, RWKV_PROJ), F32)]
    cache_group, cache_args, cache_bb = None, (), 0
    if cache_rider is not None:
        cache_group, cache_args = cache_rider
        cache_bb = cache_args[5].shape[0] * tm // m
        assert cache_bb * (m // tm) == cache_args[5].shape[0]
        c_in, c_out, c_shape = _cache_attn_specs(cache_args, cache_bb, lambda i: i)
        in_specs, out_specs, out_shape = in_specs + c_in, out_specs + c_out, out_shape + c_shape
    return pl.pallas_call(
        functools.partial(_proj_kernel, cache_group=cache_group, cache_bb=cache_bb, tiles_per_seq=tiles_per_seq),
        out_shape=tuple(out_shape),
        grid=(m // tm,),
        in_specs=in_specs,
        out_specs=tuple(out_specs),
        scratch_shapes=scratch,
        compiler_params=_cparams("arbitrary"),
        name="proj",
    )(x2d, ln1_g, w_in_bf, bd768, gq, gk, *prep_w, *cache_args)


def _attn_prompt_kernel(*refs, dil):
    q_ref, kp_ref, kc_ref, vp_ref, vc_ref = (refs[2 * i:2 * i + 2] for i in range(5))
    o_ref, lse_ref = refs[10], refs[11]
    not_first_seg = pl.program_id(1) > 0
    nblk = SEG // (BLOCK * dil)
    stacked = (HEADS_PER_GROUP * BLOCK, 2 * BLOCK)
    row = lax.broadcasted_iota(jnp.int32, stacked, 0) % BLOCK
    col = lax.broadcasted_iota(jnp.int32, stacked, 1)
    in_prev = col < BLOCK
    band_prev = in_prev & (col >= row)
    band_cur = jnp.logical_not(in_prev) & (col - BLOCK <= row)
    bias_cur = jnp.where(band_cur, 0.0, NEG_BIG)
    bias_both = jnp.where(band_cur | band_prev, 0.0, NEG_BIG)
    lane = lax.broadcasted_iota(jnp.int32, (1, GROUP_WIDTH), 1)
    head_masks = [(lane // HEAD_DIM) == h for h in range(HEADS_PER_GROUP)]

    def row_slice(start):
        return pl.ds(start, BLOCK) if dil == 1 else pl.ds(start, BLOCK, stride=dil)

    def rows(halves, start):
        return jnp.concatenate([h[row_slice(start), :] for h in halves], axis=1)

    def block(q_start, k1_ref, k1_start, v1_ref, prev_valid):
        q = rows(q_ref, q_start) * (HEAD_DIM ** -0.5)
        q4 = jnp.concatenate([jnp.where(hm, q, 0.0) for hm in head_masks], axis=0).astype(BF16)
        kt = jnp.concatenate([rows(k1_ref, k1_start), rows(kc_ref, q_start)], axis=0).astype(BF16)
        vt = jnp.concatenate([rows(v1_ref, k1_start), rows(vc_ref, q_start)], axis=0).astype(BF16)
        s = lax.dot_general(q4, kt, (((1,), (1,)), ((), ())), preferred_element_type=F32)
        s = s + (bias_both if prev_valid is True else jnp.where(prev_valid, bias_both, bias_cur))
        m = jnp.max(s, axis=-1, keepdims=True)
        p = jnp.exp(s - m)
        l = jnp.sum(p, axis=-1, keepdims=True)
        ov = _dot(p.astype(BF16), vt)
        lse = m + jnp.log(l)
        o = jnp.zeros((BLOCK, GROUP_WIDTH), F32)
        l_b = jnp.ones((BLOCK, GROUP_WIDTH), F32)
        lse_b = jnp.zeros((BLOCK, GROUP_WIDTH), F32)
        for h, hm in enumerate(head_masks):
            sl = slice(h * BLOCK, (h + 1) * BLOCK)
            o = jnp.where(hm, ov[sl], o)
            l_b = jnp.where(hm, l[sl], l_b)
            lse_b = jnp.where(hm, lse[sl], lse_b)
        o = o / l_b
        for half in range(2):
            lanes = slice(half * LANES, (half + 1) * LANES)
            o_ref[half, row_slice(q_start), :] = o[:, lanes]
            lse_ref[half, row_slice(q_start), :] = lse_b[:, lanes]

    def first_blocks(r, carry):
        block(r, kp_ref, SEG - BLOCK * dil + r, vp_ref, not_first_seg)
        return carry

    def later_blocks(i, carry):
        r = i % dil
        nb = 1 + i // dil
        q_start = nb * BLOCK * dil + r
        block(q_start, kc_ref, q_start - BLOCK * dil, vc_ref, True)
        return carry

    lax.fori_loop(0, dil, first_blocks, 0, unroll=min(8, dil))
    if nblk > 1:
        n_later = dil * (nblk - 1)
        lax.fori_loop(0, n_later, later_blocks, 0, unroll=max(u for u in (3, 4, 5) if n_later % u == 0))


def _attn_prompt(qn, kn, v, group, batch, seq):
    dil = ATT_GROUPS[group][1]
    nseg = seq // SEG
    halves = GROUP_WIDTH // LANES
    cur = [pl.BlockSpec((SEG, LANES), lambda b, s, c=halves * group + h: (b * nseg + s, c)) for h in range(halves)]
    prev = [pl.BlockSpec((SEG, LANES), lambda b, s, c=halves * group + h: (b * nseg + jnp.maximum(s - 1, 0), c))
            for h in range(halves)]
    out = pl.BlockSpec((halves, SEG, LANES), lambda b, s: (0, b * nseg + s, 0))
    shape = jax.ShapeDtypeStruct((halves, batch * seq, LANES), F32)
    return pl.pallas_call(
        functools.partial(_attn_prompt_kernel, dil=dil),
        out_shape=(shape, shape),
        grid=(batch, nseg),
        in_specs=cur + prev + cur + prev + cur,
        out_specs=(out, out),
        compiler_params=_cparams("parallel", "arbitrary"),
        name=f"attn_prompt_g{group}",
    )(qn, qn, kn, kn, kn, kn, v, v, v, v)


def _cache_attn_body(refs, first, group, bb):
    q_ref, kn_ref, vn_ref, knt_ref, vnt_ref, kc_ref, vc_ref, ko_ref, vo_ref, o_ref, lse_ref = refs
    dil = ATT_GROUPS[group][1]
    win = kc_ref.shape[2]
    nseq = q_ref.shape[0]
    cols = slice(group * GROUP_WIDTH, (group + 1) * GROUP_WIDTH)
    pos = lax.broadcasted_iota(jnp.int32, (1, win), 1)
    in_window = (pos % dil) == 0
    newest = lax.broadcasted_iota(jnp.int32, (GROUP_WIDTH, LANES), 1) == LANES - 1
    head_rows = (lax.broadcasted_iota(jnp.int32, (8, GROUP_WIDTH), 0)
                 == lax.broadcasted_iota(jnp.int32, (8, GROUP_WIDTH), 1) // HEAD_DIM)
    seq_lane = lax.broadcasted_iota(jnp.int32, (GROUP_WIDTH, nseq), 1)

    def hi_lo_rows(x):
        hi, lo = _split_bf16(x)
        return jnp.concatenate([hi, lo], axis=0)

    def fold(x):
        return x[0:8] + x[8:16]

    def shifted(cache, new_t_ref, n):
        new_col = jnp.sum(jnp.where(seq_lane == n, new_t_ref[cols, :], 0.0), axis=1, keepdims=True)
        rolled = pltpu.roll(cache, win - 1, 1)
        last_tile = jnp.where(newest, new_col, rolled[:, win - LANES:])
        return last_tile if win == LANES else jnp.concatenate([rolled[:, :win - LANES], last_tile], axis=1)

    for b in range(bb):
        n = first + b
        q = q_ref[pl.ds(n, 1), cols] * (HEAD_DIM ** -0.5)
        k_new = kn_ref[pl.ds(n, 1), cols]
        v_new = vn_ref[pl.ds(n, 1), cols]
        q8 = jnp.where(head_rows, q, 0.0)
        kc = kc_ref[b]
        vc = vc_ref[b]
        s = fold(_dot(hi_lo_rows(q8), kc.astype(BF16)))
        s = jnp.where(in_window, s, NEG_BIG)
        s_new = jnp.sum(q8 * k_new, axis=1, keepdims=True)
        m = jnp.maximum(jnp.max(s, axis=1, keepdims=True), s_new)
        p = jnp.exp(s - m)
        p_new = jnp.exp(s_new - m)
        l = jnp.sum(p, axis=1, keepdims=True) + p_new
        pv = fold(lax.dot_general(hi_lo_rows(p), vc.astype(BF16), (((1,), (1,)), ((), ())),
                                  preferred_element_type=F32))
        ov = (pv + p_new * v_new) / l
        o = jnp.sum(jnp.where(head_rows, ov, 0.0), axis=0, keepdims=True)
        lse = jnp.sum(jnp.where(head_rows, m + jnp.log(l), 0.0), axis=0, keepdims=True)
        for half in range(GROUP_WIDTH // LANES):
            lanes = slice(half * LANES, (half + 1) * LANES)
            o_ref[half, pl.ds(n, 1), :] = o[:, lanes]
            lse_ref[half, pl.ds(n, 1), :] = lse[:, lanes]
        ko_ref[b] = shifted(kc, knt_ref, n)
        vo_ref[b] = shifted(vc, vnt_ref, n)


N_CACHE_IN, N_CACHE_OUT = 7, 4


def _cache_attn_specs(args, bb, block_of):
    n, _, win = args[5].shape
    halves = GROUP_WIDTH // LANES
    cache = pl.BlockSpec((bb, GROUP_WIDTH, win), lambda *g: (block_of(*g), 0, 0))
    rows = _resident((n, ATT_WIDTH))
    rows_t = _resident((ATT_WIDTH, n))
    out = pl.BlockSpec((halves, n, LANES), lambda *g: (0, 0, 0))
    cache_shape = jax.ShapeDtypeStruct(args[5].shape, F32)
    o_shape = jax.ShapeDtypeStruct((halves, n, LANES), F32)
    return [rows] * 3 + [rows_t] * 2 + [cache] * 2, [cache, cache, out, out], [cache_shape] * 2 + [o_shape] * 2


def _cache_attn_kernel(*refs, group, bb):
    _cache_attn_body(refs, pl.program_id(0) * bb, group, bb)


def _cache_attn(args, group, bb):
    in_specs, out_specs, out_shape = _cache_attn_specs(args, bb, lambda i: i)
    return pl.pallas_call(
        functools.partial(_cache_attn_kernel, group=group, bb=bb),
        out_shape=tuple(out_shape),
        grid=(args[5].shape[0] // bb,),
        in_specs=in_specs,
        out_specs=tuple(out_specs),
        compiler_params=_cparams("arbitrary"),
        name=f"cache_attn_g{group}",
    )(*args)


def _prep_math(p, prev, mu_ref, w0_ref, a0_ref, w2h_ref, w2l_ref, a2h_ref, a2l_ref, g2h_ref, g2l_ref,
               kk_ref, ka_ref, rk_ref, bd_ref, outs):
    na_ref, dec_ref, nb_ref, kp_ref, rp_ref, v_ref, vkr_ref, bonus_ref, g_ref = outs
    bd = bd_ref[...]

    def put(ref, val):
        ref[...] = val.T

    xs = p + mu_ref[...] * (prev - p)
    r = xs[:, 0:RWKV_WIDTH]
    k = xs[:, RWKV_WIDTH:2 * RWKV_WIDTH]
    v = xs[:, 2 * RWKV_WIDTH:3 * RWKV_WIDTH]
    lora = xs[:, LORA_OFF:GATE_OFF]
    gl = xs[:, GATE_OFF:RWKV_PROJ]
    z = w0_ref[...] + _dot3(jnp.tanh(lora), w2h_ref[...], w2l_ref[...])
    softplus_neg = jnp.maximum(-z, 0.0) + jnp.log(1.0 + jnp.exp(-jnp.abs(z)))
    w = -softplus_neg - 0.5
    dec = jnp.exp(-jnp.exp(w))
    a = _sigmoid(a0_ref[...] + _dot3(lora, a2h_ref[...], a2l_ref[...]))
    g = _dot3(_sigmoid(gl), g2h_ref[...], g2l_ref[...])
    kk = k * kk_ref[...]
    kk = kk / jnp.maximum(jnp.sqrt(_seg_sum(kk * kk, bd)), L2_EPS)
    kp = k * (1.0 + (a - 1.0) * ka_ref[...])
    na = -kk
    nb = kk * a
    rp = dec * r + na * _seg_sum(nb * r, bd)
    put(na_ref, na)
    put(dec_ref, dec)
    put(nb_ref, nb)
    put(kp_ref, kp)
    put(rp_ref, rp)
    put(v_ref, v)
    vkr_ref[...] = v * _seg_sum(kp * r, bd)
    bonus_ref[...] = _seg_sum(r * kp * rk_ref[...], bd) * v
    g_ref[...] = g


def _prep_step_kernel(p_ref, prev_ref, *rest):
    weights, outs = rest[:N_PREP_W], rest[N_PREP_W:N_PREP_W + N_PREP_OUT]
    _prep_math(p_ref[...], prev_ref[...], *weights, outs)


def _prep_step(pr, shift_prev, weights, tm):
    m = pr.shape[0]
    w_specs = [_resident(w.shape) for w in weights]
    out_shape = tuple(jax.ShapeDtypeStruct((m, RWKV_WIDTH), F32) for _ in range(9))
    row = lambda w: pl.BlockSpec((tm, w), lambda i: (i, 0))
    chan = pl.BlockSpec((RWKV_WIDTH, tm), lambda i: (0, i))
    chan_shape = jax.ShapeDtypeStruct((RWKV_WIDTH, m), F32)
    return pl.pallas_call(
        _prep_step_kernel, out_shape=(chan_shape,) * 6 + out_shape[6:], grid=(m // tm,),
        in_specs=[row(RWKV_PROJ), row(RWKV_PROJ)] + w_specs,
        out_specs=(chan,) * 6 + tuple(row(RWKV_WIDTH) for _ in range(3)),
        compiler_params=_cparams("parallel"), name="rwkv_prep_step",
    )(pr, shift_prev, *weights)


def _wkv_seq_kernel(w_ref, nb_ref, k_ref, na_ref, rp_ref, v_ref, y_ref, sout_ref,
                    s_ref, e_ref, first_ref, stage_ref, vt_ref, yt_ref):
    @pl.when(pl.program_id(0) == 0)
    def _():
        s_ref[...] = jnp.zeros_like(s_ref)

    nbatch = w_ref.shape[0]
    tb = w_ref.shape[2]
    zero = jnp.zeros((WKV_SUB, LANES), F32)

    def head_rows(ref, b, c):
        return ref[b, pl.ds(c, RWKV_HEADS, stride=RWKV_HEAD), :]

    def time_major(ref, j):
        rows = jnp.concatenate([head_rows(ref, b, j) for b in range(nbatch)], axis=0)
        return jnp.concatenate([rows] * WKV_Q, axis=0).T

    unroll = 8

    def build_e(jg, carry):
        for j in [jg * unroll + u for u in range(unroll)]:
            for o, ref in enumerate((w_ref, nb_ref, k_ref)):
                e_ref[o, j] = time_major(ref, j)
            for o, ref in ((3, na_ref), (4, rp_ref)):
                rows = time_major(ref, j)
                first_ref[o - 3, j] = rows[0:WKV_GROUP]
                e_ref[o, j] = pltpu.roll(rows, tb - 1, 0)
        return carry

    def build_v(ig, carry):
        for ih in [ig * unroll + u for u in range(unroll)]:
            rows = jnp.concatenate([head_rows(v_ref, b, il * WKV_SUB + ih)
                                    for il in range(WKV_Q) for b in range(nbatch)], axis=0)
            vt_ref[pl.ds(ih, tb, stride=WKV_SUB), :] = rows.T
        return carry

    lax.fori_loop(0, RWKV_HEAD // unroll, build_e, 0)
    lax.fori_loop(0, WKV_SUB // unroll, build_v, 0)

    n_ops = 5
    pairs = [(o, j) for o in range(n_ops) for j in range(RWKV_HEAD)]
    per_step = len(pairs) // WKV_GROUP

    def stage_tiles(slot, group, which):
        rows = pl.ds(pl.multiple_of(jnp.minimum(group, tb // WKV_GROUP - 1) * WKV_GROUP, WKV_GROUP), WKV_GROUP)
        for o, j in which:
            stage_ref[slot, o, j] = e_ref[o, j, rows, :]

    def step(group, u, slot, carry):
        sa, y = carry
        t = group * WKV_GROUP + u
        tile = pl.ds(pl.multiple_of(t * WKV_SUB, WKV_SUB), WKV_SUB)
        yt_ref[tile, :] = y
        v = vt_ref[tile, :]
        stage_tiles(1 - slot, group + 1, pairs[u * per_step:(u + 1) * per_step])
        n_acc = 4
        sa_acc = [zero] * n_acc
        y_acc = [zero] * n_acc
        for j in range(RWKV_HEAD):
            row = lambda o: stage_ref[slot, o, j, u:u + 1, :]
            sn = s_ref[j] * row(0) + sa * row(1) + v * row(2)
            s_ref[j] = sn
            sa_acc[j % n_acc] = sa_acc[j % n_acc] + sn * row(3)
            y_acc[j % n_acc] = y_acc[j % n_acc] + sn * row(4)
        return (sa_acc[0] + sa_acc[1]) + (sa_acc[2] + sa_acc[3]), (y_acc[0] + y_acc[1]) + (y_acc[2] + y_acc[3])

    region = 1

    def two_groups(gg, carry):
        for slot in range(2):
            for u0 in range(0, WKV_GROUP, region):
                def do(c, slot=slot, u0=u0):
                    for u in range(u0, u0 + region):
                        c = step(2 * gg + slot, u, slot, c)
                    return c
                carry = do(carry) if (slot, u0) == (0, 0) else lax.cond(gg >= 0, do, lambda c: c, carry)
        return carry

    def first_matvecs():
        n_acc = 4
        sa_acc = [zero] * n_acc
        y_acc = [zero] * n_acc
        for j in range(RWKV_HEAD):
            sj = s_ref[j]
            sa_acc[j % n_acc] = sa_acc[j % n_acc] + sj * first_ref[0, j, 0:1, :]
            y_acc[j % n_acc] = y_acc[j % n_acc] + sj * first_ref[1, j, 0:1, :]
        return (sa_acc[0] + sa_acc[1]) + (sa_acc[2] + sa_acc[3]), (y_acc[0] + y_acc[1]) + (y_acc[2] + y_acc[3])

    stage_tiles(0, 0, pairs)
    lax.fori_loop(0, tb // (2 * WKV_GROUP), two_groups, first_matvecs())

    def emit_y(ig, carry):
        for ih in [ig * unroll + u for u in range(unroll)]:
            cols = yt_ref[pl.ds(ih, tb, stride=WKV_SUB), :].T
            for il in range(WKV_Q):
                for b in range(nbatch):
                    r0 = il * (LANES // WKV_Q) + b * RWKV_HEADS
                    y_ref[b, pl.ds(il * WKV_SUB + ih, RWKV_HEADS, stride=RWKV_HEAD), :] = cols[r0:r0 + RWKV_HEADS]
        return carry

    lax.fori_loop(0, WKV_SUB // unroll, emit_y, 0)

    @pl.when(pl.program_id(0) == pl.num_programs(0) - 1)
    def _():
        sout_ref[...] = s_ref[...]


def _wkv_seq(w, nb, k, na, rp, v):
    batch, _, t = w.shape
    blk = pl.BlockSpec((batch, RWKV_WIDTH, WKV_TB), lambda i: (0, 0, i))
    state = pl.BlockSpec((RWKV_HEAD, WKV_SUB, LANES), lambda i: (0, 0, 0))
    return pl.pallas_call(
        _wkv_seq_kernel,
        out_shape=(jax.ShapeDtypeStruct((batch, RWKV_WIDTH, t), F32),
                   jax.ShapeDtypeStruct((RWKV_HEAD, WKV_SUB, LANES), F32)),
        grid=(t // WKV_TB,),
        in_specs=[blk] * 6,
        out_specs=(blk, state),
        scratch_shapes=[pltpu.VMEM((RWKV_HEAD, WKV_SUB, LANES), F32),
                        pltpu.VMEM((5, RWKV_HEAD, WKV_TB, LANES), F32),
                        pltpu.VMEM((2, RWKV_HEAD, WKV_GROUP, LANES), F32),
                        pltpu.VMEM((2, 5, RWKV_HEAD, WKV_GROUP, LANES), F32),
                        pltpu.VMEM((WKV_TB * WKV_SUB, LANES), F32),
                        pltpu.VMEM((WKV_TB * WKV_SUB, LANES), F32)],
        compiler_params=_cparams("arbitrary"),
        name="wkv_seq",
    )(w, nb, k, na, rp, v)


def _wkv_step_kernel(s_ref, w_ref, nb_ref, k_ref, na_ref, rp_ref, v_ref, y_ref, sout_ref):
    w, nb, k, na, rp = w_ref[...], nb_ref[...], k_ref[...], na_ref[...], rp_ref[...]
    for i in range(RWKV_HEAD):
        si = s_ref[i]
        sa = jnp.sum(si * na, axis=0, keepdims=True)
        y_ref[i:i + 1, :] = jnp.sum(si * rp, axis=0, keepdims=True)
        sout_ref[i] = si * w + sa * nb + v_ref[i:i + 1, :] * k


def _wkv_step(state, w, nb, k, na, rp, v):
    n = state.shape[-1]
    st = pl.BlockSpec((None, RWKV_HEAD, RWKV_HEAD, n), lambda h: (h, 0, 0, 0))
    vec = pl.BlockSpec((RWKV_HEAD, n), lambda h: (h, 0))
    return pl.pallas_call(
        _wkv_step_kernel,
        out_shape=(jax.ShapeDtypeStruct((RWKV_WIDTH, n), F32), jax.ShapeDtypeStruct(state.shape, F32)),
        grid=(RWKV_HEADS,),
        in_specs=[st] + [vec] * 6,
        out_specs=(vec, st),
        compiler_params=_cparams("parallel"),
        name="wkv_step",
    )(state, w, nb, k, na, rp, v)


N_MERGE_IN = 18


def _merge_tile(refs, yw_channels_first):
    (x_ref, o0_ref, o1_ref, o2_ref, l0_ref, l1_ref, l2_ref, yw_ref, vkr_ref, bonus_ref, g_ref,
     gate_ref, gng_ref, gnb_ref, bd_ref, wa_ref, wr_ref, wo_ref) = refs
    wide = lambda ref: jnp.concatenate([ref[0], ref[1]], axis=1)
    l0, l1, l2 = wide(l0_ref), wide(l1_ref), wide(l2_ref)
    lm = jnp.maximum(jnp.maximum(l0, l1), l2)
    e0, e1, e2 = jnp.exp(l0 - lm), jnp.exp(l1 - lm), jnp.exp(l2 - lm)
    o_att = (e0 * wide(o0_ref) + e1 * wide(o1_ref) + e2 * wide(o2_ref)) / (e0 + e1 + e2)
    y_att = _dot(o_att.astype(BF16), wa_ref[...])

    bd = bd_ref[...]
    y = (yw_ref[...].T if yw_channels_first else yw_ref[...]) + vkr_ref[...]
    mean = _seg_sum(y, bd) * (1.0 / RWKV_HEAD)
    yc = y - mean
    var = _seg_sum(yc * yc, bd) * (1.0 / RWKV_HEAD)
    yn = yc * lax.rsqrt(var + GN_EPS) * gng_ref[...] + gnb_ref[...]
    o_rwkv = (yn + bonus_ref[...]) * g_ref[...]
    y_rwkv = _dot(o_rwkv.astype(BF16), wr_ref[...])

    gate_att = gate_ref[:, 0:D_MODEL].astype(F32)
    gate_rwkv = gate_ref[:, D_MODEL:2 * D_MODEL].astype(F32)
    merged = _sigmoid(gate_att) * y_att + _sigmoid(gate_rwkv) * y_rwkv
    return x_ref[...] + _dot(merged.astype(BF16), wo_ref[...])


def _merge_kernel(*refs, yw_channels_first):
    refs[N_MERGE_IN][...] = _merge_tile(refs[:N_MERGE_IN], yw_channels_first)


def _merge_specs(yw, tm, tile_of):
    row = lambda w: pl.BlockSpec((tm, w), lambda *g: (tile_of(*g), 0))
    if yw.ndim == 3:
        nt = yw.shape[2] // tm
        yw_spec = pl.BlockSpec((None, RWKV_WIDTH, tm), lambda *g: (tile_of(*g) // nt, 0, tile_of(*g) % nt))
    else:
        yw_spec = row(RWKV_WIDTH)
    halves = pl.BlockSpec((GROUP_WIDTH // LANES, tm, LANES), lambda *g: (0, tile_of(*g), 0))
    return ([row(D_MODEL)] + [halves] * 6 + [yw_spec] + [row(RWKV_WIDTH)] * 3 + [row(2 * D_MODEL)]
            + [_resident((1, RWKV_WIDTH)), _resident((1, RWKV_WIDTH)), _resident((SEG_TILE, SEG_TILE)),
               _resident((GROUP_WIDTH, D_MODEL)), _resident((RWKV_WIDTH, D_MODEL)), _resident((D_MODEL, D_MODEL))])


def _merge(x2d, o_groups, lse_groups, yw, vkr, bonus, g, gates, gn_g, gn_b, bd512, wa, wr, wo, tm):
    m = x2d.shape[0]
    return pl.pallas_call(
        functools.partial(_merge_kernel, yw_channels_first=yw.ndim == 3),
        out_shape=jax.ShapeDtypeStruct((m, D_MODEL), F32),
        grid=(m // tm,),
        in_specs=_merge_specs(yw, tm, lambda i: i),
        out_specs=pl.BlockSpec((tm, D_MODEL), lambda i: (i, 0)),
        compiler_params=_cparams("parallel"),
        name="merge",
    )(x2d, *o_groups, *lse_groups, yw, vkr, bonus, g, gates, gn_g, gn_b, bd512, wa, wr, wo)


def _ffn_body(x, g2_ref, wup_ref, cw_ref, cb_ref, wdn_ref, out_ref, u_prev, u_sink):
    ms = jnp.mean(x * x, axis=-1, keepdims=True)
    xn = (x * lax.rsqrt(ms + RMS_EPS) * g2_ref[...]).astype(BF16)

    def up(c):
        return tuple(_dot(xn, wup_ref[:, c0:c0 + FF_CHUNK]) for c0 in (c, D_FF + c))

    def conv(u, c0):
        c1 = c0 + FF_CHUNK
        u_m2, u_m1 = u_prev(u, c0, c1)
        u_sink(u, c0)
        return cb_ref[:, c0:c1] + u_m2 * cw_ref[0:1, c0:c1] + u_m1 * cw_ref[1:2, c0:c1] + u * cw_ref[2:3, c0:c1]

    hidden = []
    nxt = up(0)
    for c in range(0, D_FF, FF_CHUNK):
        ug, uv = nxt
        if c + FF_CHUNK < D_FF:
            nxt = up(c + FF_CHUNK)
        gate = conv(ug, c)
        val = conv(uv, D_FF + c)
        hidden.append((gate * _sigmoid(gate) * val).astype(BF16))
    out_ref[...] = x + _dot(jnp.concatenate(hidden, axis=1), wdn_ref[...])


def _ffn_seq_kernel(*refs, cache_group, yw_channels_first, tiles_per_seq, n_tiles):
    g2_ref, wup_ref, cw_ref, cb_ref, wdn_ref = refs[N_MERGE_IN:N_MERGE_IN + 5]
    n_in = N_MERGE_IN + 5 + (N_CACHE_IN if cache_group is not None else 0)
    out_ref, ust_ref = refs[n_in:n_in + 2]
    carry_ref, x_ref = refs[-2:]
    step = pl.program_id(0)
    ffn_tile = jnp.maximum(step - 1, 0)

    @pl.when(step == 0)
    def _():
        x_ref[...] = jnp.zeros_like(x_ref)

    @pl.when(ffn_tile % tiles_per_seq == 0)
    def _():
        carry_ref[...] = jnp.zeros_like(carry_ref)

    if cache_group is not None:
        _cache_attn_body(refs[N_MERGE_IN + 5:n_in] + refs[n_in + 2:-2], jnp.minimum(step, n_tiles - 1),
                         cache_group, bb=1)

    x = x_ref[...]
    x_ref[...] = _merge_tile(refs[:N_MERGE_IN], yw_channels_first)
    tm = x.shape[0]
    row = lax.broadcasted_iota(jnp.int32, (tm, FF_CHUNK), 0)

    def u_prev(u, c0, c1):
        c6 = carry_ref[6:7, c0:c1]
        c7 = carry_ref[7:8, c0:c1]
        u_m1 = jnp.where(row == 0, c7, pltpu.roll(u, 1, 0))
        u_m2 = jnp.where(row == 0, c6, jnp.where(row == 1, c7, pltpu.roll(u, 2, 0)))
        return u_m2, u_m1

    def u_sink(u, c0):
        carry_ref[:, c0:c0 + FF_CHUNK] = u[tm - 8:tm, :]

    _ffn_body(x, g2_ref, wup_ref, cw_ref, cb_ref, wdn_ref, out_ref, u_prev, u_sink)
    ust_ref[...] = carry_ref[...]


def _ffn_step_kernel(x_ref, p0_ref, p1_ref, g2_ref, wup_ref, cw_ref, cb_ref, wdn_ref, out_ref, u_ref):
    def u_prev(u, c0, c1):
        return p0_ref[:, c0:c1], p1_ref[:, c0:c1]

    def u_sink(u, c0):
        u_ref[:, c0:c0 + FF_CHUNK] = u

    _ffn_body(x_ref[...], g2_ref, wup_ref, cw_ref, cb_ref, wdn_ref, out_ref, u_prev, u_sink)


_FFN_W_SPECS = lambda: [_resident((1, D_MODEL)), _resident((D_MODEL, 2 * D_FF)), _resident((3, 2 * D_FF)),
                        _resident((1, 2 * D_FF)), _resident((D_FF, D_MODEL))]


def _merge_ffn_seq(merge_args, ffn_w, batch, seq, tm, cache_rider=None):
    m = merge_args[0].shape[0]
    nt = seq // tm
    n_tiles = batch * nt
    merge_tile = lambda s: jnp.minimum(s, n_tiles - 1)
    ffn_tile = lambda s: jnp.maximum(s - 1, 0)
    yw = merge_args[7]
    in_specs = _merge_specs(yw, tm, merge_tile) + _FFN_W_SPECS()
    out_specs = [pl.BlockSpec((tm, D_MODEL), lambda s: (ffn_tile(s), 0)),
                 pl.BlockSpec((None, 8, 2 * D_FF), lambda s: (ffn_tile(s) // nt, 0, 0))]
    out_shape = [jax.ShapeDtypeStruct((m, D_MODEL), F32), jax.ShapeDtypeStruct((batch, 8, 2 * D_FF), F32)]
    cache_group, cache_args = cache_rider if cache_rider is not None else (None, ())
    if cache_rider is not None:
        assert cache_args[5].shape[0] == n_tiles, "one sample sequence per row tile"
        c_in, c_out, c_shape = _cache_attn_specs(cache_args, 1, merge_tile)
        in_specs, out_specs, out_shape = in_specs + c_in, out_specs + c_out, out_shape + c_shape
    return pl.pallas_call(
        functools.partial(_ffn_seq_kernel, cache_group=cache_group, yw_channels_first=yw.ndim == 3,
                          tiles_per_seq=nt, n_tiles=n_tiles),
        out_shape=tuple(out_shape),
        grid=(n_tiles + 1,),
        in_specs=in_specs,
        out_specs=tuple(out_specs),
        scratch_shapes=[pltpu.VMEM((8, 2 * D_FF), F32), pltpu.VMEM((tm, D_MODEL), F32)],
        compiler_params=_cparams("arbitrary"), name="merge_ffn_seq",
    )(*merge_args, *ffn_w, *cache_args)


def _ffn_step(x2d, conv_prev, ln2_g, wup, cw, cb, wdn, tm):
    m = x2d.shape[0]
    w_specs = _FFN_W_SPECS()
    row = lambda w: pl.BlockSpec((tm, w), lambda i: (i, 0))
    return pl.pallas_call(
        _ffn_step_kernel,
        out_shape=(jax.ShapeDtypeStruct((m, D_MODEL), F32), jax.ShapeDtypeStruct((m, 2 * D_FF), F32)),
        grid=(m // tm,),
        in_specs=[row(D_MODEL), row(2 * D_FF), row(2 * D_FF)] + w_specs,
        out_specs=(row(D_MODEL), row(2 * D_FF)),
        compiler_params=_cparams("parallel"), name="ffn_step",
    )(x2d, conv_prev[0], conv_prev[1], ln2_g, wup, cw, cb, wdn)


def _pad_rows(w, before, total):
    return jnp.zeros((total, w.shape[1]), w.dtype).at[before:before + w.shape[0]].set(w)


def kernel(x_prompt, x_sample, cache_k_w128, cache_v_w128, cache_k_w512, cache_v_w512, cache_k_w2048, cache_v_w2048,
           state_rwkv_shift, state_rwkv_wkv, state_ffn_conv, ln1_g, w_in, q_norm_g, k_norm_g, w_attn_out,
           rwkv_mu, rwkv_w0, rwkv_w2, rwkv_a0, rwkv_a2, rwkv_g2, rwkv_k_k, rwkv_k_a, rwkv_r_k, rwkv_gn_g, rwkv_gn_b,
           w_rwkv_out, w_o, ln2_g, w_up, conv_w, conv_b, w_down):
    batch, seq, _ = x_prompt.shape
    nsamp = x_sample.shape[0]
    assert seq % SEG == 0 and x_sample.shape[1] == 1 and batch * RWKV_HEADS * WKV_Q == LANES
    assert ln1_g.shape[0] == 1, "single layer"
    lyr = 0

    w_in_bf = w_in[lyr].astype(BF16)
    bd768 = bd512 = _block_diag_ones(SEG_TILE)
    gq = jnp.tile(q_norm_g[lyr], ATT_WIDTH // HEAD_DIM)[None]
    gk = jnp.tile(k_norm_g[lyr], ATT_WIDTH // HEAD_DIM)[None]
    row1 = lambda a: a.reshape(1, -1)
    prep_w = (row1(rwkv_mu[lyr]), row1(rwkv_w0[lyr]), row1(rwkv_a0[lyr]),
              *_split_bf16(_pad_rows(rwkv_w2[lyr], 0, DECAY_LORA + ICLR_LORA)),
              *_split_bf16(_pad_rows(rwkv_a2[lyr], DECAY_LORA, DECAY_LORA + ICLR_LORA)),
              *_split_bf16(rwkv_g2[lyr]),
              row1(rwkv_k_k[lyr]), row1(rwkv_k_a[lyr]), row1(rwkv_r_k[lyr]), bd512)
    merge_w = (row1(rwkv_gn_g[lyr]), row1(rwkv_gn_b[lyr]), bd512, w_attn_out[lyr].astype(BF16),
               w_rwkv_out[lyr].astype(BF16), w_o[lyr].astype(BF16))
    ffn_w = (row1(ln2_g[lyr]), w_up[lyr].astype(BF16), conv_w[lyr], row1(conv_b[lyr]), w_down[lyr].astype(BF16))

    xs = x_sample.reshape(nsamp, D_MODEL)
    qs, ks, vs, prs, gates_s = _proj(xs, row1(ln1_g[lyr]), w_in_bf, bd768, gq, gk, tm=nsamp)
    caches = [c[lyr].transpose(0, 2, 3, 1).reshape(nsamp, GROUP_WIDTH, c.shape[2])
              for c in (cache_k_w128, cache_v_w128, cache_k_w512, cache_v_w512, cache_k_w2048, cache_v_w2048)]
    for g, (win, dil) in enumerate(ATT_GROUPS):
        assert caches[2 * g].shape[2] == BLOCK * dil, "window caches must be full"
    ks_t, vs_t = ks.T, vs.T
    cache_args = lambda g: (qs, ks, vs, ks_t, vs_t, caches[2 * g], caches[2 * g + 1])
    big, mid = N_GROUPS - 1, N_GROUPS - 2
    cache_done = {}

    xp = x_prompt.reshape(batch * seq, D_MODEL)
    keep_max = min(max(w for w, _ in ATT_GROUPS), seq)
    first_half = _proj(xp, row1(ln1_g[lyr]), w_in_bf, bd768, gq, gk, tm=PROJ_ROWS,
                       prep=(prep_w, batch, seq, keep_max),
                       cache_rider=(mid, cache_args(mid)))
    qn, kn, v, pr_last, gates = first_half[:5]
    na, dec, nb, kp, rp, rv, vkr, bonus, gg = first_half[5:5 + N_PREP_OUT]
    k_tail, v_tail = first_half[5 + N_PREP_OUT:5 + N_SEQ_OUT]
    cache_done[mid] = first_half[5 + N_SEQ_OUT:]
    att = [_attn_prompt(qn, kn, v, g, batch, seq) for g in range(N_GROUPS)]
    yw, s_fin = _wkv_seq(dec, nb, kp, na, rp, rv)
    merge_args = (xp, *[a[0] for a in att], *[a[1] for a in att], yw, vkr, bonus, gg, gates, *merge_w)
    ffn_rows = seq * batch // nsamp
    yp, ust, *cache_done[big] = _merge_ffn_seq(merge_args, ffn_w, batch, seq, tm=ffn_rows,
                                               cache_rider=(big, cache_args(big)))

    y_prompt = yp.reshape(batch, seq, D_MODEL)
    kv_prompt = []
    for g, (win, _) in enumerate(ATT_GROUPS):
        keep = min(win, seq)
        for tail in (k_tail, v_tail):
            c = tail[:, g * GROUP_WIDTH:(g + 1) * GROUP_WIDTH, keep_max - keep:]
            kv_prompt.append(c.reshape(batch, HEADS_PER_GROUP, HEAD_DIM, keep).transpose(0, 3, 1, 2)[None])
    shift_prompt = pr_last[None, :, -1]
    wkv_prompt = (s_fin.reshape(RWKV_HEAD, WKV_SUB, WKV_Q, batch, RWKV_HEADS)
                  .transpose(3, 4, 2, 1, 0).reshape(1, batch, RWKV_HEADS, RWKV_HEAD, RWKV_HEAD))
    conv_prompt = ust[None, :, 6:8]

    kv_sample, o_s, lse_s = [], [], []
    for g, (win, dil) in enumerate(ATT_GROUPS):
        k_out, v_out, o_g, lse_g = (cache_done[g] if g in cache_done
                                    else _cache_attn(cache_args(g), g, bb=SMALL_CACHE_BLOCK))
        kv_sample += [c.reshape(nsamp, HEADS_PER_GROUP, HEAD_DIM, win).transpose(0, 3, 1, 2)[None]
                      for c in (k_out, v_out)]
        o_s.append(o_g)
        lse_s.append(lse_g)
    na, dec, nb, kp, rp, rv, vkr, bonus, gg = _prep_step(prs, state_rwkv_shift[lyr], prep_w, tm=nsamp)
    state_t = state_rwkv_wkv[lyr].transpose(1, 2, 3, 0)
    yw_s, state_new = _wkv_step(state_t, dec, nb, kp, na, rp, rv)
    x2s = _merge(xs, o_s, lse_s, yw_s[None], vkr, bonus, gg, gates_s, *merge_w, tm=nsamp)
    conv_prev = state_ffn_conv[lyr]
    ys, u_s = _ffn_step(x2s, (conv_prev[:, 0], conv_prev[:, 1]), *ffn_w, tm=nsamp)

    y_sample = ys.reshape(nsamp, 1, D_MODEL)
    shift_sample = prs[None]
    wkv_sample = state_new.transpose(3, 0, 1, 2)[None]
    conv_sample = jnp.stack([conv_prev[:, 1], u_s], axis=1)[None]

    return (y_prompt, y_sample,
            kv_prompt[0], kv_sample[0], kv_prompt[1], kv_sample[1],
            kv_prompt[2], kv_sample[2], kv_prompt[3], kv_sample[3],
            kv_prompt[4], kv_sample[4], kv_prompt[5], kv_sample[5],
            shift_prompt, shift_sample, wkv_prompt, wkv_sample, conv_prompt, conv_sample)
```

```python
import functools

import jax
import jax.numpy as jnp
from jax import lax
from jax.experimental import pallas as pl
from jax.experimental.pallas import tpu as pltpu

F32 = jnp.float32
BF16 = jnp.bfloat16

D_MODEL = 1024
HEAD_DIM = 64
ATT_GROUPS = ((128, 1), (512, 4), (2048, 16))
N_GROUPS = len(ATT_GROUPS)
HEADS_PER_GROUP = 4
GROUP_WIDTH = HEADS_PER_GROUP * HEAD_DIM
ATT_WIDTH = N_GROUPS * GROUP_WIDTH
BLOCK = 128
SEG = 2048
RWKV_HEADS = 8
RWKV_HEAD = 64
RWKV_WIDTH = RWKV_HEADS * RWKV_HEAD
DECAY_LORA = 64
ICLR_LORA = 64
GATE_LORA = 128
RWKV_PROJ = 3 * RWKV_WIDTH + DECAY_LORA + ICLR_LORA + GATE_LORA
LORA_OFF = 3 * RWKV_WIDTH
GATE_OFF = LORA_OFF + DECAY_LORA + ICLR_LORA
PR_OFF = 3 * ATT_WIDTH
GATES_OFF = PR_OFF + RWKV_PROJ
P_TOTAL = GATES_OFF + 2 * D_MODEL
D_FF = 2816
FF_CHUNK = 256
RMS_EPS = 1e-6
GN_EPS = 64e-5
L2_EPS = 1e-12
NEG_BIG = -1e30

LANES = 128
SEG_TILE = 256
WKV_Q = 4
WKV_SUB = RWKV_HEAD // WKV_Q
WKV_TB = 128
WKV_GROUP = 8
VMEM_LIMIT = 56 * 1024 * 1024
PROJ_ROWS = 256
SMALL_CACHE_BLOCK = 8


def _cparams(*sem):
    return pltpu.CompilerParams(dimension_semantics=sem, vmem_limit_bytes=VMEM_LIMIT)


def _resident(shape):
    n = len(shape)
    return pl.BlockSpec(shape, lambda *_: (0,) * n, pipeline_mode=pl.Buffered(1))


def _split_bf16(x):
    hi = x.astype(BF16)
    lo = (x - hi.astype(F32)).astype(BF16)
    return hi, lo


def _dot(a, b):
    return jnp.dot(a, b, preferred_element_type=F32)


def _dot3(a, b_hi, b_lo):
    a_hi, a_lo = _split_bf16(a)
    return _dot(jnp.concatenate([a_hi, a_lo, a_hi], axis=1), jnp.concatenate([b_hi, b_hi, b_lo], axis=0))


def _seg_sum_bf16(x_bf, ones_bd):
    return jnp.concatenate([_dot(x_bf[:, c:c + SEG_TILE], ones_bd) for c in range(0, x_bf.shape[1], SEG_TILE)],
                           axis=1)


def _seg_sum(x, ones_bd):
    hi, lo = _split_bf16(x)
    ones2 = jnp.concatenate([ones_bd, ones_bd], axis=0)
    return jnp.concatenate([_dot(jnp.concatenate([hi[:, c:c + SEG_TILE], lo[:, c:c + SEG_TILE]], axis=1), ones2)
                            for c in range(0, x.shape[1], SEG_TILE)], axis=1)


def _sigmoid(x):
    return 1.0 / (1.0 + jnp.exp(-x))


def _block_diag_ones(width):
    idx = jnp.arange(width) // HEAD_DIM
    return (idx[:, None] == idx[None, :]).astype(BF16)


N_PREP_W, N_PREP_OUT = 13, 9


N_SEQ_OUT = N_PREP_OUT + 2


def _proj_kernel(*refs, cache_group, cache_bb, tiles_per_seq):
    with_prep = tiles_per_seq is not None
    x_ref, g1_ref, w_ref, bd_ref, gq_ref, gk_ref = refs[:6]
    n_w = 6 + (N_PREP_W if with_prep else 0)
    n_in = n_w + (N_CACHE_IN if cache_group is not None else 0)
    q_ref, k_ref, v_ref, pr_ref, gate_ref = refs[n_in:n_in + 5]
    n_out = n_in + 5 + (N_SEQ_OUT if with_prep else 0)
    if with_prep:
        carry_ref = refs[-1]

        @pl.when(pl.program_id(0) % tiles_per_seq == 0)
        def _():
            carry_ref[...] = jnp.zeros_like(carry_ref)

    if cache_group is not None:
        cache_refs = refs[n_w:n_in] + refs[n_out:n_out + N_CACHE_OUT]
        _cache_attn_body(cache_refs, pl.program_id(0) * cache_bb, cache_group, cache_bb)
    x = x_ref[...]
    ms = jnp.mean(x * x, axis=-1, keepdims=True)
    xn = (x * lax.rsqrt(ms + RMS_EPS) * g1_ref[...]).astype(BF16)

    def proj(c0, c1):
        return _dot(xn, w_ref[:, c0:c1])

    def head_norm(z, g):
        ss = _seg_sum_bf16((z * z).astype(BF16), bd_ref[...])
        return z * lax.rsqrt(ss * (1.0 / HEAD_DIM) + RMS_EPS) * g

    q_ref[...] = head_norm(proj(0, ATT_WIDTH), gq_ref[...])
    k = head_norm(proj(ATT_WIDTH, 2 * ATT_WIDTH), gk_ref[...])
    v = proj(2 * ATT_WIDTH, PR_OFF)
    k_ref[...] = k
    v_ref[...] = v
    gate_ref[...] = proj(GATES_OFF, P_TOTAL).astype(gate_ref.dtype)
    pr = proj(PR_OFF, GATES_OFF)
    if not with_prep:
        pr_ref[...] = pr
        return
    kt_ref, vt_ref = refs[n_out - 2:n_out]
    kt_ref[...] = k.T
    vt_ref[...] = v.T
    tm = pr.shape[0]
    row = lax.broadcasted_iota(jnp.int32, pr.shape, 0)
    prev = jnp.where(row == 0, carry_ref[7:8, :], pltpu.roll(pr, 1, 0))
    carry_ref[...] = pr[tm - 8:tm, :]
    pr_ref[...] = pr[tm - 8:tm, :]
    _prep_math(pr, prev, *refs[6:n_w], refs[n_in + 5:n_in + 5 + N_PREP_OUT])


def _proj(x2d, ln1_g, w_in_bf, bd768, gq, gk, tm, prep=None, cache_rider=None):
    m = x2d.shape[0]
    row = lambda w: pl.BlockSpec((tm, w), lambda i: (i, 0))
    widths = (ATT_WIDTH, ATT_WIDTH, ATT_WIDTH, RWKV_PROJ, 2 * D_MODEL)
    in_specs = [row(D_MODEL), _resident((1, D_MODEL)), _resident((D_MODEL, P_TOTAL)),
                _resident((SEG_TILE, SEG_TILE)), _resident((1, ATT_WIDTH)), _resident((1, ATT_WIDTH))]
    out_specs = [row(w) for w in widths]
    out_shape = [jax.ShapeDtypeStruct((m, w), BF16 if i == len(widths) - 1 else F32) for i, w in enumerate(widths)]
    prep_w, tiles_per_seq, scratch = (), None, []
    if prep is not None:
        prep_w, batch, seq, keep = prep
        assert keep % tm == 0 and seq % tm == 0
        tiles_per_seq = seq // tm
        in_specs += [_resident(w.shape) for w in prep_w]
        out_specs[3] = pl.BlockSpec((None, 8, RWKV_PROJ), lambda i: (i // tiles_per_seq, 0, 0))
        out_shape[3] = jax.ShapeDtypeStruct((batch, 8, RWKV_PROJ), F32)
        chan = pl.BlockSpec((None, RWKV_WIDTH, tm), lambda i: (i // tiles_per_seq, 0, i % tiles_per_seq))
        out_specs += [chan] * 6 + [row(RWKV_WIDTH)] * 3
        out_shape += [jax.ShapeDtypeStruct((batch, RWKV_WIDTH, seq), F32)] * 6
        out_shape += [jax.ShapeDtypeStruct((m, RWKV_WIDTH), F32)] * 3
        first = (seq - keep) // tm
        tail = pl.BlockSpec((None, ATT_WIDTH, tm),
                            lambda i: (i // tiles_per_seq, 0, jnp.maximum(i % tiles_per_seq - first, 0)))
        out_specs += [tail] * 2
        out_shape += [jax.ShapeDtypeStruct((batch, ATT_WIDTH, keep), F32)] * 2
        scratch = [pltpu.VMEM((8, RWKV_PROJ), F32)]
    cache_group, cache_args, cache_bb = None, (), 0
    if cache_rider is not None:
        cache_group, cache_args = cache_rider
        cache_bb = cache_args[5].shape[0] * tm // m
        assert cache_bb * (m // tm) == cache_args[5].shape[0]
        c_in, c_out, c_shape = _cache_attn_specs(cache_args, cache_bb, lambda i: i)
        in_specs, out_specs, out_shape = in_specs + c_in, out_specs + c_out, out_shape + c_shape
    return pl.pallas_call(
        functools.partial(_proj_kernel, cache_group=cache_group, cache_bb=cache_bb, tiles_per_seq=tiles_per_seq),
        out_shape=tuple(out_shape),
        grid=(m // tm,),
        in_specs=in_specs,
        out_specs=tuple(out_specs),
        scratch_shapes=scratch,
        compiler_params=_cparams("arbitrary"),
        name="proj",
    )(x2d, ln1_g, w_in_bf, bd768, gq, gk, *prep_w, *cache_args)


def _attn_prompt_kernel(*refs, dil):
    q_ref, kp_ref, kc_ref, vp_ref, vc_ref = (refs[2 * i:2 * i + 2] for i in range(5))
    o_ref, lse_ref = refs[10], refs[11]
    not_first_seg = pl.program_id(1) > 0
    nblk = SEG // (BLOCK * dil)
    stacked = (HEADS_PER_GROUP * BLOCK, 2 * BLOCK)
    row = lax.broadcasted_iota(jnp.int32, stacked, 0) % BLOCK
    col = lax.broadcasted_iota(jnp.int32, stacked, 1)
    in_prev = col < BLOCK
    band_prev = in_prev & (col >= row)
    band_cur = jnp.logical_not(in_prev) & (col - BLOCK <= row)
    bias_cur = jnp.where(band_cur, 0.0, NEG_BIG)
    bias_both = jnp.where(band_cur | band_prev, 0.0, NEG_BIG)
    lane = lax.broadcasted_iota(jnp.int32, (1, GROUP_WIDTH), 1)
    head_masks = [(lane // HEAD_DIM) == h for h in range(HEADS_PER_GROUP)]

    def row_slice(start):
        return pl.ds(start, BLOCK) if dil == 1 else pl.ds(start, BLOCK, stride=dil)

    def rows(halves, start):
        return jnp.concatenate([h[row_slice(start), :] for h in halves], axis=1)

    def block(q_start, k1_ref, k1_start, v1_ref, prev_valid):
        q = rows(q_ref, q_start) * (HEAD_DIM ** -0.5)
        q4 = jnp.concatenate([jnp.where(hm, q, 0.0) for hm in head_masks], axis=0).astype(BF16)
        kt = jnp.concatenate([rows(k1_ref, k1_start), rows(kc_ref, q_start)], axis=0).astype(BF16)
        vt = jnp.concatenate([rows(v1_ref, k1_start), rows(vc_ref, q_start)], axis=0).astype(BF16)
        s = lax.dot_general(q4, kt, (((1,), (1,)), ((), ())), preferred_element_type=F32)
        s = s + (bias_both if prev_valid is True else jnp.where(prev_valid, bias_both, bias_cur))
        m = jnp.max(s, axis=-1, keepdims=True)
        p = jnp.exp(s - m)
        l = jnp.sum(p, axis=-1, keepdims=True)
        ov = _dot(p.astype(BF16), vt)
        lse = m + jnp.log(l)
        o = jnp.zeros((BLOCK, GROUP_WIDTH), F32)
        l_b = jnp.ones((BLOCK, GROUP_WIDTH), F32)
        lse_b = jnp.zeros((BLOCK, GROUP_WIDTH), F32)
        for h, hm in enumerate(head_masks):
            sl = slice(h * BLOCK, (h + 1) * BLOCK)
            o = jnp.where(hm, ov[sl], o)
            l_b = jnp.where(hm, l[sl], l_b)
            lse_b = jnp.where(hm, lse[sl], lse_b)
        o = o / l_b
        for half in range(2):
            lanes = slice(half * LANES, (half + 1) * LANES)
            o_ref[half, row_slice(q_start), :] = o[:, lanes]
            lse_ref[half, row_slice(q_start), :] = lse_b[:, lanes]

    def first_blocks(r, carry):
        block(r, kp_ref, SEG - BLOCK * dil + r, vp_ref, not_first_seg)
        return carry

    def later_blocks(i, carry):
        r = i % dil
        nb = 1 + i // dil
        q_start = nb * BLOCK * dil + r
        block(q_start, kc_ref, q_start - BLOCK * dil, vc_ref, True)
        return carry

    lax.fori_loop(0, dil, first_blocks, 0, unroll=min(8, dil))
    if nblk > 1:
        n_later = dil * (nblk - 1)
        lax.fori_loop(0, n_later, later_blocks, 0, unroll=max(u for u in (3, 4, 5) if n_later % u == 0))


def _attn_prompt(qn, kn, v, group, batch, seq):
    dil = ATT_GROUPS[group][1]
    nseg = seq // SEG
    halves = GROUP_WIDTH // LANES
    cur = [pl.BlockSpec((SEG, LANES), lambda b, s, c=halves * group + h: (b * nseg + s, c)) for h in range(halves)]
    prev = [pl.BlockSpec((SEG, LANES), lambda b, s, c=halves * group + h: (b * nseg + jnp.maximum(s - 1, 0), c))
            for h in range(halves)]
    out = pl.BlockSpec((halves, SEG, LANES), lambda b, s: (0, b * nseg + s, 0))
    shape = jax.ShapeDtypeStruct((halves, batch * seq, LANES), F32)
    return pl.pallas_call(
        functools.partial(_attn_prompt_kernel, dil=dil),
        out_shape=(shape, shape),
        grid=(batch, nseg),
        in_specs=cur + prev + cur + prev + cur,
        out_specs=(out, out),
        compiler_params=_cparams("parallel", "arbitrary"),
        name=f"attn_prompt_g{group}",
    )(qn, qn, kn, kn, kn, kn, v, v, v, v)


def _cache_attn_body(refs, first, group, bb):
    q_ref, kn_ref, vn_ref, knt_ref, vnt_ref, kc_ref, vc_ref, ko_ref, vo_ref, o_ref, lse_ref = refs
    dil = ATT_GROUPS[group][1]
    win = kc_ref.shape[2]
    nseq = q_ref.shape[0]
    cols = slice(group * GROUP_WIDTH, (group + 1) * GROUP_WIDTH)
    pos = lax.broadcasted_iota(jnp.int32, (1, win), 1)
    in_window = (pos % dil) == 0
    newest = lax.broadcasted_iota(jnp.int32, (GROUP_WIDTH, LANES), 1) == LANES - 1
    head_rows = (lax.broadcasted_iota(jnp.int32, (8, GROUP_WIDTH), 0)
                 == lax.broadcasted_iota(jnp.int32, (8, GROUP_WIDTH), 1) // HEAD_DIM)
    seq_lane = lax.broadcasted_iota(jnp.int32, (GROUP_WIDTH, nseq), 1)

    def hi_lo_rows(x):
        hi, lo = _split_bf16(x)
        return jnp.concatenate([hi, lo], axis=0)

    def fold(x):
        return x[0:8] + x[8:16]

    def shifted(cache, new_t_ref, n):
        new_col = jnp.sum(jnp.where(seq_lane == n, new_t_ref[cols, :], 0.0), axis=1, keepdims=True)
        rolled = pltpu.roll(cache, win - 1, 1)
        last_tile = jnp.where(newest, new_col, rolled[:, win - LANES:])
        return last_tile if win == LANES else jnp.concatenate([rolled[:, :win - LANES], last_tile], axis=1)

    for b in range(bb):
        n = first + b
        q = q_ref[pl.ds(n, 1), cols] * (HEAD_DIM ** -0.5)
        k_new = kn_ref[pl.ds(n, 1), cols]
        v_new = vn_ref[pl.ds(n, 1), cols]
        q8 = jnp.where(head_rows, q, 0.0)
        kc = kc_ref[b]
        vc = vc_ref[b]
        s = fold(_dot(hi_lo_rows(q8), kc.astype(BF16)))
        s = jnp.where(in_window, s, NEG_BIG)
        s_new = jnp.sum(q8 * k_new, axis=1, keepdims=True)
        m = jnp.maximum(jnp.max(s, axis=1, keepdims=True), s_new)
        p = jnp.exp(s - m)
        p_new = jnp.exp(s_new - m)
        l = jnp.sum(p, axis=1, keepdims=True) + p_new
        pv = fold(lax.dot_general(hi_lo_rows(p), vc.astype(BF16), (((1,), (1,)), ((), ())),
                                  preferred_element_type=F32))
        ov = (pv + p_new * v_new) / l
        o = jnp.sum(jnp.where(head_rows, ov, 0.0), axis=0, keepdims=True)
        lse = jnp.sum(jnp.where(head_rows, m + jnp.log(l), 0.0), axis=0, keepdims=True)
        for half in range(GROUP_WIDTH // LANES):
            lanes = slice(half * LANES, (half + 1) * LANES)
            o_ref[half, pl.ds(n, 1), :] = o[:, lanes]
            lse_ref[half, pl.ds(n, 1), :] = lse[:, lanes]
        ko_ref[b] = shifted(kc, knt_ref, n)
        vo_ref[b] = shifted(vc, vnt_ref, n)


N_CACHE_IN, N_CACHE_OUT = 7, 4


def _cache_attn_specs(args, bb, block_of):
    n, _, win = args[5].shape
    halves = GROUP_WIDTH // LANES
    cache = pl.BlockSpec((bb, GROUP_WIDTH, win), lambda *g: (block_of(*g), 0, 0))
    rows = _resident((n, ATT_WIDTH))
    rows_t = _resident((ATT_WIDTH, n))
    out = pl.BlockSpec((halves, n, LANES), lambda *g: (0, 0, 0))
    cache_shape = jax.ShapeDtypeStruct(args[5].shape, F32)
    o_shape = jax.ShapeDtypeStruct((halves, n, LANES), F32)
    return [rows] * 3 + [rows_t] * 2 + [cache] * 2, [cache, cache, out, out], [cache_shape] * 2 + [o_shape] * 2


def _cache_attn_kernel(*refs, group, bb):
    _cache_attn_body(refs, pl.program_id(0) * bb, group, bb)


def _cache_attn(args, group, bb):
    in_specs, out_specs, out_shape = _cache_attn_specs(args, bb, lambda i: i)
    return pl.pallas_call(
        functools.partial(_cache_attn_kernel, group=group, bb=bb),
        out_shape=tuple(out_shape),
        grid=(args[5].shape[0] // bb,),
        in_specs=in_specs,
        out_specs=tuple(out_specs),
        compiler_params=_cparams("arbitrary"),
        name=f"cache_attn_g{group}",
    )(*args)


def _prep_math(p, prev, mu_ref, w0_ref, a0_ref, w2h_ref, w2l_ref, a2h_ref, a2l_ref, g2h_ref, g2l_ref,
               kk_ref, ka_ref, rk_ref, bd_ref, outs):
    na_ref, dec_ref, nb_ref, kp_ref, rp_ref, v_ref, vkr_ref, bonus_ref, g_ref = outs
    bd = bd_ref[...]

    def put(ref, val):
        ref[...] = val.T

    xs = p + mu_ref[...] * (prev - p)
    r = xs[:, 0:RWKV_WIDTH]
    k = xs[:, RWKV_WIDTH:2 * RWKV_WIDTH]
    v = xs[:, 2 * RWKV_WIDTH:3 * RWKV_WIDTH]
    lora = xs[:, LORA_OFF:GATE_OFF]
    gl = xs[:, GATE_OFF:RWKV_PROJ]
    z = w0_ref[...] + _dot3(jnp.tanh(lora), w2h_ref[...], w2l_ref[...])
    softplus_neg = jnp.maximum(-z, 0.0) + jnp.log(1.0 + jnp.exp(-jnp.abs(z)))
    w = -softplus_neg - 0.5
    dec = jnp.exp(-jnp.exp(w))
    a = _sigmoid(a0_ref[...] + _dot3(lora, a2h_ref[...], a2l_ref[...]))
    g = _dot3(_sigmoid(gl), g2h_ref[...], g2l_ref[...])
    kk = k * kk_ref[...]
    kk = kk / jnp.maximum(jnp.sqrt(_seg_sum(kk * kk, bd)), L2_EPS)
    kp = k * (1.0 + (a - 1.0) * ka_ref[...])
    na = -kk
    nb = kk * a
    rp = dec * r + na * _seg_sum(nb * r, bd)
    put(na_ref, na)
    put(dec_ref, dec)
    put(nb_ref, nb)
    put(kp_ref, kp)
    put(rp_ref, rp)
    put(v_ref, v)
    vkr_ref[...] = v * _seg_sum(kp * r, bd)
    bonus_ref[...] = _seg_sum(r * kp * rk_ref[...], bd) * v
    g_ref[...] = g


def _prep_step_kernel(p_ref, prev_ref, *rest):
    weights, outs = rest[:N_PREP_W], rest[N_PREP_W:N_PREP_W + N_PREP_OUT]
    _prep_math(p_ref[...], prev_ref[...], *weights, outs)


def _prep_step(pr, shift_prev, weights, tm):
    m = pr.shape[0]
    w_specs = [_resident(w.shape) for w in weights]
    out_shape = tuple(jax.ShapeDtypeStruct((m, RWKV_WIDTH), F32) for _ in range(9))
    row = lambda w: pl.BlockSpec((tm, w), lambda i: (i, 0))
    chan = pl.BlockSpec((RWKV_WIDTH, tm), lambda i: (0, i))
    chan_shape = jax.ShapeDtypeStruct((RWKV_WIDTH, m), F32)
    return pl.pallas_call(
        _prep_step_kernel, out_shape=(chan_shape,) * 6 + out_shape[6:], grid=(m // tm,),
        in_specs=[row(RWKV_PROJ), row(RWKV_PROJ)] + w_specs,
        out_specs=(chan,) * 6 + tuple(row(RWKV_WIDTH) for _ in range(3)),
        compiler_params=_cparams("parallel"), name="rwkv_prep_step",
    )(pr, shift_prev, *weights)


def _wkv_seq_kernel(w_ref, nb_ref, k_ref, na_ref, rp_ref, v_ref, y_ref, sout_ref,
                    s_ref, e_ref, first_ref, stage_ref, vt_ref, yt_ref):
    @pl.when(pl.program_id(0) == 0)
    def _():
        s_ref[...] = jnp.zeros_like(s_ref)

    nbatch = w_ref.shape[0]
    tb = w_ref.shape[2]
    zero = jnp.zeros((WKV_SUB, LANES), F32)

    def head_rows(ref, b, c):
        return ref[b, pl.ds(c, RWKV_HEADS, stride=RWKV_HEAD), :]

    def time_major(ref, j):
        rows = jnp.concatenate([head_rows(ref, b, j) for b in range(nbatch)], axis=0)
        return jnp.concatenate([rows] * WKV_Q, axis=0).T

    unroll = 8

    def build_e(jg, carry):
        for j in [jg * unroll + u for u in range(unroll)]:
            for o, ref in enumerate((w_ref, nb_ref, k_ref)):
                e_ref[o, j] = time_major(ref, j)
            for o, ref in ((3, na_ref), (4, rp_ref)):
                rows = time_major(ref, j)
                first_ref[o - 3, j] = rows[0:WKV_GROUP]
                e_ref[o, j] = pltpu.roll(rows, tb - 1, 0)
        return carry

    def build_v(ig, carry):
        for ih in [ig * unroll + u for u in range(unroll)]:
            rows = jnp.concatenate([head_rows(v_ref, b, il * WKV_SUB + ih)
                                    for il in range(WKV_Q) for b in range(nbatch)], axis=0)
            vt_ref[pl.ds(ih, tb, stride=WKV_SUB), :] = rows.T
        return carry

    lax.fori_loop(0, RWKV_HEAD // unroll, build_e, 0)
    lax.fori_loop(0, WKV_SUB // unroll, build_v, 0)

    n_ops = 5
    pairs = [(o, j) for o in range(n_ops) for j in range(RWKV_HEAD)]
    per_step = len(pairs) // WKV_GROUP

    def stage_tiles(slot, group, which):
        rows = pl.ds(pl.multiple_of(jnp.minimum(group, tb // WKV_GROUP - 1) * WKV_GROUP, WKV_GROUP), WKV_GROUP)
        for o, j in which:
            stage_ref[slot, o, j] = e_ref[o, j, rows, :]

    def step(group, u, slot, carry):
        sa, y = carry
        t = group * WKV_GROUP + u
        tile = pl.ds(pl.multiple_of(t * WKV_SUB, WKV_SUB), WKV_SUB)
        yt_ref[tile, :] = y
        v = vt_ref[tile, :]
        stage_tiles(1 - slot, group + 1, pairs[u * per_step:(u + 1) * per_step])
        n_acc = 4
        sa_acc = [zero] * n_acc
        y_acc = [zero] * n_acc
        for j in range(RWKV_HEAD):
            row = lambda o: stage_ref[slot, o, j, u:u + 1, :]
            sn = s_ref[j] * row(0) + sa * row(1) + v * row(2)
            s_ref[j] = sn
            sa_acc[j % n_acc] = sa_acc[j % n_acc] + sn * row(3)
            y_acc[j % n_acc] = y_acc[j % n_acc] + sn * row(4)
        return (sa_acc[0] + sa_acc[1]) + (sa_acc[2] + sa_acc[3]), (y_acc[0] + y_acc[1]) + (y_acc[2] + y_acc[3])

    region = 1

    def two_groups(gg, carry):
        for slot in range(2):
            for u0 in range(0, WKV_GROUP, region):
                def do(c, slot=slot, u0=u0):
                    for u in range(u0, u0 + region):
                        c = step(2 * gg + slot, u, slot, c)
                    return c
                carry = do(carry) if (slot, u0) == (0, 0) else lax.cond(gg >= 0, do, lambda c: c, carry)
        return carry

    def first_matvecs():
        n_acc = 4
        sa_acc = [zero] * n_acc
        y_acc = [zero] * n_acc
        for j in range(RWKV_HEAD):
            sj = s_ref[j]
            sa_acc[j % n_acc] = sa_acc[j % n_acc] + sj * first_ref[0, j, 0:1, :]
            y_acc[j % n_acc] = y_acc[j % n_acc] + sj * first_ref[1, j, 0:1, :]
        return (sa_acc[0] + sa_acc[1]) + (sa_acc[2] + sa_acc[3]), (y_acc[0] + y_acc[1]) + (y_acc[2] + y_acc[3])

    stage_tiles(0, 0, pairs)
    lax.fori_loop(0, tb // (2 * WKV_GROUP), two_groups, first_matvecs())

    def emit_y(ig, carry):
        for ih in [ig * unroll + u for u in range(unroll)]:
            cols = yt_ref[pl.ds(ih, tb, stride=WKV_SUB), :].T
            for il in range(WKV_Q):
                for b in range(nbatch):
                    r0 = il * (LANES // WKV_Q) + b * RWKV_HEADS
                    y_ref[b, pl.ds(il * WKV_SUB + ih, RWKV_HEADS, stride=RWKV_HEAD), :] = cols[r0:r0 + RWKV_HEADS]
        return carry

    lax.fori_loop(0, WKV_SUB // unroll, emit_y, 0)

    @pl.when(pl.program_id(0) == pl.num_programs(0) - 1)
    def _():
        sout_ref[...] = s_ref[...]


def _wkv_seq(w, nb, k, na, rp, v):
    batch, _, t = w.shape
    blk = pl.BlockSpec((batch, RWKV_WIDTH, WKV_TB), lambda i: (0, 0, i))
    state = pl.BlockSpec((RWKV_HEAD, WKV_SUB, LANES), lambda i: (0, 0, 0))
    return pl.pallas_call(
        _wkv_seq_kernel,
        out_shape=(jax.ShapeDtypeStruct((batch, RWKV_WIDTH, t), F32),
                   jax.ShapeDtypeStruct((RWKV_HEAD, WKV_SUB, LANES), F32)),
        grid=(t // WKV_TB,),
        in_specs=[blk] * 6,
        out_specs=(blk, state),
        scratch_shapes=[pltpu.VMEM((RWKV_HEAD, WKV_SUB, LANES), F32),
                        pltpu.VMEM((5, RWKV_HEAD, WKV_TB, LANES), F32),
                        pltpu.VMEM((2, RWKV_HEAD, WKV_GROUP, LANES), F32),
                        pltpu.VMEM((2, 5, RWKV_HEAD, WKV_GROUP, LANES), F32),
                        pltpu.VMEM((WKV_TB * WKV_SUB, LANES), F32),
                        pltpu.VMEM((WKV_TB * WKV_SUB, LANES), F32)],
        compiler_params=_cparams("arbitrary"),
        name="wkv_seq",
    )(w, nb, k, na, rp, v)


def _wkv_step_kernel(s_ref, w_ref, nb_ref, k_ref, na_ref, rp_ref, v_ref, y_ref, sout_ref):
    w, nb, k, na, rp = w_ref[...], nb_ref[...], k_ref[...], na_ref[...], rp_ref[...]
    for i in range(RWKV_HEAD):
        si = s_ref[i]
        sa = jnp.sum(si * na, axis=0, keepdims=True)
        y_ref[i:i + 1, :] = jnp.sum(si * rp, axis=0, keepdims=True)
        sout_ref[i] = si * w + sa * nb + v_ref[i:i + 1, :] * k


def _wkv_step(state, w, nb, k, na, rp, v):
    n = state.shape[-1]
    st = pl.BlockSpec((None, RWKV_HEAD, RWKV_HEAD, n), lambda h: (h, 0, 0, 0))
    vec = pl.BlockSpec((RWKV_HEAD, n), lambda h: (h, 0))
    return pl.pallas_call(
        _wkv_step_kernel,
        out_shape=(jax.ShapeDtypeStruct((RWKV_WIDTH, n), F32), jax.ShapeDtypeStruct(state.shape, F32)),
        grid=(RWKV_HEADS,),
        in_specs=[st] + [vec] * 6,
        out_specs=(vec, st),
        compiler_params=_cparams("parallel"),
        name="wkv_step",
    )(state, w, nb, k, na, rp, v)


N_MERGE_IN = 18


def _merge_tile(refs, yw_channels_first):
    (x_ref, o0_ref, o1_ref, o2_ref, l0_ref, l1_ref, l2_ref, yw_ref, vkr_ref, bonus_ref, g_ref,
     gate_ref, gng_ref, gnb_ref, bd_ref, wa_ref, wr_ref, wo_ref) = refs
    wide = lambda ref: jnp.concatenate([ref[0], ref[1]], axis=1)
    l0, l1, l2 = wide(l0_ref), wide(l1_ref), wide(l2_ref)
    lm = jnp.maximum(jnp.maximum(l0, l1), l2)
    e0, e1, e2 = jnp.exp(l0 - lm), jnp.exp(l1 - lm), jnp.exp(l2 - lm)
    o_att = (e0 * wide(o0_ref) + e1 * wide(o1_ref) + e2 * wide(o2_ref)) / (e0 + e1 + e2)
    y_att = _dot(o_att.astype(BF16), wa_ref[...])

    bd = bd_ref[...]
    y = (yw_ref[...].T if yw_channels_first else yw_ref[...]) + vkr_ref[...]
    mean = _seg_sum(y, bd) * (1.0 / RWKV_HEAD)
    yc = y - mean
    var = _seg_sum(yc * yc, bd) * (1.0 / RWKV_HEAD)
    yn = yc * lax.rsqrt(var + GN_EPS) * gng_ref[...] + gnb_ref[...]
    o_rwkv = (yn + bonus_ref[...]) * g_ref[...]
    y_rwkv = _dot(o_rwkv.astype(BF16), wr_ref[...])

    gate_att = gate_ref[:, 0:D_MODEL].astype(F32)
    gate_rwkv = gate_ref[:, D_MODEL:2 * D_MODEL].astype(F32)
    merged = _sigmoid(gate_att) * y_att + _sigmoid(gate_rwkv) * y_rwkv
    return x_ref[...] + _dot(merged.astype(BF16), wo_ref[...])


def _merge_kernel(*refs, yw_channels_first):
    refs[N_MERGE_IN][...] = _merge_tile(refs[:N_MERGE_IN], yw_channels_first)


def _merge_specs(yw, tm, tile_of):
    row = lambda w: pl.BlockSpec((tm, w), lambda *g: (tile_of(*g), 0))
    if yw.ndim == 3:
        nt = yw.shape[2] // tm
        yw_spec = pl.BlockSpec((None, RWKV_WIDTH, tm), lambda *g: (tile_of(*g) // nt, 0, tile_of(*g) % nt))
    else:
        yw_spec = row(RWKV_WIDTH)
    halves = pl.BlockSpec((GROUP_WIDTH // LANES, tm, LANES), lambda *g: (0, tile_of(*g), 0))
    return ([row(D_MODEL)] + [halves] * 6 + [yw_spec] + [row(RWKV_WIDTH)] * 3 + [row(2 * D_MODEL)]
            + [_resident((1, RWKV_WIDTH)), _resident((1, RWKV_WIDTH)), _resident((SEG_TILE, SEG_TILE)),
               _resident((GROUP_WIDTH, D_MODEL)), _resident((RWKV_WIDTH, D_MODEL)), _resident((D_MODEL, D_MODEL))])


def _merge(x2d, o_groups, lse_groups, yw, vkr, bonus, g, gates, gn_g, gn_b, bd512, wa, wr, wo, tm):
    m = x2d.shape[0]
    return pl.pallas_call(
        functools.partial(_merge_kernel, yw_channels_first=yw.ndim == 3),
        out_shape=jax.ShapeDtypeStruct((m, D_MODEL), F32),
        grid=(m // tm,),
        in_specs=_merge_specs(yw, tm, lambda i: i),
        out_specs=pl.BlockSpec((tm, D_MODEL), lambda i: (i, 0)),
        compiler_params=_cparams("parallel"),
        name="merge",
    )(x2d, *o_groups, *lse_groups, yw, vkr, bonus, g, gates, gn_g, gn_b, bd512, wa, wr, wo)


def _ffn_body(x, g2_ref, wup_ref, cw_ref, cb_ref, wdn_ref, out_ref, u_prev, u_sink):
    ms = jnp.mean(x * x, axis=-1, keepdims=True)
    xn = (x * lax.rsqrt(ms + RMS_EPS) * g2_ref[...]).astype(BF16)

    def up(c):
        return tuple(_dot(xn, wup_ref[:, c0:c0 + FF_CHUNK]) for c0 in (c, D_FF + c))

    def conv(u, c0):
        c1 = c0 + FF_CHUNK
        u_m2, u_m1 = u_prev(u, c0, c1)
        u_sink(u, c0)
        return cb_ref[:, c0:c1] + u_m2 * cw_ref[0:1, c0:c1] + u_m1 * cw_ref[1:2, c0:c1] + u * cw_ref[2:3, c0:c1]

    hidden = []
    nxt = up(0)
    for c in range(0, D_FF, FF_CHUNK):
        ug, uv = nxt
        if c + FF_CHUNK < D_FF:
            nxt = up(c + FF_CHUNK)
        gate = conv(ug, c)
        val = conv(uv, D_FF + c)
        hidden.append((gate * _sigmoid(gate) * val).astype(BF16))
    out_ref[...] = x + _dot(jnp.concatenate(hidden, axis=1), wdn_ref[...])


def _ffn_seq_kernel(*refs, cache_group, yw_channels_first, tiles_per_seq, n_tiles):
    g2_ref, wup_ref, cw_ref, cb_ref, wdn_ref = refs[N_MERGE_IN:N_MERGE_IN + 5]
    n_in = N_MERGE_IN + 5 + (N_CACHE_IN if cache_group is not None else 0)
    out_ref, ust_ref = refs[n_in:n_in + 2]
    carry_ref, x_ref = refs[-2:]
    step = pl.program_id(0)
    ffn_tile = jnp.maximum(step - 1, 0)

    @pl.when(step == 0)
    def _():
        x_ref[...] = jnp.zeros_like(x_ref)

    @pl.when(ffn_tile % tiles_per_seq == 0)
    def _():
        carry_ref[...] = jnp.zeros_like(carry_ref)

    if cache_group is not None:
        _cache_attn_body(refs[N_MERGE_IN + 5:n_in] + refs[n_in + 2:-2], jnp.minimum(step, n_tiles - 1),
                         cache_group, bb=1)

    x = x_ref[...]
    x_ref[...] = _merge_tile(refs[:N_MERGE_IN], yw_channels_first)
    tm = x.shape[0]
    row = lax.broadcasted_iota(jnp.int32, (tm, FF_CHUNK), 0)

    def u_prev(u, c0, c1):
        c6 = carry_ref[6:7, c0:c1]
        c7 = carry_ref[7:8, c0:c1]
        u_m1 = jnp.where(row == 0, c7, pltpu.roll(u, 1, 0))
        u_m2 = jnp.where(row == 0, c6, jnp.where(row == 1, c7, pltpu.roll(u, 2, 0)))
        return u_m2, u_m1

    def u_sink(u, c0):
        carry_ref[:, c0:c0 + FF_CHUNK] = u[tm - 8:tm, :]

    _ffn_body(x, g2_ref, wup_ref, cw_ref, cb_ref, wdn_ref, out_ref, u_prev, u_sink)
    ust_ref[...] = carry_ref[...]


def _ffn_step_kernel(x_ref, p0_ref, p1_ref, g2_ref, wup_ref, cw_ref, cb_ref, wdn_ref, out_ref, u_ref):
    def u_prev(u, c0, c1):
        return p0_ref[:, c0:c1], p1_ref[:, c0:c1]

    def u_sink(u, c0):
        u_ref[:, c0:c0 + FF_CHUNK] = u

    _ffn_body(x_ref[...], g2_ref, wup_ref, cw_ref, cb_ref, wdn_ref, out_ref, u_prev, u_sink)


_FFN_W_SPECS = lambda: [_resident((1, D_MODEL)), _resident((D_MODEL, 2 * D_FF)), _resident((3, 2 * D_FF)),
                        _resident((1, 2 * D_FF)), _resident((D_FF, D_MODEL))]


def _merge_ffn_seq(merge_args, ffn_w, batch, seq, tm, cache_rider=None):
    m = merge_args[0].shape[0]
    nt = seq // tm
    n_tiles = batch * nt
    merge_tile = lambda s: jnp.minimum(s, n_tiles - 1)
    ffn_tile = lambda s: jnp.maximum(s - 1, 0)
    yw = merge_args[7]
    in_specs = _merge_specs(yw, tm, merge_tile) + _FFN_W_SPECS()
    out_specs = [pl.BlockSpec((tm, D_MODEL), lambda s: (ffn_tile(s), 0)),
                 pl.BlockSpec((None, 8, 2 * D_FF), lambda s: (ffn_tile(s) // nt, 0, 0))]
    out_shape = [jax.ShapeDtypeStruct((m, D_MODEL), F32), jax.ShapeDtypeStruct((batch, 8, 2 * D_FF), F32)]
    cache_group, cache_args = cache_rider if cache_rider is not None else (None, ())
    if cache_rider is not None:
        assert cache_args[5].shape[0] == n_tiles, "one sample sequence per row tile"
        c_in, c_out, c_shape = _cache_attn_specs(cache_args, 1, merge_tile)
        in_specs, out_specs, out_shape = in_specs + c_in, out_specs + c_out, out_shape + c_shape
    return pl.pallas_call(
        functools.partial(_ffn_seq_kernel, cache_group=cache_group, yw_channels_first=yw.ndim == 3,
                          tiles_per_seq=nt, n_tiles=n_tiles),
        out_shape=tuple(out_shape),
        grid=(n_tiles + 1,),
        in_specs=in_specs,
        out_specs=tuple(out_specs),
        scratch_shapes=[pltpu.VMEM((8, 2 * D_FF), F32), pltpu.VMEM((tm, D_MODEL), F32)],
        compiler_params=_cparams("arbitrary"), name="merge_ffn_seq",
    )(*merge_args, *ffn_w, *cache_args)


def _ffn_step(x2d, conv_prev, ln2_g, wup, cw, cb, wdn, tm):
    m = x2d.shape[0]
    w_specs = _FFN_W_SPECS()
    row = lambda w: pl.BlockSpec((tm, w), lambda i: (i, 0))
    return pl.pallas_call(
        _ffn_step_kernel,
        out_shape=(jax.ShapeDtypeStruct((m, D_MODEL), F32), jax.ShapeDtypeStruct((m, 2 * D_FF), F32)),
        grid=(m // tm,),
        in_specs=[row(D_MODEL), row(2 * D_FF), row(2 * D_FF)] + w_specs,
        out_specs=(row(D_MODEL), row(2 * D_FF)),
        compiler_params=_cparams("parallel"), name="ffn_step",
    )(x2d, conv_prev[0], conv_prev[1], ln2_g, wup, cw, cb, wdn)


def _pad_rows(w, before, total):
    return jnp.zeros((total, w.shape[1]), w.dtype).at[before:before + w.shape[0]].set(w)


def kernel(x_prompt, x_sample, cache_k_w128, cache_v_w128, cache_k_w512, cache_v_w512, cache_k_w2048, cache_v_w2048,
           state_rwkv_shift, state_rwkv_wkv, state_ffn_conv, ln1_g, w_in, q_norm_g, k_norm_g, w_attn_out,
           rwkv_mu, rwkv_w0, rwkv_w2, rwkv_a0, rwkv_a2, rwkv_g2, rwkv_k_k, rwkv_k_a, rwkv_r_k, rwkv_gn_g, rwkv_gn_b,
           w_rwkv_out, w_o, ln2_g, w_up, conv_w, conv_b, w_down):
    batch, seq, _ = x_prompt.shape
    nsamp = x_sample.shape[0]
    assert seq % SEG == 0 and x_sample.shape[1] == 1 and batch * RWKV_HEADS * WKV_Q == LANES
    assert ln1_g.shape[0] == 1, "single layer"
    lyr = 0

    w_in_bf = w_in[lyr].astype(BF16)
    bd768 = bd512 = _block_diag_ones(SEG_TILE)
    gq = jnp.tile(q_norm_g[lyr], ATT_WIDTH // HEAD_DIM)[None]
    gk = jnp.tile(k_norm_g[lyr], ATT_WIDTH // HEAD_DIM)[None]
    row1 = lambda a: a.reshape(1, -1)
    prep_w = (row1(rwkv_mu[lyr]), row1(rwkv_w0[lyr]), row1(rwkv_a0[lyr]),
              *_split_bf16(_pad_rows(rwkv_w2[lyr], 0, DECAY_LORA + ICLR_LORA)),
              *_split_bf16(_pad_rows(rwkv_a2[lyr], DECAY_LORA, DECAY_LORA + ICLR_LORA)),
              *_split_bf16(rwkv_g2[lyr]),
              row1(rwkv_k_k[lyr]), row1(rwkv_k_a[lyr]), row1(rwkv_r_k[lyr]), bd512)
    merge_w = (row1(rwkv_gn_g[lyr]), row1(rwkv_gn_b[lyr]), bd512, w_attn_out[lyr].astype(BF16),
               w_rwkv_out[lyr].astype(BF16), w_o[lyr].astype(BF16))
    ffn_w = (row1(ln2_g[lyr]), w_up[lyr].astype(BF16), conv_w[lyr], row1(conv_b[lyr]), w_down[lyr].astype(BF16))

    xs = x_sample.reshape(nsamp, D_MODEL)
    qs, ks, vs, prs, gates_s = _proj(xs, row1(ln1_g[lyr]), w_in_bf, bd768, gq, gk, tm=nsamp)
    caches = [c[lyr].transpose(0, 2, 3, 1).reshape(nsamp, GROUP_WIDTH, c.shape[2])
              for c in (cache_k_w128, cache_v_w128, cache_k_w512, cache_v_w512, cache_k_w2048, cache_v_w2048)]
    for g, (win, dil) in enumerate(ATT_GROUPS):
        assert caches[2 * g].shape[2] == BLOCK * dil, "window caches must be full"
    ks_t, vs_t = ks.T, vs.T
    cache_args = lambda g: (qs, ks, vs, ks_t, vs_t, caches[2 * g], caches[2 * g + 1])
    big, mid = N_GROUPS - 1, N_GROUPS - 2
    cache_done = {}

    xp = x_prompt.reshape(batch * seq, D_MODEL)
    keep_max = min(max(w for w, _ in ATT_GROUPS), seq)
    first_half = _proj(xp, row1(ln1_g[lyr]), w_in_bf, bd768, gq, gk, tm=PROJ_ROWS,
                       prep=(prep_w, batch, seq, keep_max),
                       cache_rider=(mid, cache_args(mid)))
    qn, kn, v, pr_last, gates = first_half[:5]
    na, dec, nb, kp, rp, rv, vkr, bonus, gg = first_half[5:5 + N_PREP_OUT]
    k_tail, v_tail = first_half[5 + N_PREP_OUT:5 + N_SEQ_OUT]
    cache_done[mid] = first_half[5 + N_SEQ_OUT:]
    att = [_attn_prompt(qn, kn, v, g, batch, seq) for g in range(N_GROUPS)]
    yw, s_fin = _wkv_seq(dec, nb, kp, na, rp, rv)
    merge_args = (xp, *[a[0] for a in att], *[a[1] for a in att], yw, vkr, bonus, gg, gates, *merge_w)
    ffn_rows = seq * batch // nsamp
    yp, ust, *cache_done[big] = _merge_ffn_seq(merge_args, ffn_w, batch, seq, tm=ffn_rows,
                                               cache_rider=(big, cache_args(big)))

    y_prompt = yp.reshape(batch, seq, D_MODEL)
    kv_prompt = []
    for g, (win, _) in enumerate(ATT_GROUPS):
        keep = min(win, seq)
        for tail in (k_tail, v_tail):
            c = tail[:, g * GROUP_WIDTH:(g + 1) * GROUP_WIDTH, keep_max - keep:]
            kv_prompt.append(c.reshape(batch, HEADS_PER_GROUP, HEAD_DIM, keep).transpose(0, 3, 1, 2)[None])
    shift_prompt = pr_last[None, :, -1]
    wkv_prompt = (s_fin.reshape(RWKV_HEAD, WKV_SUB, WKV_Q, batch, RWKV_HEADS)
                  .transpose(3, 4, 2, 1, 0).reshape(1, batch, RWKV_HEADS, RWKV_HEAD, RWKV_HEAD))
    conv_prompt = ust[None, :, 6:8]

    kv_sample, o_s, lse_s = [], [], []
    for g, (win, dil) in enumerate(ATT_GROUPS):
        k_out, v_out, o_g, lse_g = (cache_done[g] if g in cache_done
                                    else _cache_attn(cache_args(g), g, bb=SMALL_CACHE_BLOCK))
        kv_sample += [c.reshape(nsamp, HEADS_PER_GROUP, HEAD_DIM, win).transpose(0, 3, 1, 2)[None]
                      for c in (k_out, v_out)]
        o_s.append(o_g)
        lse_s.append(lse_g)
    na, dec, nb, kp, rp, rv, vkr, bonus, gg = _prep_step(prs, state_rwkv_shift[lyr], prep_w, tm=nsamp)
    state_t = state_rwkv_wkv[lyr].transpose(1, 2, 3, 0)
    yw_s, state_new = _wkv_step(state_t, dec, nb, kp, na, rp, rv)
    x2s = _merge(xs, o_s, lse_s, yw_s[None], vkr, bonus, gg, gates_s, *merge_w, tm=nsamp)
    conv_prev = state_ffn_conv[lyr]
    ys, u_s = _ffn_step(x2s, (conv_prev[:, 0], conv_prev[:, 1]), *ffn_w, tm=nsamp)

    y_sample = ys.reshape(nsamp, 1, D_MODEL)
    shift_sample = prs[None]
    wkv_sample = state_new.transpose(3, 0, 1, 2)[None]
    conv_sample = jnp.stack([conv_prev[:, 1], u_s], axis=1)[None]

    return (y_prompt, y_sample,
            kv_prompt[0], kv_sample[0], kv_prompt[1], kv_sample[1],
            kv_prompt[2], kv_sample[2], kv_prompt[3], kv_sample[3],
            kv_prompt[4], kv_sample[4], kv_prompt[5], kv_sample[5],
            shift_prompt, shift_sample, wkv_prompt, wkv_sample, conv_prompt, conv_sample)
```
